```python
import jax, jax.numpy as jnp
from jax import lax
import numpy as np

D_MODEL = 2048
BATCH = 4
SEQ = 2048
DEPTH = 2

N_A = max(1, DEPTH // 2)
N_B = DEPTH - N_A

SSM_EXPAND = 2
D_INNER = SSM_EXPAND * D_MODEL
SSM_HEAD_DIM = 64
SSM_HEADS = D_INNER // SSM_HEAD_DIM
SSM_GROUPS = 8
SSM_HEADS_PER_GROUP = SSM_HEADS // SSM_GROUPS
SSM_STATE = 128
CONV_WIDTH = 4
CHUNK = 128
CONV_DIM = D_INNER + 2 * SSM_GROUPS * SSM_STATE
IN_PROJ_DIM = D_INNER + CONV_DIM + SSM_HEADS

SB_HEADS = 16
SB_HEAD_DIM = D_MODEL // SB_HEADS
Q_BLOCK = 128

N_GROUPS = 4
EXPERTS_PER_GROUP = 4
N_EXPERTS = N_GROUPS * EXPERTS_PER_GROUP
TOP_K = 2
D_EXPERT = D_MODEL // 4

EPS = 1e-5
F32 = jnp.float32

kernel_name = "yoco_mamba2_stickbreaking_hiermoe"


def rms_norm(x, g):
    xf = x.astype(F32)
    y = xf * lax.rsqrt(jnp.mean(xf * xf, axis=-1, keepdims=True) + EPS)
    return (y * g.astype(F32)).astype(x.dtype)


def causal_dwconv(x, w, b):
    out = lax.conv_general_dilated(
        x, w[:, None, :].astype(x.dtype), window_strides=(1,),
        padding=[(CONV_WIDTH - 1, 0)], dimension_numbers=("NWC", "WIO", "NWC"),
        feature_group_count=x.shape[-1])
    return out + b.astype(x.dtype)


def ssd_chunked(x, a, b, c):
    bsz, L = x.shape[0], x.shape[1]
    nc = L // CHUNK
    x = x.reshape(bsz, nc, CHUNK, SSM_GROUPS, SSM_HEADS_PER_GROUP, SSM_HEAD_DIM)
    b = b.reshape(bsz, nc, CHUNK, SSM_GROUPS, SSM_STATE)
    c = c.reshape(bsz, nc, CHUNK, SSM_GROUPS, SSM_STATE)
    a = a.reshape(bsz, nc, CHUNK, SSM_GROUPS, SSM_HEADS_PER_GROUP).transpose(0, 3, 4, 1, 2)
    a_cs = jnp.cumsum(a, axis=-1)
    idx = jnp.arange(CHUNK)
    causal = idx[:, None] >= idx[None, :]
    seg = a_cs[..., :, None] - a_cs[..., None, :]
    decay = jnp.exp(jnp.where(causal, seg, -jnp.inf))
    cb = jnp.einsum("bclgn,bcsgn->bgcls", c, b)
    m = cb[:, :, None] * decay
    y_diag = jnp.einsum("bgrcls,bcsgrp->bclgrp", m, x)
    decay_states = jnp.exp(a_cs[..., -1:] - a_cs).transpose(0, 3, 4, 1, 2)
    xd = x * decay_states[..., None]
    states = jnp.einsum("bclgn,bclgrp->bcgrpn", b, xd)
    chunk_decay = jnp.exp(a_cs[..., -1]).transpose(3, 0, 1, 2)

    def step(h, inp):
        dec, st = inp
        return h * dec[..., None, None] + st, h

    h0 = jnp.zeros(states.shape[:1] + states.shape[2:], states.dtype)
    _, prev = lax.scan(step, h0, (chunk_decay.astype(states.dtype), states.transpose(1, 0, 2, 3, 4, 5)))
    out_decay = jnp.exp(a_cs).transpose(0, 3, 4, 1, 2)[..., None]
    y_off = jnp.einsum("bclgn,cbgrpn->bclgrp", c, prev) * out_decay
    y = y_diag + y_off
    return y.reshape(bsz, L, SSM_GROUPS, SSM_HEADS_PER_GROUP, SSM_HEAD_DIM)


def gated_group_rmsnorm(y, z, w):
    bsz, L, _ = y.shape
    yf = (y.astype(F32) * jax.nn.silu(z.astype(F32))).reshape(bsz, L, SSM_GROUPS, D_INNER // SSM_GROUPS)
    yf = yf * lax.rsqrt(jnp.mean(yf * yf, axis=-1, keepdims=True) + EPS)
    return (yf.reshape(bsz, L, D_INNER) * w.astype(F32)).astype(y.dtype)


def mamba2_mixer(h, w_in, conv_w, conv_b, dt_bias, a_log, d_skip, norm_w, w_out):
    bsz, L, _ = h.shape
    zxbcdt = h @ w_in
    z = zxbcdt[..., :D_INNER]
    xbc = zxbcdt[..., D_INNER:D_INNER + CONV_DIM]
    dt = zxbcdt[..., D_INNER + CONV_DIM:]
    xbc = jax.nn.silu(causal_dwconv(xbc, conv_w, conv_b))
    gn = SSM_GROUPS * SSM_STATE
    xs = xbc[..., :D_INNER].reshape(bsz, L, SSM_GROUPS, SSM_HEADS_PER_GROUP, SSM_HEAD_DIM)
    bm = xbc[..., D_INNER:D_INNER + gn].reshape(bsz, L, SSM_GROUPS, SSM_STATE)
    cm = xbc[..., D_INNER + gn:].reshape(bsz, L, SSM_GROUPS, SSM_STATE)
    dt = jax.nn.softplus((dt + dt_bias).astype(F32)).reshape(bsz, L, SSM_GROUPS, SSM_HEADS_PER_GROUP)
    a = -jnp.exp(a_log.astype(F32)).reshape(SSM_GROUPS, SSM_HEADS_PER_GROUP)
    y = ssd_chunked(xs * dt[..., None], dt * a, bm, cm)
    y = y + xs * d_skip.reshape(SSM_GROUPS, SSM_HEADS_PER_GROUP, 1)
    y = gated_group_rmsnorm(y.reshape(bsz, L, D_INNER), z, norm_w)
    return y @ w_out


def stick_breaking_attention(q, k, v):
    S = q.shape[2]
    scale = SB_HEAD_DIM ** -0.5
    outs = []
    for i in range(S // Q_BLOCK):
        start, end = i * Q_BLOCK, (i + 1) * Q_BLOCK
        qb, kb, vb = q[:, :, start:end], k[:, :, :end], v[:, :, :end]
        z = jnp.einsum("bhqd,bhkd->bhqk", qb, kb).astype(F32) * scale
        q_pos = start + jnp.arange(Q_BLOCK)
        k_pos = jnp.arange(end)
        mask = k_pos[None, :] < q_pos[:, None]
        log_keep = jnp.where(mask, -jax.nn.softplus(z), 0.0)
        suffix = lax.cumsum(log_keep, axis=3, reverse=True) - log_keep
        attn = jnp.where(mask, jnp.exp(jax.nn.log_sigmoid(z) + suffix), 0.0)
        outs.append(jnp.einsum("bhqk,bhkd->bhqd", attn.astype(v.dtype), vb))
    return jnp.concatenate(outs, axis=2)


def hier_moe(h, w_coarse, b_coarse, w_fine, b_fine, w_gate, w_up, w_down):
    bsz, L, _ = h.shape
    t = h.reshape(bsz * L, D_MODEL)
    p_group = jax.nn.softmax((t @ w_coarse + b_coarse).astype(F32), axis=-1)
    p_g, g_idx = lax.top_k(p_group, 1)
    fine_all = jnp.einsum("td,dge->tge", t, w_fine) + b_fine
    fine = jnp.take_along_axis(fine_all, g_idx[:, :, None], axis=1)[:, 0]
    p_fine = jax.nn.softmax(fine.astype(F32), axis=-1)
    p_e, e_idx = lax.top_k(p_fine, TOP_K)
    p_e = p_e / jnp.sum(p_e, axis=-1, keepdims=True)
    gate_w = p_g * p_e
    expert_id = g_idx * EXPERTS_PER_GROUP + e_idx
    gates = jnp.sum(jax.nn.one_hot(expert_id, N_EXPERTS, dtype=F32) * gate_w[..., None], axis=1)
    hid = jax.nn.silu(jnp.einsum("td,edf->tef", t, w_gate)) * jnp.einsum("td,edf->tef", t, w_up)
    hid = hid * gates[:, :, None].astype(hid.dtype)
    y = jnp.einsum("tef,efd->td", hid, w_down)
    return y.reshape(bsz, L, D_MODEL)


def setup_inputs(seed: int = 0) -> dict:
    key = jax.random.key(seed)
    ks = jax.random.split(key, 24)
    nrm = lambda k, shape, s: jax.random.normal(k, shape, F32) * s
    dt = jnp.exp(jax.random.uniform(ks[6], (N_A, SSM_HEADS), F32, np.log(1e-3), np.log(1e-1)))
    return {
        "x": jax.random.normal(ks[0], (BATCH, SEQ, D_MODEL), F32),
        "mix_norm": 1.0 + nrm(ks[1], (DEPTH, D_MODEL), 0.02),
        "ffn_norm": 1.0 + nrm(ks[2], (DEPTH, D_MODEL), 0.02),
        "ssm_w_in": nrm(ks[3], (N_A, D_MODEL, IN_PROJ_DIM), D_MODEL ** -0.5),
        "ssm_conv_w": nrm(ks[4], (N_A, CONV_WIDTH, CONV_DIM), CONV_WIDTH ** -0.5),
        "ssm_conv_b": nrm(ks[5], (N_A, CONV_DIM), 0.01),
        "ssm_dt_bias": dt + jnp.log(-jnp.expm1(-dt)),
        "ssm_a_log": jnp.log(jax.random.uniform(ks[7], (N_A, SSM_HEADS), F32, 1.0, 16.0)),
        "ssm_d": 1.0 + nrm(ks[8], (N_A, SSM_HEADS), 0.1),
        "ssm_norm_w": 1.0 + nrm(ks[9], (N_A, D_INNER), 0.02),
        "ssm_w_out": nrm(ks[10], (N_A, D_INNER, D_MODEL), D_INNER ** -0.5),
        "kv_norm": 1.0 + nrm(ks[11], (D_MODEL,), 0.02),
        "w_k": nrm(ks[12], (D_MODEL, D_MODEL), D_MODEL ** -0.5),
        "w_v": nrm(ks[13], (D_MODEL, D_MODEL), D_MODEL ** -0.5),
        "sb_w_q": nrm(ks[14], (N_B, D_MODEL, D_MODEL), D_MODEL ** -0.5),
        "sb_w_out": nrm(ks[15], (N_B, D_MODEL, D_MODEL), D_MODEL ** -0.5),
        "moe_w_coarse": nrm(ks[16], (DEPTH, D_MODEL, N_GROUPS), D_MODEL ** -0.5),
        "moe_b_coarse": nrm(ks[17], (DEPTH, N_GROUPS), 0.01),
        "moe_w_fine": nrm(ks[18], (DEPTH, D_MODEL, N_GROUPS, EXPERTS_PER_GROUP), D_MODEL ** -0.5),
        "moe_b_fine": nrm(ks[19], (DEPTH, N_GROUPS, EXPERTS_PER_GROUP), 0.01),
        "moe_w_gate": nrm(ks[20], (DEPTH, N_EXPERTS, D_MODEL, D_EXPERT), D_MODEL ** -0.5),
        "moe_w_up": nrm(ks[21], (DEPTH, N_EXPERTS, D_MODEL, D_EXPERT), D_MODEL ** -0.5),
        "moe_w_down": nrm(ks[22], (DEPTH, N_EXPERTS, D_EXPERT, D_MODEL), D_EXPERT ** -0.5),
        "final_norm": 1.0 + nrm(ks[23], (D_MODEL,), 0.02),
    }


def reference(x, mix_norm, ffn_norm, ssm_w_in, ssm_conv_w, ssm_conv_b, ssm_dt_bias, ssm_a_log,
              ssm_d, ssm_norm_w, ssm_w_out, kv_norm, w_k, w_v, sb_w_q, sb_w_out,
              moe_w_coarse, moe_b_coarse, moe_w_fine, moe_b_fine, moe_w_gate, moe_w_up,
              moe_w_down, final_norm):
    bsz, S, _ = x.shape
    h = x
    k_sh = None
    v_sh = None
    for layer in range(DEPTH):
        hn = rms_norm(h, mix_norm[layer])
        if layer < N_A:
            mix = mamba2_mixer(hn, ssm_w_in[layer], ssm_conv_w[layer], ssm_conv_b[layer],
                               ssm_dt_bias[layer], ssm_a_log[layer], ssm_d[layer],
                               ssm_norm_w[layer], ssm_w_out[layer])
        else:
            j = layer - N_A
            q = (hn @ sb_w_q[j]).reshape(bsz, S, SB_HEADS, SB_HEAD_DIM).transpose(0, 2, 1, 3)
            o = stick_breaking_attention(q, k_sh, v_sh)
            mix = o.transpose(0, 2, 1, 3).reshape(bsz, S, D_MODEL) @ sb_w_out[j]
        h = h + mix.astype(h.dtype)
        ff = hier_moe(rms_norm(h, ffn_norm[layer]), moe_w_coarse[layer], moe_b_coarse[layer],
                      moe_w_fine[layer], moe_b_fine[layer], moe_w_gate[layer],
                      moe_w_up[layer], moe_w_down[layer])
        h = h + ff.astype(h.dtype)
        if layer == N_A - 1:
            hkv = rms_norm(h, kv_norm)
            k_sh = (hkv @ w_k).reshape(bsz, S, SB_HEADS, SB_HEAD_DIM).transpose(0, 2, 1, 3)
            v_sh = (hkv @ w_v).reshape(bsz, S, SB_HEADS, SB_HEAD_DIM).transpose(0, 2, 1, 3)
    return rms_norm(h, final_norm)
```

```python
import functools

import jax
import jax.numpy as jnp
from jax import lax
from jax.experimental import pallas as pl
from jax.experimental.pallas import tpu as pltpu

F32 = jnp.float32
BF16 = jnp.bfloat16
EPS = 1e-5

D_MODEL = 2048
D_INNER = 4096
SSM_HEAD_DIM = 64
SSM_HEADS = 64
SSM_GROUPS = 8
SSM_HEADS_PER_GROUP = 8
SSM_STATE = 128
GROUP_WIDTH = D_INNER // SSM_GROUPS
CONV_WIDTH = 4
CONV_DIM = D_INNER + 2 * SSM_GROUPS * SSM_STATE
IN_PROJ_DIM = D_INNER + CONV_DIM + SSM_HEADS
SB_HEADS = 16
SB_HEAD_DIM = 128
N_GROUPS = 4
EXPERTS_PER_GROUP = 4
N_EXPERTS = 16
D_EXPERT = 512

LANES = 128
SUBLANES = 8
IN_PROJ_PAD = 10368
DT_COL_BLOCK = (D_INNER + CONV_DIM) // LANES
SSD_CHUNK = 128
ATT_BLOCK = 256
VMEM_LIMIT = 56 * 1024 * 1024


def _dot(a, b):
    return jnp.dot(a, b, preferred_element_type=F32)


def _dot_nt(a, b):
    return lax.dot_general(a, b, (((1,), (1,)), ((), ())), preferred_element_type=F32)


def _split3(v):
    v1 = v.astype(BF16)
    r = v - v1.astype(F32)
    v2 = r.astype(BF16)
    r = r - v2.astype(F32)
    return v1, v2, r.astype(BF16)


def _dot3_left(m, v):
    v1, v2, v3 = _split3(v)
    return _dot(m, v1) + _dot(m, v2) + _dot(m, v3)


def _dot3_right(v, m):
    v1, v2, v3 = _split3(v)
    return _dot(v1, m) + _dot(v2, m) + _dot(v3, m)


def _softplus(x):
    return jnp.maximum(x, 0.0) + jnp.log1p(jnp.exp(-jnp.abs(x)))


def _silu(x):
    return x * (1.0 / (1.0 + jnp.exp(-x)))


def _rms_scale(x):
    return lax.rsqrt(jnp.mean(x * x, axis=-1, keepdims=True) + EPS)


def _params(*sem):
    return pltpu.CompilerParams(dimension_semantics=sem, vmem_limit_bytes=VMEM_LIMIT)


def _norm_matmul_kernel(x_ref, g_ref, w_ref, o_ref, xn_ref):
    @pl.when(pl.program_id(1) == 0)
    def _():
        x = x_ref[...]
        xn_ref[...] = (x * _rms_scale(x) * g_ref[...]).astype(BF16)

    o_ref[...] = _dot(xn_ref[...], w_ref[...]).astype(o_ref.dtype)


def _norm_matmul(x, g, w, *, tm, tn, out_dtype):
    m, k = x.shape
    n = w.shape[1]
    return pl.pallas_call(
        _norm_matmul_kernel,
        out_shape=jax.ShapeDtypeStruct((m, n), out_dtype),
        grid=(m // tm, n // tn),
        in_specs=[
            pl.BlockSpec((tm, k), lambda i, j: (i, 0)),
            pl.BlockSpec((1, k), lambda i, j: (0, 0)),
            pl.BlockSpec((k, tn), lambda i, j: (0, j)),
        ],
        out_specs=pl.BlockSpec((tm, tn), lambda i, j: (i, j)),
        scratch_shapes=[pltpu.VMEM((tm, k), BF16)],
        compiler_params=_params("arbitrary", "arbitrary"),
        name="norm_matmul",
    )(x, g.reshape(1, k), w)


def _matmul_res_kernel(x_ref, w_ref, r_ref, o_ref):
    o_ref[...] = r_ref[...] + _dot(x_ref[...], w_ref[...])


def _matmul_res(x, w, res, *, tm, tn):
    m, k = x.shape
    n = w.shape[1]
    return pl.pallas_call(
        _matmul_res_kernel,
        out_shape=jax.ShapeDtypeStruct((m, n), F32),
        grid=(m // tm, n // tn),
        in_specs=[
            pl.BlockSpec((tm, k), lambda i, j: (i, 0)),
            pl.BlockSpec((k, tn), lambda i, j: (0, j)),
            pl.BlockSpec((tm, tn), lambda i, j: (i, j)),
        ],
        out_specs=pl.BlockSpec((tm, tn), lambda i, j: (i, j)),
        compiler_params=_params("arbitrary", "arbitrary"),
        name="matmul_res",
    )(x, w, res)


def _ssd_kernel(z_ref, x_ref, b_ref, c_ref, dt_ref,
                wx_ref, wb_ref, wc_ref, bx_ref, bb_ref, bc_ref,
                dtb_ref, alog_ref, dskip_ref, nw_ref,
                o_ref,
                h_ref, xbuf, bbuf, cbuf):
    L = SSD_CHUNK
    g = pl.program_id(1)
    c = pl.program_id(2)
    halo = SUBLANES

    @pl.when(c == 0)
    def _():
        h_ref[...] = jnp.zeros_like(h_ref)
        xbuf[0:halo, :] = jnp.zeros((halo, GROUP_WIDTH), F32)
        bbuf[0:halo, :] = jnp.zeros((halo, SSM_STATE), F32)
        cbuf[0:halo, :] = jnp.zeros((halo, SSM_STATE), F32)

    def conv_silu(raw_ref, buf, w_ref, bias_ref):
        buf[halo:halo + L, :] = raw_ref[...]
        acc = bias_ref[...]
        for k in range(CONV_WIDTH):
            off = halo - (CONV_WIDTH - 1) + k
            acc = acc + w_ref[k:k + 1, :] * buf[off:off + L, :]
        buf[0:halo, :] = buf[L:L + halo, :]
        return _silu(acc)

    xs = conv_silu(x_ref, xbuf, wx_ref, bx_ref)
    bs = conv_silu(b_ref, bbuf, wb_ref, bb_ref)
    cs = conv_silu(c_ref, cbuf, wc_ref, bc_ref)

    src = lax.broadcasted_iota(jnp.int32, (LANES, GROUP_WIDTH), 0)
    dst = lax.broadcasted_iota(jnp.int32, (LANES, GROUP_WIDTH), 1)
    head = lax.shift_right_logical(dst, SSM_HEAD_DIM.bit_length() - 1)
    expand = (src == g * SSM_HEADS_PER_GROUP + head).astype(BF16)
    dt = _softplus(_dot3_right(dt_ref[...], expand) + dtb_ref[...])
    a = dt * (-jnp.exp(alog_ref[...]))

    row = lax.broadcasted_iota(jnp.int32, (L, L), 0)
    col = lax.broadcasted_iota(jnp.int32, (L, L), 1)
    causal = row >= col
    tril = causal.astype(BF16)
    acs = _dot3_left(tril, a)

    hrow = lax.broadcasted_iota(jnp.int32, (SUBLANES, GROUP_WIDTH), 0)
    hcol = lax.broadcasted_iota(jnp.int32, (SUBLANES, GROUP_WIDTH), 1)
    pick = (hcol == hrow * SSM_HEAD_DIM).astype(BF16)
    a1, a2, a3 = _split3(acs)
    acs_rows = _dot_nt(pick, a1) + _dot_nt(pick, a2) + _dot_nt(pick, a3)

    xdt = xs * dt
    xdt_b = xdt.astype(BF16)
    bs_b = bs.astype(BF16)
    cs_b = cs.astype(BF16)
    cb = _dot_nt(cs_b, bs_b)

    parts = []
    for r in range(SSM_HEADS_PER_GROUP):
        lo = r * SSM_HEAD_DIM
        seg = acs[:, lo:lo + 1] - acs_rows[r:r + 1, :]
        decay = jnp.exp(jnp.where(causal, seg, -jnp.inf))
        m = (cb * decay).astype(BF16)
        parts.append(_dot(m, xdt_b[:, lo:lo + SSM_HEAD_DIM]))
    y_diag = jnp.concatenate(parts, axis=1)

    h_prev = h_ref[...]
    y_off = _dot(cs_b, h_prev.astype(BF16)) * jnp.exp(acs)
    a_last = acs[L - 1:L, :]
    xd = (xdt * jnp.exp(a_last - acs)).astype(BF16)
    h_ref[...] = h_prev * jnp.exp(a_last) + _dot(bs.T.astype(BF16), xd)

    y = y_diag + y_off + xs * dskip_ref[...]
    yz = y * _silu(z_ref[...])
    o_ref[...] = (yz * _rms_scale(yz) * nw_ref[...]).astype(o_ref.dtype)


def _ssd(zx, conv_w, conv_b, dtb_e, alog_e, dskip_e, norm_w, *, batch, seq):
    L = SSD_CHUNK
    nc = seq // L
    gw = GROUP_WIDTH // LANES
    x_blk = D_INNER // GROUP_WIDTH
    b_blk = (2 * D_INNER) // SSM_STATE
    c_blk = b_blk + SSM_GROUPS
    cw_b = D_INNER // SSM_STATE
    cw_c = cw_b + SSM_GROUPS
    del gw

    def rows(b, g, c):
        return b * nc + c

    in_specs = [
        pl.BlockSpec((L, GROUP_WIDTH), lambda b, g, c: (rows(b, g, c), g)),
        pl.BlockSpec((L, GROUP_WIDTH), lambda b, g, c: (rows(b, g, c), x_blk + g)),
        pl.BlockSpec((L, SSM_STATE), lambda b, g, c: (rows(b, g, c), b_blk + g)),
        pl.BlockSpec((L, SSM_STATE), lambda b, g, c: (rows(b, g, c), c_blk + g)),
        pl.BlockSpec((L, LANES), lambda b, g, c: (rows(b, g, c), DT_COL_BLOCK)),
        pl.BlockSpec((CONV_WIDTH, GROUP_WIDTH), lambda b, g, c: (0, g)),
        pl.BlockSpec((CONV_WIDTH, SSM_STATE), lambda b, g, c: (0, cw_b + g)),
        pl.BlockSpec((CONV_WIDTH, SSM_STATE), lambda b, g, c: (0, cw_c + g)),
        pl.BlockSpec((1, GROUP_WIDTH), lambda b, g, c: (0, g)),
        pl.BlockSpec((1, SSM_STATE), lambda b, g, c: (0, cw_b + g)),
        pl.BlockSpec((1, SSM_STATE), lambda b, g, c: (0, cw_c + g)),
        pl.BlockSpec((1, GROUP_WIDTH), lambda b, g, c: (0, g)),
        pl.BlockSpec((1, GROUP_WIDTH), lambda b, g, c: (0, g)),
        pl.BlockSpec((1, GROUP_WIDTH), lambda b, g, c: (0, g)),
        pl.BlockSpec((1, GROUP_WIDTH), lambda b, g, c: (0, g)),
    ]
    cb2 = conv_b.reshape(1, CONV_DIM)
    return pl.pallas_call(
        _ssd_kernel,
        out_shape=jax.ShapeDtypeStruct((batch * seq, D_INNER), BF16),
        grid=(batch, SSM_GROUPS, nc),
        in_specs=in_specs,
        out_specs=pl.BlockSpec((L, GROUP_WIDTH), lambda b, g, c: (rows(b, g, c), g)),
        scratch_shapes=[
            pltpu.VMEM((SSM_STATE, GROUP_WIDTH), F32),
            pltpu.VMEM((L + SUBLANES, GROUP_WIDTH), F32),
            pltpu.VMEM((L + SUBLANES, SSM_STATE), F32),
            pltpu.VMEM((L + SUBLANES, SSM_STATE), F32),
        ],
        compiler_params=_params("arbitrary", "arbitrary", "arbitrary"),
        name="ssd",
    )(zx, zx, zx, zx, zx, conv_w, conv_w, conv_w, cb2, cb2, cb2,
      dtb_e, alog_e, dskip_e, norm_w.reshape(1, D_INNER))


def _attn_kernel(q_ref, k_ref, v_ref, o_ref):
    tq = ATT_BLOCK
    i = pl.program_id(2)
    scale = SB_HEAD_DIM ** -0.5
    q = q_ref[...]
    row = lax.broadcasted_iota(jnp.int32, (tq, tq), 0)
    col = lax.broadcasted_iota(jnp.int32, (tq, tq), 1)
    later = (row > col).astype(BF16)
    strict = col < row

    def block(j, carry, acc, diagonal):
        start = pl.multiple_of(j * tq, tq)
        kj = k_ref[pl.ds(start, tq), :]
        vj = v_ref[pl.ds(start, tq), :]
        z = _dot_nt(q, kj) * scale
        sp = _softplus(z)
        log_keep = jnp.where(strict, -sp, 0.0) if diagonal else -sp
        hi = log_keep.astype(BF16)
        lo = (log_keep - hi.astype(F32)).astype(BF16)
        suffix = _dot(hi, later) + _dot(lo, later) + carry
        w = jnp.exp((z - sp) + suffix)
        if diagonal:
            w = jnp.where(strict, w, 0.0)
        acc = acc + _dot(w.astype(BF16), vj)
        carry = carry + jnp.sum(log_keep, axis=1, keepdims=True)
        return carry, acc

    carry, acc = block(i, jnp.zeros((tq, 1), F32), jnp.zeros((tq, SB_HEAD_DIM), F32), True)

    def body(t, state):
        return block(i - 1 - t, state[0], state[1], False)

    carry, acc = lax.fori_loop(0, i, body, (carry, acc))
    o_ref[...] = acc.astype(o_ref.dtype)


def _attention(q, kv, *, batch, seq):
    tq = ATT_BLOCK
    nq = seq // tq
    return pl.pallas_call(
        _attn_kernel,
        out_shape=jax.ShapeDtypeStruct((batch * seq, D_MODEL), BF16),
        grid=(batch, SB_HEADS, nq),
        in_specs=[
            pl.BlockSpec((tq, SB_HEAD_DIM), lambda b, h, i: (b * nq + i, h)),
            pl.BlockSpec((seq, SB_HEAD_DIM), lambda b, h, i: (b, h)),
            pl.BlockSpec((seq, SB_HEAD_DIM), lambda b, h, i: (b, SB_HEADS + h)),
        ],
        out_specs=pl.BlockSpec((tq, SB_HEAD_DIM), lambda b, h, i: (b * nq + i, h)),
        compiler_params=_params("arbitrary", "arbitrary", "arbitrary"),
        name="sb_attention",
    )(q, kv, kv)


ROUTER_FINE0 = N_GROUPS


def _router_kernel(h_ref, g_ref, w_ref, b_ref, hn_ref, gates_ref):
    x = h_ref[...]
    hn = x * _rms_scale(x) * g_ref[...]
    hn_ref[...] = hn.astype(BF16)
    x1 = hn.astype(BF16)
    x2 = (hn - x1.astype(F32)).astype(BF16)
    w = w_ref[...]
    w1 = w.astype(BF16)
    w2 = (w - w1.astype(F32)).astype(BF16)
    logits = _dot(x1, w1) + _dot(x2, w1) + _dot(x1, w2) + b_ref[...]

    lane_i = lax.broadcasted_iota(jnp.int32, logits.shape, 1)
    lane = lane_i.astype(F32)
    neg = -jnp.inf

    def first_argmax(v, vmax):
        return jnp.min(jnp.where(v == vmax, lane, float(LANES)), axis=1, keepdims=True)

    coarse = jnp.where(lane_i < N_GROUPS, logits, neg)
    cmax = jnp.max(coarse, axis=1, keepdims=True)
    p_g = 1.0 / jnp.sum(jnp.exp(coarse - cmax), axis=1, keepdims=True)
    g_idx = first_argmax(coarse, cmax)
    fine_lo = ROUTER_FINE0 + g_idx * EXPERTS_PER_GROUP
    in_group = (lane >= fine_lo) & (lane < fine_lo + EXPERTS_PER_GROUP)
    fine = jnp.where(in_group, logits, neg)
    m1 = jnp.max(fine, axis=1, keepdims=True)
    denom = jnp.sum(jnp.exp(fine - m1), axis=1, keepdims=True)
    i1 = first_argmax(fine, m1)
    rest = jnp.where(lane == i1, neg, fine)
    m2 = jnp.max(rest, axis=1, keepdims=True)
    i2 = first_argmax(rest, m2)
    p1 = 1.0 / denom
    p2 = jnp.exp(m2 - m1) / denom
    norm = p1 + p2
    gates_ref[...] = (jnp.where(lane == i1, p_g * (p1 / norm), 0.0)
                      + jnp.where(lane == i2, p_g * (p2 / norm), 0.0))


def _router(h, g, w_router, b_router, *, tm):
    m, k = h.shape
    return pl.pallas_call(
        _router_kernel,
        out_shape=(jax.ShapeDtypeStruct((m, k), BF16), jax.ShapeDtypeStruct((m, LANES), F32)),
        grid=(m // tm,),
        in_specs=[
            pl.BlockSpec((tm, k), lambda i: (i, 0)),
            pl.BlockSpec((1, k), lambda i: (0, 0)),
            pl.BlockSpec((k, LANES), lambda i: (0, 0)),
            pl.BlockSpec((1, LANES), lambda i: (0, 0)),
        ],
        out_specs=(pl.BlockSpec((tm, k), lambda i: (i, 0)),
                   pl.BlockSpec((tm, LANES), lambda i: (i, 0))),
        compiler_params=_params("arbitrary"),
        name="moe_router",
    )(h, g.reshape(1, k), w_router, b_router)


def _moe_dense_kernel(x_ref, gates_ref, wg_ref, wu_ref, wd_ref, r_ref, o_ref):
    e = pl.program_id(1)

    @pl.when((e == 0) & (pl.program_id(2) == 0))
    def _():
        o_ref[...] = r_ref[...]

    x = x_ref[...]
    lane = lax.broadcasted_iota(jnp.int32, gates_ref.shape, 1)
    gate = jnp.sum(jnp.where(lane == ROUTER_FINE0 + e, gates_ref[...], 0.0), axis=1, keepdims=True)
    hid = _silu(_dot(x, wg_ref[0].astype(BF16))) * _dot(x, wu_ref[0].astype(BF16)) * gate
    o_ref[...] += _dot(hid.astype(BF16), wd_ref[0].astype(BF16))


def _moe_dense(hn, gates, w_gate, w_up, w_down, res, *, tm):
    m, k = hn.shape
    ne, _, f = w_gate.shape
    fh = f // 2
    return pl.pallas_call(
        _moe_dense_kernel,
        out_shape=jax.ShapeDtypeStruct((m, k), F32),
        grid=(m // tm, ne, 2),
        in_specs=[
            pl.BlockSpec((tm, k), lambda i, e, s: (i, 0)),
            pl.BlockSpec((tm, LANES), lambda i, e, s: (i, 0)),
            pl.BlockSpec((1, k, fh), lambda i, e, s: (e, 0, s)),
            pl.BlockSpec((1, k, fh), lambda i, e, s: (e, 0, s)),
            pl.BlockSpec((1, fh, k), lambda i, e, s: (e, s, 0)),
            pl.BlockSpec((tm, k), lambda i, e, s: (i, 0)),
        ],
        out_specs=pl.BlockSpec((tm, k), lambda i, e, s: (i, 0)),
        compiler_params=_params("arbitrary", "arbitrary", "arbitrary"),
        name="moe_dense",
    )(hn, gates, w_gate, w_up, w_down, res)


def _rmsnorm_kernel(x_ref, g_ref, o_ref):
    x = x_ref[...]
    o_ref[...] = x * _rms_scale(x) * g_ref[...]


def _rmsnorm(x, g, *, tm):
    m, k = x.shape
    return pl.pallas_call(
        _rmsnorm_kernel,
        out_shape=jax.ShapeDtypeStruct((m, k), F32),
        grid=(m // tm,),
        in_specs=[pl.BlockSpec((tm, k), lambda i: (i, 0)), pl.BlockSpec((1, k), lambda i: (0, 0))],
        out_specs=pl.BlockSpec((tm, k), lambda i: (i, 0)),
        compiler_params=_params("arbitrary"),
        name="final_rmsnorm",
    )(x, g.reshape(1, k))


def _router_weights(w_coarse, b_coarse, w_fine, b_fine):
    pad = LANES - N_GROUPS - N_EXPERTS
    w = jnp.concatenate([w_coarse, w_fine.reshape(D_MODEL, N_EXPERTS),
                         jnp.zeros((D_MODEL, pad), F32)], axis=1)
    b = jnp.concatenate([b_coarse, b_fine.reshape(N_EXPERTS), jnp.zeros((pad,), F32)])
    return w, b.reshape(1, LANES)


def _moe(h, norm_g, w_coarse, b_coarse, w_fine, b_fine, w_gate, w_up, w_down):
    w_r, b_r = _router_weights(w_coarse, b_coarse, w_fine, b_fine)
    hn, gates = _router(h, norm_g, w_r, b_r, tm=512)
    return _moe_dense(hn, gates, w_gate, w_up, w_down, h, tm=512)


def kernel(x, mix_norm, ffn_norm, ssm_w_in, ssm_conv_w, ssm_conv_b, ssm_dt_bias, ssm_a_log, ssm_d,
           ssm_norm_w, ssm_w_out, kv_norm, w_k, w_v, sb_w_q, sb_w_out, moe_w_coarse, moe_b_coarse,
           moe_w_fine, moe_b_fine, moe_w_gate, moe_w_up, moe_w_down, final_norm):
    batch, seq, d = x.shape
    h = x.reshape(batch * seq, d)

    w_in = jnp.pad(ssm_w_in[0], ((0, 0), (0, IN_PROJ_PAD - IN_PROJ_DIM))).astype(BF16)
    zx = _norm_matmul(h, mix_norm[0], w_in, tm=1024, tn=1152, out_dtype=F32)
    expand = lambda p: jnp.repeat(p, SSM_HEAD_DIM).reshape(1, D_INNER)
    yn = _ssd(zx, ssm_conv_w[0], ssm_conv_b[0], expand(ssm_dt_bias[0]), expand(ssm_a_log[0]),
              expand(ssm_d[0]), ssm_norm_w[0], batch=batch, seq=seq)
    h = _matmul_res(yn, ssm_w_out[0].astype(BF16), h, tm=1024, tn=512)
    h = _moe(h, ffn_norm[0], moe_w_coarse[0], moe_b_coarse[0], moe_w_fine[0], moe_b_fine[0],
             moe_w_gate[0], moe_w_up[0], moe_w_down[0])

    w_kv = jnp.concatenate([w_k, w_v], axis=1).astype(BF16)
    kv = _norm_matmul(h, kv_norm, w_kv, tm=1024, tn=1024, out_dtype=BF16)

    q = _norm_matmul(h, mix_norm[1], sb_w_q[0].astype(BF16), tm=1024, tn=1024, out_dtype=BF16)
    o = _attention(q, kv, batch=batch, seq=seq)
    h = _matmul_res(o, sb_w_out[0].astype(BF16), h, tm=1024, tn=512)
    h = _moe(h, ffn_norm[1], moe_w_coarse[1], moe_b_coarse[1], moe_w_fine[1], moe_b_fine[1],
             moe_w_gate[1], moe_w_up[1], moe_w_down[1])

    return _rmsnorm(h, final_norm, tm=512).reshape(batch, seq, d)
```

```python
import functools

import jax
import jax.numpy as jnp
from jax import lax
from jax.experimental import pallas as pl
from jax.experimental.pallas import tpu as pltpu

F32 = jnp.float32
BF16 = jnp.bfloat16
EPS = 1e-5

D_MODEL = 2048
D_INNER = 4096
SSM_HEAD_DIM = 64
SSM_HEADS = 64
SSM_GROUPS = 8
SSM_HEADS_PER_GROUP = 8
SSM_STATE = 128
GROUP_WIDTH = D_INNER // SSM_GROUPS
CONV_WIDTH = 4
CONV_DIM = D_INNER + 2 * SSM_GROUPS * SSM_STATE
IN_PROJ_DIM = D_INNER + CONV_DIM + SSM_HEADS
SB_HEADS = 16
SB_HEAD_DIM = 128
N_GROUPS = 4
EXPERTS_PER_GROUP = 4
N_EXPERTS = 16
D_EXPERT = 512

LANES = 128
SUBLANES = 8
IN_PROJ_PAD = 10368
DT_COL_BLOCK = (D_INNER + CONV_DIM) // LANES
SSD_CHUNK = 128
ATT_BLOCK = 256
VMEM_LIMIT = 56 * 1024 * 1024


def _dot(a, b):
    return jnp.dot(a, b, preferred_element_type=F32)


def _dot_nt(a, b):
    return lax.dot_general(a, b, (((1,), (1,)), ((), ())), preferred_element_type=F32)


def _split3(v):
    v1 = v.astype(BF16)
    r = v - v1.astype(F32)
    v2 = r.astype(BF16)
    r = r - v2.astype(F32)
    return v1, v2, r.astype(BF16)


def _dot3_left(m, v):
    v1, v2, v3 = _split3(v)
    return _dot(m, v1) + _dot(m, v2) + _dot(m, v3)


def _dot3_right(v, m):
    v1, v2, v3 = _split3(v)
    return _dot(v1, m) + _dot(v2, m) + _dot(v3, m)


def _softplus(x):
    return jnp.maximum(x, 0.0) + jnp.log(1.0 + jnp.exp(-jnp.abs(x)))


def _silu(x):
    half = 0.5 * x
    return half + half * jnp.tanh(half)


def _rms_scale(x):
    return lax.rsqrt(jnp.mean(x * x, axis=-1, keepdims=True) + EPS)


def _params(*sem):
    return pltpu.CompilerParams(dimension_semantics=sem, vmem_limit_bytes=VMEM_LIMIT)


def _norm_matmul_kernel(x_ref, g_ref, w_ref, o_ref, xn_ref):
    @pl.when(pl.program_id(1) == 0)
    def _():
        x = x_ref[...]
        xn_ref[...] = (x * _rms_scale(x) * g_ref[...]).astype(BF16)

    o_ref[...] = _dot(xn_ref[...], w_ref[...]).astype(o_ref.dtype)


def _norm_matmul(x, g, w, *, tm, tn, out_dtype):
    m, k = x.shape
    n = w.shape[1]
    return pl.pallas_call(
        _norm_matmul_kernel,
        out_shape=jax.ShapeDtypeStruct((m, n), out_dtype),
        grid=(m // tm, n // tn),
        in_specs=[
            pl.BlockSpec((tm, k), lambda i, j: (i, 0)),
            pl.BlockSpec((1, k), lambda i, j: (0, 0)),
            pl.BlockSpec((k, tn), lambda i, j: (0, j)),
        ],
        out_specs=pl.BlockSpec((tm, tn), lambda i, j: (i, j)),
        scratch_shapes=[pltpu.VMEM((tm, k), BF16)],
        compiler_params=_params("arbitrary", "arbitrary"),
        name="norm_matmul",
    )(x, g.reshape(1, k), w)


def _matmul_res_kernel(x_ref, w_ref, r_ref, o_ref):
    o_ref[...] = r_ref[...] + _dot(x_ref[...], w_ref[...])


def _matmul_res(x, w, res, *, tm, tn):
    m, k = x.shape
    n = w.shape[1]
    return pl.pallas_call(
        _matmul_res_kernel,
        out_shape=jax.ShapeDtypeStruct((m, n), F32),
        grid=(m // tm, n // tn),
        in_specs=[
            pl.BlockSpec((tm, k), lambda i, j: (i, 0)),
            pl.BlockSpec((k, tn), lambda i, j: (0, j)),
            pl.BlockSpec((tm, tn), lambda i, j: (i, j)),
        ],
        out_specs=pl.BlockSpec((tm, tn), lambda i, j: (i, j)),
        compiler_params=_params("arbitrary", "arbitrary"),
        name="matmul_res",
    )(x, w, res)


def _ssd_kernel(z_ref, x_ref, b_ref, c_ref, dt_ref,
                wx_ref, wb_ref, wc_ref, bx_ref, bb_ref, bc_ref,
                dtb_ref, alog_ref, dskip_ref, nw_ref,
                o_ref,
                h_ref, xbuf, bbuf, cbuf):
    L = SSD_CHUNK
    g = pl.program_id(1)
    c = pl.program_id(2)
    halo = SUBLANES

    @pl.when(c == 0)
    def _():
        h_ref[...] = jnp.zeros_like(h_ref)
        xbuf[0:halo, :] = jnp.zeros((halo, GROUP_WIDTH), F32)
        bbuf[0:halo, :] = jnp.zeros((halo, SSM_STATE), F32)
        cbuf[0:halo, :] = jnp.zeros((halo, SSM_STATE), F32)

    def conv_silu(raw_ref, buf, w_ref, bias_ref):
        buf[halo:halo + L, :] = raw_ref[...]
        acc = bias_ref[...]
        for k in range(CONV_WIDTH):
            off = halo - (CONV_WIDTH - 1) + k
            acc = acc + w_ref[k:k + 1, :] * buf[off:off + L, :]
        buf[0:halo, :] = buf[L:L + halo, :]
        return _silu(acc)

    xs = conv_silu(x_ref, xbuf, wx_ref, bx_ref)
    bs = conv_silu(b_ref, bbuf, wb_ref, bb_ref)
    cs = conv_silu(c_ref, cbuf, wc_ref, bc_ref)

    src = lax.broadcasted_iota(jnp.int32, (LANES, GROUP_WIDTH), 0)
    dst = lax.broadcasted_iota(jnp.int32, (LANES, GROUP_WIDTH), 1)
    head = lax.shift_right_logical(dst, SSM_HEAD_DIM.bit_length() - 1)
    expand = (src == g * SSM_HEADS_PER_GROUP + head).astype(BF16)
    dt = _softplus(_dot3_right(dt_ref[...], expand) + dtb_ref[...])
    a = dt * (-jnp.exp(alog_ref[...]))

    row = lax.broadcasted_iota(jnp.int32, (L, L), 0)
    col = lax.broadcasted_iota(jnp.int32, (L, L), 1)
    causal = row >= col
    tril = causal.astype(BF16)
    triu = (row <= col).astype(BF16)
    a_parts = _split3(a)
    acs = sum(_dot(tril, p) for p in a_parts)

    hrow = lax.broadcasted_iota(jnp.int32, (SUBLANES, GROUP_WIDTH), 0)
    hcol = lax.broadcasted_iota(jnp.int32, (SUBLANES, GROUP_WIDTH), 1)
    pick = (hcol == hrow * SSM_HEAD_DIM).astype(BF16)
    acs_rows = sum(_dot(_dot_nt(pick, p).astype(BF16), triu) for p in a_parts)

    xdt = xs * dt
    xdt_b = xdt.astype(BF16)
    bs_b = bs.astype(BF16)
    cs_b = cs.astype(BF16)
    cb = _dot_nt(cs_b, bs_b)

    parts = []
    for r in range(SSM_HEADS_PER_GROUP):
        lo = r * SSM_HEAD_DIM
        seg = acs[:, lo:lo + 1] - acs_rows[r:r + 1, :]
        decay = jnp.exp(jnp.where(causal, seg, -jnp.inf))
        m = (cb * decay).astype(BF16)
        parts.append(_dot(m, xdt_b[:, lo:lo + SSM_HEAD_DIM]))
    y_diag = jnp.concatenate(parts, axis=1)

    h_prev = h_ref[...]
    y_off = _dot(cs_b, h_prev.astype(BF16)) * jnp.exp(acs)
    a_last = acs[L - 1:L, :]
    xd = (xdt * jnp.exp(a_last - acs)).astype(BF16)
    h_ref[...] = h_prev * jnp.exp(a_last) + _dot(bs.T.astype(BF16), xd)

    y = y_diag + y_off + xs * dskip_ref[...]
    yz = y * _silu(z_ref[...])
    o_ref[...] = (yz * _rms_scale(yz) * nw_ref[...]).astype(o_ref.dtype)


def _ssd(zx, conv_w, conv_b, dtb_e, alog_e, dskip_e, norm_w, *, batch, seq):
    L = SSD_CHUNK
    nc = seq // L
    gw = GROUP_WIDTH // LANES
    x_blk = D_INNER // GROUP_WIDTH
    b_blk = (2 * D_INNER) // SSM_STATE
    c_blk = b_blk + SSM_GROUPS
    cw_b = D_INNER // SSM_STATE
    cw_c = cw_b + SSM_GROUPS
    del gw

    def rows(b, g, c):
        return b * nc + c

    in_specs = [
        pl.BlockSpec((L, GROUP_WIDTH), lambda b, g, c: (rows(b, g, c), g)),
        pl.BlockSpec((L, GROUP_WIDTH), lambda b, g, c: (rows(b, g, c), x_blk + g)),
        pl.BlockSpec((L, SSM_STATE), lambda b, g, c: (rows(b, g, c), b_blk + g)),
        pl.BlockSpec((L, SSM_STATE), lambda b, g, c: (rows(b, g, c), c_blk + g)),
        pl.BlockSpec((L, LANES), lambda b, g, c: (rows(b, g, c), DT_COL_BLOCK)),
        pl.BlockSpec((CONV_WIDTH, GROUP_WIDTH), lambda b, g, c: (0, g)),
        pl.BlockSpec((CONV_WIDTH, SSM_STATE), lambda b, g, c: (0, cw_b + g)),
        pl.BlockSpec((CONV_WIDTH, SSM_STATE), lambda b, g, c: (0, cw_c + g)),
        pl.BlockSpec((1, GROUP_WIDTH), lambda b, g, c: (0, g)),
        pl.BlockSpec((1, SSM_STATE), lambda b, g, c: (0, cw_b + g)),
        pl.BlockSpec((1, SSM_STATE), lambda b, g, c: (0, cw_c + g)),
        pl.BlockSpec((1, GROUP_WIDTH), lambda b, g, c: (0, g)),
        pl.BlockSpec((1, GROUP_WIDTH), lambda b, g, c: (0, g)),
        pl.BlockSpec((1, GROUP_WIDTH), lambda b, g, c: (0, g)),
        pl.BlockSpec((1, GROUP_WIDTH), lambda b, g, c: (0, g)),
    ]
    cb2 = conv_b.reshape(1, CONV_DIM)
    return pl.pallas_call(
        _ssd_kernel,
        out_shape=jax.ShapeDtypeStruct((batch * seq, D_INNER), BF16),
        grid=(batch, SSM_GROUPS, nc),
        in_specs=in_specs,
        out_specs=pl.BlockSpec((L, GROUP_WIDTH), lambda b, g, c: (rows(b, g, c), g)),
        scratch_shapes=[
            pltpu.VMEM((SSM_STATE, GROUP_WIDTH), F32),
            pltpu.VMEM((L + SUBLANES, GROUP_WIDTH), F32),
            pltpu.VMEM((L + SUBLANES, SSM_STATE), F32),
            pltpu.VMEM((L + SUBLANES, SSM_STATE), F32),
        ],
        compiler_params=_params("arbitrary", "arbitrary", "arbitrary"),
        name="ssd",
    )(zx, zx, zx, zx, zx, conv_w, conv_w, conv_w, cb2, cb2, cb2,
      dtb_e, alog_e, dskip_e, norm_w.reshape(1, D_INNER))


ATT_HEADS_PER_STEP = 4


def _attn_kernel(q_ref, k_ref, v_ref, o_ref):
    tq = ATT_BLOCK
    dh = SB_HEAD_DIM
    i = pl.program_id(2)
    scale = dh ** -0.5
    row = lax.broadcasted_iota(jnp.int32, (tq, tq), 0)
    col = lax.broadcasted_iota(jnp.int32, (tq, tq), 1)
    later = (row > col).astype(BF16)
    strict = col < row

    def block(j, state, diagonal):
        start = pl.multiple_of(j * tq, tq)
        new_state = []
        for h in range(ATT_HEADS_PER_STEP):
            carry, acc = state[h]
            q = q_ref[:, h * dh:(h + 1) * dh]
            kj = k_ref[pl.ds(start, tq), h * dh:(h + 1) * dh]
            vj = v_ref[pl.ds(start, tq), h * dh:(h + 1) * dh]
            z = _dot_nt(q, kj) * scale
            sp = _softplus(z)
            log_keep = jnp.where(strict, -sp, 0.0) if diagonal else -sp
            hi = log_keep.astype(BF16)
            lo = (log_keep - hi.astype(F32)).astype(BF16)
            suffix = _dot(hi, later) + _dot(lo, later) + carry
            w = jnp.exp((z - sp) + suffix)
            if diagonal:
                w = jnp.where(strict, w, 0.0)
            acc = acc + _dot(w.astype(BF16), vj)
            carry = carry + jnp.sum(log_keep, axis=1, keepdims=True)
            new_state.append((carry, acc))
        return tuple(new_state)

    zero = (jnp.zeros((tq, 1), F32), jnp.zeros((tq, dh), F32))
    state = block(i, (zero,) * ATT_HEADS_PER_STEP, True)
    state = lax.fori_loop(0, i, lambda t, s: block(i - 1 - t, s, False), state)
    for h in range(ATT_HEADS_PER_STEP):
        o_ref[:, h * dh:(h + 1) * dh] = state[h][1].astype(o_ref.dtype)


def _attention(q, kv, *, batch, seq):
    tq = ATT_BLOCK
    nq = seq // tq
    width = ATT_HEADS_PER_STEP * SB_HEAD_DIM
    n_steps = SB_HEADS // ATT_HEADS_PER_STEP
    return pl.pallas_call(
        _attn_kernel,
        out_shape=jax.ShapeDtypeStruct((batch * seq, D_MODEL), BF16),
        grid=(batch, n_steps, nq),
        in_specs=[
            pl.BlockSpec((tq, width), lambda b, h, i: (b * nq + i, h)),
            pl.BlockSpec((seq, width), lambda b, h, i: (b, h)),
            pl.BlockSpec((seq, width), lambda b, h, i: (b, n_steps + h)),
        ],
        out_specs=pl.BlockSpec((tq, width), lambda b, h, i: (b * nq + i, h)),
        compiler_params=_params("arbitrary", "arbitrary", "arbitrary"),
        name="sb_attention",
    )(q, kv, kv)


ROUTER_FINE0 = N_GROUPS


ROUTE_E1, ROUTE_E2, ROUTE_W1, ROUTE_W2, ROUTE_R1, ROUTE_R2 = range(6)


def _router_kernel(h_ref, g_ref, w_ref, b_ref, hn_ref, route_ref, counts_ref, carry_ref):
    @pl.when(pl.program_id(0) == 0)
    def _():
        carry_ref[...] = jnp.zeros_like(carry_ref)

    x = h_ref[...]
    hn = x * _rms_scale(x) * g_ref[...]
    hn_ref[...] = hn
    x1 = hn.astype(BF16)
    x2 = (hn - x1.astype(F32)).astype(BF16)
    w = w_ref[...]
    w1 = w.astype(BF16)
    w2 = (w - w1.astype(F32)).astype(BF16)
    logits = _dot(x1, w1) + _dot(x2, w1) + _dot(x1, w2) + b_ref[...]

    lane_i = lax.broadcasted_iota(jnp.int32, logits.shape, 1)
    lane = lane_i.astype(F32)
    neg = -jnp.inf

    def first_argmax(v, vmax):
        return jnp.min(jnp.where(v == vmax, lane, float(LANES)), axis=1, keepdims=True)

    coarse = jnp.where(lane_i < N_GROUPS, logits, neg)
    cmax = jnp.max(coarse, axis=1, keepdims=True)
    p_g = 1.0 / jnp.sum(jnp.exp(coarse - cmax), axis=1, keepdims=True)
    g_idx = first_argmax(coarse, cmax)
    fine_lo = ROUTER_FINE0 + g_idx * EXPERTS_PER_GROUP
    in_group = (lane >= fine_lo) & (lane < fine_lo + EXPERTS_PER_GROUP)
    fine = jnp.where(in_group, logits, neg)
    m1 = jnp.max(fine, axis=1, keepdims=True)
    denom = jnp.sum(jnp.exp(fine - m1), axis=1, keepdims=True)
    i1 = first_argmax(fine, m1)
    rest = jnp.where(lane == i1, neg, fine)
    m2 = jnp.max(rest, axis=1, keepdims=True)
    i2 = first_argmax(rest, m2)
    p1 = 1.0 / denom
    p2 = jnp.exp(m2 - m1) / denom
    norm = p1 + p2
    gate1 = p_g * (p1 / norm)
    gate2 = p_g * (p2 / norm)

    tm = x.shape[0]
    chosen = (lane == i1) | (lane == i2)
    r = lax.broadcasted_iota(jnp.int32, (tm, tm), 0)
    c = lax.broadcasted_iota(jnp.int32, (tm, tm), 1)
    before = _dot((c < r).astype(BF16), chosen.astype(BF16)) + carry_ref[...]
    rank1 = jnp.sum(jnp.where(lane == i1, before, 0.0), axis=1, keepdims=True)
    rank2 = jnp.sum(jnp.where(lane == i2, before, 0.0), axis=1, keepdims=True)
    carry_ref[...] += jnp.sum(chosen.astype(F32), axis=0, keepdims=True)
    counts_ref[...] = carry_ref[...]

    route = jnp.zeros_like(logits)
    for slot, val in ((ROUTE_E1, i1 - ROUTER_FINE0), (ROUTE_E2, i2 - ROUTER_FINE0),
                      (ROUTE_W1, gate1), (ROUTE_W2, gate2), (ROUTE_R1, rank1), (ROUTE_R2, rank2)):
        route = jnp.where(lane_i == slot, val, route)
    route_ref[...] = route


def _router(h, g, w_router, b_router, *, tm):
    m, k = h.shape
    return pl.pallas_call(
        _router_kernel,
        out_shape=(jax.ShapeDtypeStruct((m, k), F32), jax.ShapeDtypeStruct((m, LANES), F32),
                   jax.ShapeDtypeStruct((1, LANES), F32)),
        grid=(m // tm,),
        in_specs=[
            pl.BlockSpec((tm, k), lambda i: (i, 0)),
            pl.BlockSpec((1, k), lambda i: (0, 0)),
            pl.BlockSpec((k, LANES), lambda i: (0, 0)),
            pl.BlockSpec((1, LANES), lambda i: (0, 0)),
        ],
        out_specs=(pl.BlockSpec((tm, k), lambda i: (i, 0)),
                   pl.BlockSpec((tm, LANES), lambda i: (i, 0)),
                   pl.BlockSpec((1, LANES), lambda i: (0, 0))),
        scratch_shapes=[pltpu.VMEM((1, LANES), F32)],
        compiler_params=_params("arbitrary"),
        name="moe_router",
    )(h, g.reshape(1, k), w_router, b_router)


DISPATCH_TOKENS = 512


def _row_copy(src, src_row, dst, dst_row, sem):
    return pltpu.make_async_copy(src.at[pl.ds(src_row, 1)], dst.at[pl.ds(dst_row, 1)], sem)


def _dispatch_kernel(pos1_ref, pos2_ref, hn_ref, xs_ref, sem):
    base = pl.program_id(0) * DISPATCH_TOKENS

    def issue(t, _):
        tok = base + t
        _row_copy(hn_ref, tok, xs_ref, pos1_ref[tok], sem).start()
        _row_copy(hn_ref, tok, xs_ref, pos2_ref[tok], sem).start()
        return 0

    lax.fori_loop(0, DISPATCH_TOKENS, issue, 0)

    n_rows = 2 * DISPATCH_TOKENS
    pltpu.make_async_copy(hn_ref.at[pl.ds(0, n_rows)], xs_ref.at[pl.ds(0, n_rows)], sem).wait()


def _dispatch(hn, pos1, pos2):
    m, k = hn.shape
    return pl.pallas_call(
        _dispatch_kernel,
        out_shape=jax.ShapeDtypeStruct((2 * m, k), F32),
        grid_spec=pltpu.PrefetchScalarGridSpec(
            num_scalar_prefetch=2,
            grid=(m // DISPATCH_TOKENS,),
            in_specs=[pl.BlockSpec(memory_space=pl.ANY)],
            out_specs=pl.BlockSpec(memory_space=pl.ANY),
            scratch_shapes=[pltpu.SemaphoreType.DMA],
        ),
        compiler_params=_params("arbitrary"),
        name="moe_dispatch",
    )(pos1, pos2, hn)


EXPERT_TILE = 256


def _experts_kernel(tile_ref, expert_ref, lo_ref, hi_ref, first_ref, valid_ref,
                    x_ref, wg_ref, wu_ref, wd_ref, o_ref):
    del tile_ref, expert_ref
    n = pl.program_id(0)

    @pl.when(valid_ref[n] == 1)
    def _():
        x = x_ref[...].astype(BF16)
        rows = lax.broadcasted_iota(jnp.int32, (EXPERT_TILE, 1), 0)
        inside = (rows >= lo_ref[n]) & (rows < hi_ref[n])
        hid = _silu(_dot(x, wg_ref[0, 0].astype(BF16))) * _dot(x, wu_ref[0, 0].astype(BF16))
        hid = jnp.where(inside, hid, 0.0).astype(BF16)
        y = _dot(hid, wd_ref[0, 0].astype(BF16))

        @pl.when(first_ref[n] == 1)
        def _():
            o_ref[...] = y

        @pl.when(first_ref[n] == 0)
        def _():
            o_ref[...] += y


def _experts(work, xs, w_gate, w_up, w_down, layer):
    rows, k = xs.shape
    f = w_gate.shape[-1]
    n_items = work[0].shape[0]
    return pl.pallas_call(
        _experts_kernel,
        out_shape=jax.ShapeDtypeStruct((rows, k), F32),
        grid_spec=pltpu.PrefetchScalarGridSpec(
            num_scalar_prefetch=6,
            grid=(n_items,),
            in_specs=[
                pl.BlockSpec((EXPERT_TILE, k), lambda n, tile, *_: (tile[n], 0)),
                pl.BlockSpec((1, 1, k, f), lambda n, tile, expert, *_: (layer, expert[n], 0, 0)),
                pl.BlockSpec((1, 1, k, f), lambda n, tile, expert, *_: (layer, expert[n], 0, 0)),
                pl.BlockSpec((1, 1, f, k), lambda n, tile, expert, *_: (layer, expert[n], 0, 0)),
            ],
            out_specs=pl.BlockSpec((EXPERT_TILE, k), lambda n, tile, *_: (tile[n], 0)),
        ),
        compiler_params=_params("arbitrary"),
        name="moe_experts",
    )(*work, xs, w_gate, w_up, w_down)


COMBINE_TOKENS = 256


def _combine_kernel(pos1_ref, pos2_ref, h_ref, route_ref, ys_ref, o_ref, buf1, buf2, sem):
    base = pl.program_id(0) * COMBINE_TOKENS

    def issue(t, _):
        tok = base + t
        _row_copy(ys_ref, pos1_ref[tok], buf1, t, sem).start()
        _row_copy(ys_ref, pos2_ref[tok], buf2, t, sem).start()
        return 0

    lax.fori_loop(0, COMBINE_TOKENS, issue, 0)

    pltpu.make_async_copy(ys_ref.at[pl.ds(0, COMBINE_TOKENS)], buf1, sem).wait()
    pltpu.make_async_copy(ys_ref.at[pl.ds(0, COMBINE_TOKENS)], buf2, sem).wait()
    route = route_ref[...]
    w1 = route[:, ROUTE_W1:ROUTE_W1 + 1]
    w2 = route[:, ROUTE_W2:ROUTE_W2 + 1]
    o_ref[...] = h_ref[...] + w1 * buf1[...] + w2 * buf2[...]


def _combine(h, route, ys, pos1, pos2):
    m, k = h.shape
    tm = COMBINE_TOKENS
    return pl.pallas_call(
        _combine_kernel,
        out_shape=jax.ShapeDtypeStruct((m, k), F32),
        grid_spec=pltpu.PrefetchScalarGridSpec(
            num_scalar_prefetch=2,
            grid=(m // tm,),
            in_specs=[
                pl.BlockSpec((tm, k), lambda i, *_: (i, 0)),
                pl.BlockSpec((tm, LANES), lambda i, *_: (i, 0)),
                pl.BlockSpec(memory_space=pl.ANY),
            ],
            out_specs=pl.BlockSpec((tm, k), lambda i, *_: (i, 0)),
            scratch_shapes=[pltpu.VMEM((tm, k), F32), pltpu.VMEM((tm, k), F32),
                            pltpu.SemaphoreType.DMA],
        ),
        compiler_params=_params("arbitrary"),
        name="moe_combine",
    )(pos1, pos2, h, route, ys)


def _work_list(counts, n_rows):
    tm = EXPERT_TILE
    n_tiles = n_rows // tm
    n_items = n_tiles + N_EXPERTS - 1
    ends = jnp.cumsum(counts)
    starts = ends - counts
    tile_lo = jnp.arange(n_tiles, dtype=jnp.int32) * tm
    e_first = jnp.searchsorted(ends, tile_lo, side="right").astype(jnp.int32)
    e_last = jnp.searchsorted(ends, tile_lo + (tm - 1), side="right").astype(jnp.int32)
    per_tile = e_last - e_first + 1
    item0 = jnp.cumsum(per_tile) - per_tile
    total = item0[-1] + per_tile[-1]
    n = jnp.arange(n_items, dtype=jnp.int32)
    valid = n < total
    tile = jnp.clip(jnp.searchsorted(item0, n, side="right").astype(jnp.int32) - 1, 0, n_tiles - 1)
    tile = jnp.where(valid, tile, n_tiles - 1)
    expert = jnp.where(valid, e_first[tile] + (n - item0[tile]), e_last[-1])
    expert = jnp.clip(expert, 0, N_EXPERTS - 1)
    lo = jnp.clip(starts[expert] - tile * tm, 0, tm)
    hi = jnp.clip(ends[expert] - tile * tm, 0, tm)
    first = (n == item0[tile]) & valid
    as_i32 = lambda a: a.astype(jnp.int32)
    return tuple(map(as_i32, (tile, expert, lo, hi, first, valid))), starts


def _rmsnorm_kernel(x_ref, g_ref, o_ref):
    x = x_ref[...]
    o_ref[...] = x * _rms_scale(x) * g_ref[...]


def _rmsnorm(x, g, *, tm):
    m, k = x.shape
    return pl.pallas_call(
        _rmsnorm_kernel,
        out_shape=jax.ShapeDtypeStruct((m, k), F32),
        grid=(m // tm,),
        in_specs=[pl.BlockSpec((tm, k), lambda i: (i, 0)), pl.BlockSpec((1, k), lambda i: (0, 0))],
        out_specs=pl.BlockSpec((tm, k), lambda i: (i, 0)),
        compiler_params=_params("arbitrary"),
        name="final_rmsnorm",
    )(x, g.reshape(1, k))


def _router_weights(w_coarse, b_coarse, w_fine, b_fine):
    pad = LANES - N_GROUPS - N_EXPERTS
    w = jnp.concatenate([w_coarse, w_fine.reshape(D_MODEL, N_EXPERTS),
                         jnp.zeros((D_MODEL, pad), F32)], axis=1)
    b = jnp.concatenate([b_coarse, b_fine.reshape(N_EXPERTS), jnp.zeros((pad,), F32)])
    return w, b.reshape(1, LANES)


def _moe(h, norm_g, w_coarse, b_coarse, w_fine, b_fine, w_gate, w_up, w_down, layer):
    w_r, b_r = _router_weights(w_coarse, b_coarse, w_fine, b_fine)
    hn, route, counts = _router(h, norm_g, w_r, b_r, tm=256)
    counts = counts[0, ROUTER_FINE0:ROUTER_FINE0 + N_EXPERTS].astype(jnp.int32)
    work, starts = _work_list(counts, 2 * h.shape[0])
    col = lambda c: route[:, c].astype(jnp.int32)
    pos1 = starts[col(ROUTE_E1)] + col(ROUTE_R1)
    pos2 = starts[col(ROUTE_E2)] + col(ROUTE_R2)
    xs = _dispatch(hn, pos1, pos2)
    ys = _experts(work, xs, w_gate, w_up, w_down, layer)
    return _combine(h, route, ys, pos1, pos2)


def kernel(x, mix_norm, ffn_norm, ssm_w_in, ssm_conv_w, ssm_conv_b, ssm_dt_bias, ssm_a_log, ssm_d,
           ssm_norm_w, ssm_w_out, kv_norm, w_k, w_v, sb_w_q, sb_w_out, moe_w_coarse, moe_b_coarse,
           moe_w_fine, moe_b_fine, moe_w_gate, moe_w_up, moe_w_down, final_norm):
    batch, seq, d = x.shape
    h = x.reshape(batch * seq, d)

    w_in = jnp.pad(ssm_w_in[0], ((0, 0), (0, IN_PROJ_PAD - IN_PROJ_DIM))).astype(BF16)
    zx = _norm_matmul(h, mix_norm[0], w_in, tm=1024, tn=1152, out_dtype=F32)
    expand = lambda p: jnp.repeat(p, SSM_HEAD_DIM).reshape(1, D_INNER)
    yn = _ssd(zx, ssm_conv_w[0], ssm_conv_b[0], expand(ssm_dt_bias[0]), expand(ssm_a_log[0]),
              expand(ssm_d[0]), ssm_norm_w[0], batch=batch, seq=seq)
    h = _matmul_res(yn, ssm_w_out[0].astype(BF16), h, tm=1024, tn=512)
    h = _moe(h, ffn_norm[0], moe_w_coarse[0], moe_b_coarse[0], moe_w_fine[0], moe_b_fine[0],
             moe_w_gate, moe_w_up, moe_w_down, 0)

    w_kv = jnp.concatenate([w_k, w_v], axis=1).astype(BF16)
    kv = _norm_matmul(h, kv_norm, w_kv, tm=1024, tn=1024, out_dtype=BF16)

    q = _norm_matmul(h, mix_norm[1], sb_w_q[0].astype(BF16), tm=1024, tn=1024, out_dtype=BF16)
    o = _attention(q, kv, batch=batch, seq=seq)
    h = _matmul_res(o, sb_w_out[0].astype(BF16), h, tm=1024, tn=512)
    h = _moe(h, ffn_norm[1], moe_w_coarse[1], moe_b_coarse[1], moe_w_fine[1], moe_b_fine[1],
             moe_w_gate, moe_w_up, moe_w_down, 1)

    return _rmsnorm(h, final_norm, tm=512).reshape(batch, seq, d)
```

```python
import functools

import jax
import jax.numpy as jnp
from jax import lax
from jax.experimental import pallas as pl
from jax.experimental.pallas import tpu as pltpu

F32 = jnp.float32
BF16 = jnp.bfloat16
EPS = 1e-5

D_MODEL = 2048
D_INNER = 4096
SSM_HEAD_DIM = 64
SSM_HEADS = 64
SSM_GROUPS = 8
SSM_HEADS_PER_GROUP = 8
SSM_STATE = 128
GROUP_WIDTH = D_INNER // SSM_GROUPS
CONV_WIDTH = 4
CONV_DIM = D_INNER + 2 * SSM_GROUPS * SSM_STATE
IN_PROJ_DIM = D_INNER + CONV_DIM + SSM_HEADS
SB_HEADS = 16
SB_HEAD_DIM = 128
N_GROUPS = 4
EXPERTS_PER_GROUP = 4
N_EXPERTS = 16
D_EXPERT = 512

LANES = 128
SUBLANES = 8
IN_PROJ_PAD = 10368
DT_COL_BLOCK = (D_INNER + CONV_DIM) // LANES
SSD_CHUNK = 128
SSD_GROUPS_PER_STEP = 2
ATT_BLOCK = 256
VMEM_LIMIT = 56 * 1024 * 1024


def _dot(a, b):
    return jnp.dot(a, b, preferred_element_type=F32)


def _dot_nt(a, b):
    return lax.dot_general(a, b, (((1,), (1,)), ((), ())), preferred_element_type=F32)


def _split3(v):
    v1 = v.astype(BF16)
    r = v - v1.astype(F32)
    v2 = r.astype(BF16)
    r = r - v2.astype(F32)
    return v1, v2, r.astype(BF16)


def _split2(v):
    hi = v.astype(BF16)
    return hi, (v - hi.astype(F32)).astype(BF16)


def _dot3_left(m, v):
    v1, v2, v3 = _split3(v)
    return _dot(m, v1) + _dot(m, v2) + _dot(m, v3)


def _dot3_right(v, m):
    v1, v2, v3 = _split3(v)
    return _dot(v1, m) + _dot(v2, m) + _dot(v3, m)


def _softplus(x):
    return jnp.maximum(x, 0.0) + jnp.log(1.0 + jnp.exp(-jnp.abs(x)))


def _silu(x):
    half = 0.5 * x
    return half + half * jnp.tanh(half)


def _rms_scale(x):
    return lax.rsqrt(jnp.mean(x * x, axis=-1, keepdims=True) + EPS)


def _params(*sem):
    return pltpu.CompilerParams(dimension_semantics=sem, vmem_limit_bytes=VMEM_LIMIT)


def _norm_matmul_kernel(x_ref, g_ref, w_ref, o_ref, xn_ref):
    @pl.when(pl.program_id(1) == 0)
    def _():
        x = x_ref[...]
        xn_ref[...] = (x * _rms_scale(x) * g_ref[...]).astype(BF16)

    o_ref[...] = _dot(xn_ref[...], w_ref[...]).astype(o_ref.dtype)


def _norm_matmul(x, g, w, *, tm, tn, out_dtype):
    m, k = x.shape
    n = w.shape[1]
    return pl.pallas_call(
        _norm_matmul_kernel,
        out_shape=jax.ShapeDtypeStruct((m, n), out_dtype),
        grid=(m // tm, n // tn),
        in_specs=[
            pl.BlockSpec((tm, k), lambda i, j: (i, 0)),
            pl.BlockSpec((1, k), lambda i, j: (0, 0)),
            pl.BlockSpec((k, tn), lambda i, j: (0, j)),
        ],
        out_specs=pl.BlockSpec((tm, tn), lambda i, j: (i, j)),
        scratch_shapes=[pltpu.VMEM((tm, k), BF16)],
        compiler_params=_params("arbitrary", "arbitrary"),
        name="norm_matmul",
    )(x, g.reshape(1, k), w)


def _matmul_res_kernel(x_ref, w_ref, r_ref, o_ref):
    o_ref[...] = r_ref[...] + _dot(x_ref[...], w_ref[...])


def _matmul_res(x, w, res, *, tm, tn):
    m, k = x.shape
    n = w.shape[1]
    return pl.pallas_call(
        _matmul_res_kernel,
        out_shape=jax.ShapeDtypeStruct((m, n), F32),
        grid=(m // tm, n // tn),
        in_specs=[
            pl.BlockSpec((tm, k), lambda i, j: (i, 0)),
            pl.BlockSpec((k, tn), lambda i, j: (0, j)),
            pl.BlockSpec((tm, tn), lambda i, j: (i, j)),
        ],
        out_specs=pl.BlockSpec((tm, tn), lambda i, j: (i, j)),
        compiler_params=_params("arbitrary", "arbitrary"),
        name="matmul_res",
    )(x, w, res)


def _ssd_kernel(z_ref, x_ref, b_ref, c_ref, dt_ref,
                wx_ref, wb_ref, wc_ref, bx_ref, bb_ref, bc_ref,
                dtb_ref, alog_ref, dskip_ref, nw_ref, expand_ref,
                o_ref,
                h_ref, xbuf, bbuf, cbuf):
    L = SSD_CHUNK
    gw = GROUP_WIDTH
    ns = SSM_STATE
    c = pl.program_id(2)
    halo = SUBLANES
    groups = range(SSD_GROUPS_PER_STEP)

    @pl.when(c == 0)
    def _():
        h_ref[...] = jnp.zeros_like(h_ref)
        xbuf[0:halo, :] = jnp.zeros((halo, xbuf.shape[1]), F32)
        bbuf[0:halo, :] = jnp.zeros((halo, bbuf.shape[1]), F32)
        cbuf[0:halo, :] = jnp.zeros((halo, cbuf.shape[1]), F32)

    def conv_silu(raw_ref, buf, w_ref, bias_ref):
        buf[halo:halo + L, :] = raw_ref[...]
        acc = bias_ref[...]
        for k in range(CONV_WIDTH):
            off = halo - (CONV_WIDTH - 1) + k
            acc = acc + w_ref[k:k + 1, :] * buf[off:off + L, :]
        buf[0:halo, :] = buf[L:L + halo, :]
        return _silu(acc)

    bs = conv_silu(b_ref, bbuf, wb_ref, bb_ref)
    cs = conv_silu(c_ref, cbuf, wc_ref, bc_ref)
    bs_b = [bs[:, g * ns:(g + 1) * ns].astype(BF16) for g in groups]
    cs_b = [cs[:, g * ns:(g + 1) * ns].astype(BF16) for g in groups]
    cb = [_dot_nt(cs_b[g], bs_b[g]) for g in groups]
    h_prev = [h_ref[:, g * gw:(g + 1) * gw] for g in groups]
    y_off = [_dot(cs_b[g], h_prev[g].astype(BF16)) for g in groups]

    dt_parts = _split3(dt_ref[...])
    dt_raw = [sum(_dot(p, expand_ref[g]) for p in dt_parts) for g in groups]

    xs = conv_silu(x_ref, xbuf, wx_ref, bx_ref)

    row = lax.broadcasted_iota(jnp.int32, (L, L), 0)
    col = lax.broadcasted_iota(jnp.int32, (L, L), 1)
    causal = row >= col
    tril = causal.astype(BF16)
    triu = (row <= col).astype(BF16)
    hrow = lax.broadcasted_iota(jnp.int32, (SUBLANES, gw), 0)
    hcol = lax.broadcasted_iota(jnp.int32, (SUBLANES, gw), 1)
    pick = (hcol == hrow * SSM_HEAD_DIM).astype(BF16)

    dt, a_parts = [], []
    for g in groups:
        lanes = slice(g * gw, (g + 1) * gw)
        dt.append(_softplus(dt_raw[g] + dtb_ref[:, lanes]))
        a_parts.append(_split2(dt[g] * (-jnp.exp(alog_ref[:, lanes]))))
    acs = [sum(_dot(tril, p) for p in a_parts[g]) for g in groups]
    a_rows = [[_dot_nt(pick, p).astype(BF16) for p in a_parts[g]] for g in groups]
    acs_rows = [sum(_dot(p, triu) for p in a_rows[g]) for g in groups]

    xdt = [xs[:, g * gw:(g + 1) * gw] * dt[g] for g in groups]
    xdt_b = [v.astype(BF16) for v in xdt]
    y_diag = []
    for g in groups:
        parts = []
        for r in range(SSM_HEADS_PER_GROUP):
            lo = r * SSM_HEAD_DIM
            seg = acs[g][:, lo:lo + 1] - acs_rows[g][r:r + 1, :]
            decay = jnp.exp(jnp.where(causal, seg, -jnp.inf))
            m = (cb[g] * decay).astype(BF16)
            parts.append(_dot(m, xdt_b[g][:, lo:lo + SSM_HEAD_DIM]))
        y_diag.append(jnp.concatenate(parts, axis=1))

    a_last = [acs[g][L - 1:L, :] for g in groups]
    xd = [(xdt[g] * jnp.exp(a_last[g] - acs[g])).astype(BF16) for g in groups]
    states = [_dot(bs[:, g * ns:(g + 1) * ns].T.astype(BF16), xd[g]) for g in groups]
    for g in groups:
        lanes = slice(g * gw, (g + 1) * gw)
        h_ref[:, lanes] = h_prev[g] * jnp.exp(a_last[g]) + states[g]
        y = y_diag[g] + y_off[g] * jnp.exp(acs[g]) + xs[:, lanes] * dskip_ref[:, lanes]
        yz = y * _silu(z_ref[:, lanes])
        o_ref[:, lanes] = (yz * _rms_scale(yz) * nw_ref[:, lanes]).astype(o_ref.dtype)


def _ssd(zx, conv_w, conv_b, dtb_e, alog_e, dskip_e, norm_w, *, batch, seq):
    L = SSD_CHUNK
    nc = seq // L
    xw = SSD_GROUPS_PER_STEP * GROUP_WIDTH
    sw = SSD_GROUPS_PER_STEP * SSM_STATE
    x_blk = D_INNER // xw
    b_blk = (2 * D_INNER) // sw
    c_blk = b_blk + SSM_GROUPS * SSM_STATE // sw
    cw_b = D_INNER // sw
    cw_c = cw_b + SSM_GROUPS * SSM_STATE // sw

    def rows(b, g, c):
        return b * nc + c

    in_specs = [
        pl.BlockSpec((L, xw), lambda b, g, c: (rows(b, g, c), g)),
        pl.BlockSpec((L, xw), lambda b, g, c: (rows(b, g, c), x_blk + g)),
        pl.BlockSpec((L, sw), lambda b, g, c: (rows(b, g, c), b_blk + g)),
        pl.BlockSpec((L, sw), lambda b, g, c: (rows(b, g, c), c_blk + g)),
        pl.BlockSpec((L, LANES), lambda b, g, c: (rows(b, g, c), DT_COL_BLOCK)),
        pl.BlockSpec((CONV_WIDTH, xw), lambda b, g, c: (0, g)),
        pl.BlockSpec((CONV_WIDTH, sw), lambda b, g, c: (0, cw_b + g)),
        pl.BlockSpec((CONV_WIDTH, sw), lambda b, g, c: (0, cw_c + g)),
        pl.BlockSpec((1, xw), lambda b, g, c: (0, g)),
        pl.BlockSpec((1, sw), lambda b, g, c: (0, cw_b + g)),
        pl.BlockSpec((1, sw), lambda b, g, c: (0, cw_c + g)),
        pl.BlockSpec((1, xw), lambda b, g, c: (0, g)),
        pl.BlockSpec((1, xw), lambda b, g, c: (0, g)),
        pl.BlockSpec((1, xw), lambda b, g, c: (0, g)),
        pl.BlockSpec((1, xw), lambda b, g, c: (0, g)),
        pl.BlockSpec((SSD_GROUPS_PER_STEP, LANES, GROUP_WIDTH), lambda b, g, c: (g, 0, 0)),
    ]
    head_of_lane = (jnp.arange(SSM_GROUPS)[:, None, None] * SSM_HEADS_PER_GROUP
                    + jnp.arange(GROUP_WIDTH)[None, None, :] // SSM_HEAD_DIM)
    expand = (jnp.arange(LANES)[None, :, None] == head_of_lane).astype(BF16)
    cb2 = conv_b.reshape(1, CONV_DIM)
    return pl.pallas_call(
        _ssd_kernel,
        out_shape=jax.ShapeDtypeStruct((batch * seq, D_INNER), BF16),
        grid=(batch, SSM_GROUPS // SSD_GROUPS_PER_STEP, nc),
        in_specs=in_specs,
        out_specs=pl.BlockSpec((L, xw), lambda b, g, c: (rows(b, g, c), g)),
        scratch_shapes=[
            pltpu.VMEM((SSM_STATE, xw), F32),
            pltpu.VMEM((L + SUBLANES, xw), F32),
            pltpu.VMEM((L + SUBLANES, sw), F32),
            pltpu.VMEM((L + SUBLANES, sw), F32),
        ],
        compiler_params=_params("arbitrary", "arbitrary", "arbitrary"),
        name="ssd",
    )(zx, zx, zx, zx, zx, conv_w, conv_w, conv_w, cb2, cb2, cb2,
      dtb_e, alog_e, dskip_e, norm_w.reshape(1, D_INNER), expand)


ATT_HEADS_PER_STEP = 4


def _attn_kernel(q_ref, k_ref, v_ref, o_ref):
    tq = ATT_BLOCK
    dh = SB_HEAD_DIM
    i = pl.program_id(2)
    scale = dh ** -0.5
    row = lax.broadcasted_iota(jnp.int32, (tq, tq), 0)
    col = lax.broadcasted_iota(jnp.int32, (tq, tq), 1)
    strict = col < row
    row2 = lax.broadcasted_iota(jnp.int32, (2 * tq, tq), 0) & (tq - 1)
    col2 = lax.broadcasted_iota(jnp.int32, (2 * tq, tq), 1)
    neg_later = jnp.where(row2 > col2, -1.0, 0.0).astype(BF16)

    heads = range(ATT_HEADS_PER_STEP)

    def block(j, state, diagonal):
        start = pl.multiple_of(j * tq, tq)
        cols = [slice(h * dh, (h + 1) * dh) for h in heads]
        z = [_dot_nt(q_ref[:, cols[h]], k_ref[pl.ds(start, tq), cols[h]]) * scale for h in heads]
        drop, logit, parts = [], [], []
        for h in heads:
            neg_abs = lax.bitcast_convert_type(
                lax.bitcast_convert_type(z[h], jnp.uint32) | jnp.uint32(0x80000000), F32)
            sp = jnp.maximum(z[h], 0.0) + jnp.log(1.0 + jnp.exp(neg_abs))
            logit.append(z[h] - sp)
            if diagonal:
                sp = jnp.where(strict, sp, 0.0)
            hi, lo = _split2(sp)
            parts.append(jnp.concatenate([hi, lo], axis=1))
            drop.append(sp)
        suffix = [_dot(parts[h], neg_later) for h in heads]
        weights = []
        for h in heads:
            w = jnp.exp(logit[h] + suffix[h] + state[h][0])
            if diagonal:
                w = jnp.where(strict, w, 0.0)
            weights.append(w.astype(BF16))
        out = [_dot(weights[h], v_ref[pl.ds(start, tq), cols[h]]) for h in heads]
        return tuple((state[h][0] - jnp.sum(drop[h], axis=1, keepdims=True), state[h][1] + out[h])
                     for h in heads)

    zero = (jnp.zeros((tq, 1), F32), jnp.zeros((tq, dh), F32))
    state = block(i, (zero,) * ATT_HEADS_PER_STEP, True)
    state = lax.fori_loop(0, i, lambda t, s: block(i - 1 - t, s, False), state)
    for h in range(ATT_HEADS_PER_STEP):
        o_ref[:, h * dh:(h + 1) * dh] = state[h][1].astype(o_ref.dtype)


def _attention(q, kv, *, batch, seq):
    tq = ATT_BLOCK
    nq = seq // tq
    width = ATT_HEADS_PER_STEP * SB_HEAD_DIM
    n_steps = SB_HEADS // ATT_HEADS_PER_STEP
    return pl.pallas_call(
        _attn_kernel,
        out_shape=jax.ShapeDtypeStruct((batch * seq, D_MODEL), BF16),
        grid=(batch, n_steps, nq),
        in_specs=[
            pl.BlockSpec((tq, width), lambda b, h, i: (b * nq + i, h)),
            pl.BlockSpec((seq, width), lambda b, h, i: (b, h)),
            pl.BlockSpec((seq, width), lambda b, h, i: (b, n_steps + h)),
        ],
        out_specs=pl.BlockSpec((tq, width), lambda b, h, i: (b * nq + i, h)),
        compiler_params=_params("arbitrary", "arbitrary", "arbitrary"),
        name="sb_attention",
    )(q, kv, kv)


ROUTER_FINE0 = N_GROUPS


ROUTE_E1, ROUTE_E2, ROUTE_W1, ROUTE_W2, ROUTE_R1, ROUTE_R2 = range(6)


def _router_kernel(h_ref, g_ref, w_ref, b_ref, hn_ref, route_ref, counts_ref, carry_ref):
    @pl.when(pl.program_id(0) == 0)
    def _():
        carry_ref[...] = jnp.zeros_like(carry_ref)

    x = h_ref[...]
    hn = x * _rms_scale(x) * g_ref[...]
    hn_ref[...] = hn
    x1 = hn.astype(BF16)
    x2 = (hn - x1.astype(F32)).astype(BF16)
    w = w_ref[...]
    w1 = w.astype(BF16)
    w2 = (w - w1.astype(F32)).astype(BF16)
    logits = _dot(x1, w1) + _dot(x2, w1) + _dot(x1, w2) + b_ref[...]

    lane_i = lax.broadcasted_iota(jnp.int32, logits.shape, 1)
    lane = lane_i.astype(F32)
    neg = -jnp.inf

    def first_argmax(v, vmax):
        return jnp.min(jnp.where(v == vmax, lane, float(LANES)), axis=1, keepdims=True)

    coarse = jnp.where(lane_i < N_GROUPS, logits, neg)
    cmax = jnp.max(coarse, axis=1, keepdims=True)
    p_g = 1.0 / jnp.sum(jnp.exp(coarse - cmax), axis=1, keepdims=True)
    g_idx = first_argmax(coarse, cmax)
    fine_lo = ROUTER_FINE0 + g_idx * EXPERTS_PER_GROUP
    in_group = (lane >= fine_lo) & (lane < fine_lo + EXPERTS_PER_GROUP)
    fine = jnp.where(in_group, logits, neg)
    m1 = jnp.max(fine, axis=1, keepdims=True)
    denom = jnp.sum(jnp.exp(fine - m1), axis=1, keepdims=True)
    i1 = first_argmax(fine, m1)
    rest = jnp.where(lane == i1, neg, fine)
    m2 = jnp.max(rest, axis=1, keepdims=True)
    i2 = first_argmax(rest, m2)
    p1 = 1.0 / denom
    p2 = jnp.exp(m2 - m1) / denom
    norm = p1 + p2
    gate1 = p_g * (p1 / norm)
    gate2 = p_g * (p2 / norm)

    tm = x.shape[0]
    chosen = (lane == i1) | (lane == i2)
    r = lax.broadcasted_iota(jnp.int32, (tm, tm), 0)
    c = lax.broadcasted_iota(jnp.int32, (tm, tm), 1)
    before = _dot((c < r).astype(BF16), chosen.astype(BF16)) + carry_ref[...]
    rank1 = jnp.sum(jnp.where(lane == i1, before, 0.0), axis=1, keepdims=True)
    rank2 = jnp.sum(jnp.where(lane == i2, before, 0.0), axis=1, keepdims=True)
    carry_ref[...] += jnp.sum(chosen.astype(F32), axis=0, keepdims=True)
    counts_ref[...] = carry_ref[...]

    route = jnp.zeros_like(logits)
    for slot, val in ((ROUTE_E1, i1 - ROUTER_FINE0), (ROUTE_E2, i2 - ROUTER_FINE0),
                      (ROUTE_W1, gate1), (ROUTE_W2, gate2), (ROUTE_R1, rank1), (ROUTE_R2, rank2)):
        route = jnp.where(lane_i == slot, val, route)
    route_ref[...] = route


def _router(h, g, w_router, b_router, *, tm):
    m, k = h.shape
    return pl.pallas_call(
        _router_kernel,
        out_shape=(jax.ShapeDtypeStruct((m, k), F32), jax.ShapeDtypeStruct((m, LANES), F32),
                   jax.ShapeDtypeStruct((1, LANES), F32)),
        grid=(m // tm,),
        in_specs=[
            pl.BlockSpec((tm, k), lambda i: (i, 0)),
            pl.BlockSpec((1, k), lambda i: (0, 0)),
            pl.BlockSpec((k, LANES), lambda i: (0, 0)),
            pl.BlockSpec((1, LANES), lambda i: (0, 0)),
        ],
        out_specs=(pl.BlockSpec((tm, k), lambda i: (i, 0)),
                   pl.BlockSpec((tm, LANES), lambda i: (i, 0)),
                   pl.BlockSpec((1, LANES), lambda i: (0, 0))),
        scratch_shapes=[pltpu.VMEM((1, LANES), F32)],
        compiler_params=_params("arbitrary"),
        name="moe_router",
    )(h, g.reshape(1, k), w_router, b_router)


DISPATCH_TOKENS = 512


def _row_copy(src, src_row, dst, dst_row, sem):
    return pltpu.make_async_copy(src.at[pl.ds(src_row, 1)], dst.at[pl.ds(dst_row, 1)], sem)


def _dispatch_kernel(pos1_ref, pos2_ref, hn_ref, xs_ref, sem):
    base = pl.program_id(0) * DISPATCH_TOKENS

    def issue(t, _):
        tok = base + t
        _row_copy(hn_ref, t, xs_ref, pos1_ref[tok], sem).start()
        _row_copy(hn_ref, t, xs_ref, pos2_ref[tok], sem).start()
        return 0

    lax.fori_loop(0, DISPATCH_TOKENS, issue, 0)

    for _ in range(2):
        pltpu.make_async_copy(hn_ref, xs_ref.at[pl.ds(0, DISPATCH_TOKENS)], sem).wait()


def _dispatch(hn, pos1, pos2):
    m, k = hn.shape
    return pl.pallas_call(
        _dispatch_kernel,
        out_shape=jax.ShapeDtypeStruct((2 * m, k), F32),
        grid_spec=pltpu.PrefetchScalarGridSpec(
            num_scalar_prefetch=2,
            grid=(m // DISPATCH_TOKENS,),
            in_specs=[pl.BlockSpec((DISPATCH_TOKENS, k), lambda i, *_: (i, 0))],
            out_specs=pl.BlockSpec(memory_space=pl.ANY),
            scratch_shapes=[pltpu.SemaphoreType.DMA],
        ),
        compiler_params=_params("arbitrary"),
        name="moe_dispatch",
    )(pos1, pos2, hn)


EXPERT_TILE = 256


def _experts_kernel(tile_ref, expert_ref, lo_ref, hi_ref, first_ref, valid_ref,
                    x_ref, wg_ref, wu_ref, wd_ref, o_ref):
    del tile_ref, expert_ref
    n = pl.program_id(0)

    @pl.when(valid_ref[n] == 1)
    def _():
        x = x_ref[...].astype(BF16)
        rows = lax.broadcasted_iota(jnp.int32, (EXPERT_TILE, 1), 0)
        inside = (rows >= lo_ref[n]) & (rows < hi_ref[n])
        hid = _silu(_dot(x, wg_ref[0, 0].astype(BF16))) * _dot(x, wu_ref[0, 0].astype(BF16))
        hid = jnp.where(inside, hid, 0.0).astype(BF16)
        y = _dot(hid, wd_ref[0, 0].astype(BF16))

        @pl.when(first_ref[n] == 1)
        def _():
            o_ref[...] = y

        @pl.when(first_ref[n] == 0)
        def _():
            o_ref[...] += y


def _experts(work, xs, w_gate, w_up, w_down, layer):
    rows, k = xs.shape
    f = w_gate.shape[-1]
    n_items = work[0].shape[0]
    return pl.pallas_call(
        _experts_kernel,
        out_shape=jax.ShapeDtypeStruct((rows, k), F32),
        grid_spec=pltpu.PrefetchScalarGridSpec(
            num_scalar_prefetch=6,
            grid=(n_items,),
            in_specs=[
                pl.BlockSpec((EXPERT_TILE, k), lambda n, tile, *_: (tile[n], 0)),
                pl.BlockSpec((1, 1, k, f), lambda n, tile, expert, *_: (layer, expert[n], 0, 0)),
                pl.BlockSpec((1, 1, k, f), lambda n, tile, expert, *_: (layer, expert[n], 0, 0)),
                pl.BlockSpec((1, 1, f, k), lambda n, tile, expert, *_: (layer, expert[n], 0, 0)),
            ],
            out_specs=pl.BlockSpec((EXPERT_TILE, k), lambda n, tile, *_: (tile[n], 0)),
        ),
        compiler_params=_params("arbitrary"),
        name="moe_experts",
    )(*work, xs, w_gate, w_up, w_down)


COMBINE_TOKENS = 256


def _combine_kernel(pos1_ref, pos2_ref, h_ref, route_ref, ys_ref, o_ref, buf1, buf2, sem):
    base = pl.program_id(0) * COMBINE_TOKENS

    def issue(t, _):
        tok = base + t
        _row_copy(ys_ref, pos1_ref[tok], buf1, t, sem).start()
        _row_copy(ys_ref, pos2_ref[tok], buf2, t, sem).start()
        return 0

    lax.fori_loop(0, COMBINE_TOKENS, issue, 0)

    pltpu.make_async_copy(ys_ref.at[pl.ds(0, COMBINE_TOKENS)], buf1, sem).wait()
    pltpu.make_async_copy(ys_ref.at[pl.ds(0, COMBINE_TOKENS)], buf2, sem).wait()
    route = route_ref[...]
    w1 = route[:, ROUTE_W1:ROUTE_W1 + 1]
    w2 = route[:, ROUTE_W2:ROUTE_W2 + 1]
    o_ref[...] = h_ref[...] + w1 * buf1[...] + w2 * buf2[...]


def _combine(h, route, ys, pos1, pos2):
    m, k = h.shape
    tm = COMBINE_TOKENS
    return pl.pallas_call(
        _combine_kernel,
        out_shape=jax.ShapeDtypeStruct((m, k), F32),
        grid_spec=pltpu.PrefetchScalarGridSpec(
            num_scalar_prefetch=2,
            grid=(m // tm,),
            in_specs=[
                pl.BlockSpec((tm, k), lambda i, *_: (i, 0)),
                pl.BlockSpec((tm, LANES), lambda i, *_: (i, 0)),
                pl.BlockSpec(memory_space=pl.ANY),
            ],
            out_specs=pl.BlockSpec((tm, k), lambda i, *_: (i, 0)),
            scratch_shapes=[pltpu.VMEM((tm, k), F32), pltpu.VMEM((tm, k), F32),
                            pltpu.SemaphoreType.DMA],
        ),
        compiler_params=_params("arbitrary"),
        name="moe_combine",
    )(pos1, pos2, h, route, ys)


def _work_list(counts, n_rows):
    tm = EXPERT_TILE
    n_tiles = n_rows // tm
    n_items = n_tiles + N_EXPERTS - 1
    ends = jnp.cumsum(counts)
    starts = ends - counts
    tile_lo = jnp.arange(n_tiles, dtype=jnp.int32) * tm
    e_first = jnp.searchsorted(ends, tile_lo, side="right").astype(jnp.int32)
    e_last = jnp.searchsorted(ends, tile_lo + (tm - 1), side="right").astype(jnp.int32)
    per_tile = e_last - e_first + 1
    item0 = jnp.cumsum(per_tile) - per_tile
    total = item0[-1] + per_tile[-1]
    n = jnp.arange(n_items, dtype=jnp.int32)
    valid = n < total
    tile = jnp.clip(jnp.searchsorted(item0, n, side="right").astype(jnp.int32) - 1, 0, n_tiles - 1)
    tile = jnp.where(valid, tile, n_tiles - 1)
    expert = jnp.where(valid, e_first[tile] + (n - item0[tile]), e_last[-1])
    expert = jnp.clip(expert, 0, N_EXPERTS - 1)
    lo = jnp.clip(starts[expert] - tile * tm, 0, tm)
    hi = jnp.clip(ends[expert] - tile * tm, 0, tm)
    first = (n == item0[tile]) & valid
    as_i32 = lambda a: a.astype(jnp.int32)
    return tuple(map(as_i32, (tile, expert, lo, hi, first, valid))), starts


def _rmsnorm_kernel(x_ref, g_ref, o_ref):
    x = x_ref[...]
    o_ref[...] = x * _rms_scale(x) * g_ref[...]


def _rmsnorm(x, g, *, tm):
    m, k = x.shape
    return pl.pallas_call(
        _rmsnorm_kernel,
        out_shape=jax.ShapeDtypeStruct((m, k), F32),
        grid=(m // tm,),
        in_specs=[pl.BlockSpec((tm, k), lambda i: (i, 0)), pl.BlockSpec((1, k), lambda i: (0, 0))],
        out_specs=pl.BlockSpec((tm, k), lambda i: (i, 0)),
        compiler_params=_params("arbitrary"),
        name="final_rmsnorm",
    )(x, g.reshape(1, k))


def _router_weights(w_coarse, b_coarse, w_fine, b_fine):
    pad = LANES - N_GROUPS - N_EXPERTS
    w = jnp.concatenate([w_coarse, w_fine.reshape(D_MODEL, N_EXPERTS),
                         jnp.zeros((D_MODEL, pad), F32)], axis=1)
    b = jnp.concatenate([b_coarse, b_fine.reshape(N_EXPERTS), jnp.zeros((pad,), F32)])
    return w, b.reshape(1, LANES)


def _moe(h, norm_g, w_coarse, b_coarse, w_fine, b_fine, w_gate, w_up, w_down, layer):
    w_r, b_r = _router_weights(w_coarse, b_coarse, w_fine, b_fine)
    hn, route, counts = _router(h, norm_g, w_r, b_r, tm=256)
    counts = counts[0, ROUTER_FINE0:ROUTER_FINE0 + N_EXPERTS].astype(jnp.int32)
    work, starts = _work_list(counts, 2 * h.shape[0])
    col = lambda c: route[:, c].astype(jnp.int32)
    pos1 = starts[col(ROUTE_E1)] + col(ROUTE_R1)
    pos2 = starts[col(ROUTE_E2)] + col(ROUTE_R2)
    xs = _dispatch(hn, pos1, pos2)
    ys = _experts(work, xs, w_gate, w_up, w_down, layer)
    return _combine(h, route, ys, pos1, pos2)


def kernel(x, mix_norm, ffn_norm, ssm_w_in, ssm_conv_w, ssm_conv_b, ssm_dt_bias, ssm_a_log, ssm_d,
           ssm_norm_w, ssm_w_out, kv_norm, w_k, w_v, sb_w_q, sb_w_out, moe_w_coarse, moe_b_coarse,
           moe_w_fine, moe_b_fine, moe_w_gate, moe_w_up, moe_w_down, final_norm):
    batch, seq, d = x.shape
    h = x.reshape(batch * seq, d)

    w_in = jnp.pad(ssm_w_in[0], ((0, 0), (0, IN_PROJ_PAD - IN_PROJ_DIM))).astype(BF16)
    zx = _norm_matmul(h, mix_norm[0], w_in, tm=1024, tn=1152, out_dtype=F32)
    expand = lambda p: jnp.repeat(p, SSM_HEAD_DIM).reshape(1, D_INNER)
    yn = _ssd(zx, ssm_conv_w[0], ssm_conv_b[0], expand(ssm_dt_bias[0]), expand(ssm_a_log[0]),
              expand(ssm_d[0]), ssm_norm_w[0], batch=batch, seq=seq)
    h = _matmul_res(yn, ssm_w_out[0].astype(BF16), h, tm=1024, tn=512)
    h = _moe(h, ffn_norm[0], moe_w_coarse[0], moe_b_coarse[0], moe_w_fine[0], moe_b_fine[0],
             moe_w_gate, moe_w_up, moe_w_down, 0)

    w_kv = jnp.concatenate([w_k, w_v], axis=1).astype(BF16)
    kv = _norm_matmul(h, kv_norm, w_kv, tm=1024, tn=1024, out_dtype=BF16)

    q = _norm_matmul(h, mix_norm[1], sb_w_q[0].astype(BF16), tm=1024, tn=1024, out_dtype=BF16)
    o = _attention(q, kv, batch=batch, seq=seq)
    h = _matmul_res(o, sb_w_out[0].astype(BF16), h, tm=1024, tn=512)
    h = _moe(h, ffn_norm[1], moe_w_coarse[1], moe_b_coarse[1], moe_w_fine[1], moe_b_fine[1],
             moe_w_gate, moe_w_up, moe_w_down, 1)

    return _rmsnorm(h, final_norm, tm=512).reshape(batch, seq, d)
```

```python
import functools

import jax
import jax.numpy as jnp
from jax import lax
from jax.experimental import pallas as pl
from jax.experimental.pallas import tpu as pltpu

F32 = jnp.float32
BF16 = jnp.bfloat16
EPS = 1e-5

D_MODEL = 2048
D_INNER = 4096
SSM_HEAD_DIM = 64
SSM_HEADS = 64
SSM_GROUPS = 8
SSM_HEADS_PER_GROUP = 8
SSM_STATE = 128
GROUP_WIDTH = D_INNER // SSM_GROUPS
CONV_WIDTH = 4
CONV_DIM = D_INNER + 2 * SSM_GROUPS * SSM_STATE
IN_PROJ_DIM = D_INNER + CONV_DIM + SSM_HEADS
SB_HEADS = 16
SB_HEAD_DIM = 128
N_GROUPS = 4
EXPERTS_PER_GROUP = 4
N_EXPERTS = 16
D_EXPERT = 512

LANES = 128
SUBLANES = 8
IN_PROJ_PAD = 10368
DT_COL_BLOCK = (D_INNER + CONV_DIM) // LANES
SSD_CHUNK = 128
SSD_GROUPS_PER_STEP = 2
ATT_BLOCK = 256
VMEM_LIMIT = 56 * 1024 * 1024


def _dot(a, b):
    return jnp.dot(a, b, preferred_element_type=F32)


def _dot_nt(a, b):
    return lax.dot_general(a, b, (((1,), (1,)), ((), ())), preferred_element_type=F32)


def _split3(v):
    v1 = v.astype(BF16)
    r = v - v1.astype(F32)
    v2 = r.astype(BF16)
    r = r - v2.astype(F32)
    return v1, v2, r.astype(BF16)


def _split2(v):
    hi = v.astype(BF16)
    return hi, (v - hi.astype(F32)).astype(BF16)


def _dot3_left(m, v):
    v1, v2, v3 = _split3(v)
    return _dot(m, v1) + _dot(m, v2) + _dot(m, v3)


def _dot3_right(v, m):
    v1, v2, v3 = _split3(v)
    return _dot(v1, m) + _dot(v2, m) + _dot(v3, m)


def _softplus(x):
    return jnp.maximum(x, 0.0) + jnp.log(1.0 + jnp.exp(-jnp.abs(x)))


def _silu(x):
    half = 0.5 * x
    return half + half * jnp.tanh(half)


def _rms_scale(x):
    return lax.rsqrt(jnp.mean(x * x, axis=-1, keepdims=True) + EPS)


def _params(*sem):
    return pltpu.CompilerParams(dimension_semantics=sem, vmem_limit_bytes=VMEM_LIMIT)


def _norm_matmul_kernel(x_ref, g_ref, w_ref, o_ref, xn_ref):
    @pl.when(pl.program_id(1) == 0)
    def _():
        x = x_ref[...]
        xn_ref[...] = (x * _rms_scale(x) * g_ref[...]).astype(BF16)

    o_ref[...] = _dot(xn_ref[...], w_ref[...]).astype(o_ref.dtype)


def _norm_matmul(x, g, w, *, tm, tn, out_dtype):
    m, k = x.shape
    n = w.shape[1]
    return pl.pallas_call(
        _norm_matmul_kernel,
        out_shape=jax.ShapeDtypeStruct((m, n), out_dtype),
        grid=(m // tm, n // tn),
        in_specs=[
            pl.BlockSpec((tm, k), lambda i, j: (i, 0)),
            pl.BlockSpec((1, k), lambda i, j: (0, 0)),
            pl.BlockSpec((k, tn), lambda i, j: (0, j)),
        ],
        out_specs=pl.BlockSpec((tm, tn), lambda i, j: (i, j)),
        scratch_shapes=[pltpu.VMEM((tm, k), BF16)],
        compiler_params=_params("arbitrary", "arbitrary"),
        name="norm_matmul",
    )(x, g.reshape(1, k), w)


def _matmul_res_kernel(x_ref, w_ref, r_ref, o_ref):
    o_ref[...] = r_ref[...] + _dot(x_ref[...], w_ref[...])


def _matmul_res(x, w, res, *, tm, tn):
    m, k = x.shape
    n = w.shape[1]
    return pl.pallas_call(
        _matmul_res_kernel,
        out_shape=jax.ShapeDtypeStruct((m, n), F32),
        grid=(m // tm, n // tn),
        in_specs=[
            pl.BlockSpec((tm, k), lambda i, j: (i, 0)),
            pl.BlockSpec((k, tn), lambda i, j: (0, j)),
            pl.BlockSpec((tm, tn), lambda i, j: (i, j)),
        ],
        out_specs=pl.BlockSpec((tm, tn), lambda i, j: (i, j)),
        compiler_params=_params("arbitrary", "arbitrary"),
        name="matmul_res",
    )(x, w, res)


def _ssd_kernel(z_ref, x_ref, b_ref, c_ref, dt_ref,
                wx_ref, wb_ref, wc_ref, bx_ref, bb_ref, bc_ref,
                dtb_ref, alog_ref, dskip_ref, nw_ref, expand_ref,
                o_ref,
                h_ref, xbuf, bbuf, cbuf):
    L = SSD_CHUNK
    gw = GROUP_WIDTH
    ns = SSM_STATE
    c = pl.program_id(2)
    halo = SUBLANES
    groups = range(SSD_GROUPS_PER_STEP)

    @pl.when(c == 0)
    def _():
        h_ref[...] = jnp.zeros_like(h_ref)
        xbuf[0:halo, :] = jnp.zeros((halo, xbuf.shape[1]), F32)
        bbuf[0:halo, :] = jnp.zeros((halo, bbuf.shape[1]), F32)
        cbuf[0:halo, :] = jnp.zeros((halo, cbuf.shape[1]), F32)

    def conv_silu(raw_ref, buf, w_ref, bias_ref):
        raw = raw_ref[...]
        buf[halo:halo + L, :] = raw
        padded = buf[...]
        acc = bias_ref[...] + w_ref[CONV_WIDTH - 1:CONV_WIDTH, :] * raw
        for back in range(1, CONV_WIDTH):
            tap = pltpu.roll(padded, back, axis=0)[halo:halo + L, :]
            k = CONV_WIDTH - 1 - back
            acc = acc + w_ref[k:k + 1, :] * tap
        buf[0:halo, :] = raw[L - halo:L, :]
        return _silu(acc)

    bs = conv_silu(b_ref, bbuf, wb_ref, bb_ref)
    cs = conv_silu(c_ref, cbuf, wc_ref, bc_ref)
    bs_b = [bs[:, g * ns:(g + 1) * ns].astype(BF16) for g in groups]
    cs_b = [cs[:, g * ns:(g + 1) * ns].astype(BF16) for g in groups]
    cb = [_dot_nt(cs_b[g], bs_b[g]) for g in groups]
    h_prev = [h_ref[:, g * gw:(g + 1) * gw] for g in groups]
    y_off = [_dot(cs_b[g], h_prev[g].astype(BF16)) for g in groups]

    dt_parts = _split3(dt_ref[...])
    dt_raw = [sum(_dot(p, expand_ref[g]) for p in dt_parts) for g in groups]

    xs = conv_silu(x_ref, xbuf, wx_ref, bx_ref)

    row = lax.broadcasted_iota(jnp.int32, (L, L), 0)
    col = lax.broadcasted_iota(jnp.int32, (L, L), 1)
    causal = row >= col
    tril = causal.astype(BF16)
    triu = (row <= col).astype(BF16)
    hrow = lax.broadcasted_iota(jnp.int32, (SUBLANES, gw), 0)
    hcol = lax.broadcasted_iota(jnp.int32, (SUBLANES, gw), 1)
    pick = (hcol == hrow * SSM_HEAD_DIM).astype(BF16)

    dt, a_parts = [], []
    for g in groups:
        lanes = slice(g * gw, (g + 1) * gw)
        dt.append(_softplus(dt_raw[g] + dtb_ref[:, lanes]))
        a_parts.append(_split2(dt[g] * (-jnp.exp(alog_ref[:, lanes]))))
    acs = [sum(_dot(tril, p) for p in a_parts[g]) for g in groups]
    a_rows = [[_dot_nt(pick, p).astype(BF16) for p in a_parts[g]] for g in groups]
    acs_rows = [sum(_dot(p, triu) for p in a_rows[g]) for g in groups]

    xdt = [xs[:, g * gw:(g + 1) * gw] * dt[g] for g in groups]
    xdt_b = [v.astype(BF16) for v in xdt]
    y_diag = []
    for g in groups:
        parts = []
        for r in range(SSM_HEADS_PER_GROUP):
            lo = r * SSM_HEAD_DIM
            seg = acs[g][:, lo:lo + 1] - acs_rows[g][r:r + 1, :]
            decay = jnp.exp(jnp.where(causal, seg, -jnp.inf))
            m = (cb[g] * decay).astype(BF16)
            parts.append(_dot(m, xdt_b[g][:, lo:lo + SSM_HEAD_DIM]))
        y_diag.append(jnp.concatenate(parts, axis=1))

    a_last = [acs[g][L - 1:L, :] for g in groups]
    xd = [(xdt[g] * jnp.exp(a_last[g] - acs[g])).astype(BF16) for g in groups]
    states = [_dot(bs[:, g * ns:(g + 1) * ns].T.astype(BF16), xd[g]) for g in groups]
    for g in groups:
        lanes = slice(g * gw, (g + 1) * gw)
        h_ref[:, lanes] = h_prev[g] * jnp.exp(a_last[g]) + states[g]
        y = y_diag[g] + y_off[g] * jnp.exp(acs[g]) + xs[:, lanes] * dskip_ref[:, lanes]
        yz = y * _silu(z_ref[:, lanes])
        o_ref[:, lanes] = (yz * _rms_scale(yz) * nw_ref[:, lanes]).astype(o_ref.dtype)


def _ssd(zx, conv_w, conv_b, dtb_e, alog_e, dskip_e, norm_w, *, batch, seq):
    L = SSD_CHUNK
    nc = seq // L
    xw = SSD_GROUPS_PER_STEP * GROUP_WIDTH
    sw = SSD_GROUPS_PER_STEP * SSM_STATE
    x_blk = D_INNER // xw
    b_blk = (2 * D_INNER) // sw
    c_blk = b_blk + SSM_GROUPS * SSM_STATE // sw
    cw_b = D_INNER // sw
    cw_c = cw_b + SSM_GROUPS * SSM_STATE // sw

    def rows(b, g, c):
        return b * nc + c

    in_specs = [
        pl.BlockSpec((L, xw), lambda b, g, c: (rows(b, g, c), g)),
        pl.BlockSpec((L, xw), lambda b, g, c: (rows(b, g, c), x_blk + g)),
        pl.BlockSpec((L, sw), lambda b, g, c: (rows(b, g, c), b_blk + g)),
        pl.BlockSpec((L, sw), lambda b, g, c: (rows(b, g, c), c_blk + g)),
        pl.BlockSpec((L, LANES), lambda b, g, c: (rows(b, g, c), DT_COL_BLOCK)),
        pl.BlockSpec((CONV_WIDTH, xw), lambda b, g, c: (0, g)),
        pl.BlockSpec((CONV_WIDTH, sw), lambda b, g, c: (0, cw_b + g)),
        pl.BlockSpec((CONV_WIDTH, sw), lambda b, g, c: (0, cw_c + g)),
        pl.BlockSpec((1, xw), lambda b, g, c: (0, g)),
        pl.BlockSpec((1, sw), lambda b, g, c: (0, cw_b + g)),
        pl.BlockSpec((1, sw), lambda b, g, c: (0, cw_c + g)),
        pl.BlockSpec((1, xw), lambda b, g, c: (0, g)),
        pl.BlockSpec((1, xw), lambda b, g, c: (0, g)),
        pl.BlockSpec((1, xw), lambda b, g, c: (0, g)),
        pl.BlockSpec((1, xw), lambda b, g, c: (0, g)),
        pl.BlockSpec((SSD_GROUPS_PER_STEP, LANES, GROUP_WIDTH), lambda b, g, c: (g, 0, 0)),
    ]
    head_of_lane = (jnp.arange(SSM_GROUPS)[:, None, None] * SSM_HEADS_PER_GROUP
                    + jnp.arange(GROUP_WIDTH)[None, None, :] // SSM_HEAD_DIM)
    expand = (jnp.arange(LANES)[None, :, None] == head_of_lane).astype(BF16)
    cb2 = conv_b.reshape(1, CONV_DIM)
    return pl.pallas_call(
        _ssd_kernel,
        out_shape=jax.ShapeDtypeStruct((batch * seq, D_INNER), BF16),
        grid=(batch, SSM_GROUPS // SSD_GROUPS_PER_STEP, nc),
        in_specs=in_specs,
        out_specs=pl.BlockSpec((L, xw), lambda b, g, c: (rows(b, g, c), g)),
        scratch_shapes=[
            pltpu.VMEM((SSM_STATE, xw), F32),
            pltpu.VMEM((L + SUBLANES, xw), F32),
            pltpu.VMEM((L + SUBLANES, sw), F32),
            pltpu.VMEM((L + SUBLANES, sw), F32),
        ],
        compiler_params=_params("arbitrary", "arbitrary", "arbitrary"),
        name="ssd",
    )(zx, zx, zx, zx, zx, conv_w, conv_w, conv_w, cb2, cb2, cb2,
      dtb_e, alog_e, dskip_e, norm_w.reshape(1, D_INNER), expand)


ATT_HEADS_PER_STEP = 4


def _attn_kernel(q_ref, k_ref, v_ref, o_ref):
    tq = ATT_BLOCK
    dh = SB_HEAD_DIM
    i = pl.program_id(2)
    scale = dh ** -0.5
    row = lax.broadcasted_iota(jnp.int32, (tq, tq), 0)
    col = lax.broadcasted_iota(jnp.int32, (tq, tq), 1)
    strict = col < row
    row2 = lax.broadcasted_iota(jnp.int32, (2 * tq, tq), 0) & (tq - 1)
    col2 = lax.broadcasted_iota(jnp.int32, (2 * tq, tq), 1)
    neg_later = jnp.where(row2 > col2, -1.0, 0.0).astype(BF16)

    heads = range(ATT_HEADS_PER_STEP)

    def block(j, state, diagonal):
        start = pl.multiple_of(j * tq, tq)
        cols = [slice(h * dh, (h + 1) * dh) for h in heads]
        z = [_dot_nt(q_ref[:, cols[h]], k_ref[pl.ds(start, tq), cols[h]]) * scale for h in heads]
        drop, logit, parts = [], [], []
        for h in heads:
            neg_abs = lax.bitcast_convert_type(
                lax.bitcast_convert_type(z[h], jnp.uint32) | jnp.uint32(0x80000000), F32)
            sp = jnp.maximum(z[h], 0.0) + jnp.log(1.0 + jnp.exp(neg_abs))
            logit.append(z[h] - sp)
            if diagonal:
                sp = jnp.where(strict, sp, 0.0)
            hi, lo = _split2(sp)
            parts.append(jnp.concatenate([hi, lo], axis=1))
            drop.append(sp)
        suffix = [_dot(parts[h], neg_later) for h in heads]
        weights = []
        for h in heads:
            w = jnp.exp(logit[h] + suffix[h] + state[h][0])
            if diagonal:
                w = jnp.where(strict, w, 0.0)
            weights.append(w.astype(BF16))
        out = [_dot(weights[h], v_ref[pl.ds(start, tq), cols[h]]) for h in heads]
        return tuple((state[h][0] - jnp.sum(drop[h], axis=1, keepdims=True), state[h][1] + out[h])
                     for h in heads)

    zero = (jnp.zeros((tq, 1), F32), jnp.zeros((tq, dh), F32))
    state = block(i, (zero,) * ATT_HEADS_PER_STEP, True)
    state = lax.fori_loop(0, i, lambda t, s: block(i - 1 - t, s, False), state)
    for h in range(ATT_HEADS_PER_STEP):
        o_ref[:, h * dh:(h + 1) * dh] = state[h][1].astype(o_ref.dtype)


def _attention(q, kv, *, batch, seq):
    tq = ATT_BLOCK
    nq = seq // tq
    width = ATT_HEADS_PER_STEP * SB_HEAD_DIM
    n_steps = SB_HEADS // ATT_HEADS_PER_STEP
    return pl.pallas_call(
        _attn_kernel,
        out_shape=jax.ShapeDtypeStruct((batch * seq, D_MODEL), BF16),
        grid=(batch, n_steps, nq),
        in_specs=[
            pl.BlockSpec((tq, width), lambda b, h, i: (b * nq + i, h)),
            pl.BlockSpec((seq, width), lambda b, h, i: (b, h)),
            pl.BlockSpec((seq, width), lambda b, h, i: (b, n_steps + h)),
        ],
        out_specs=pl.BlockSpec((tq, width), lambda b, h, i: (b * nq + i, h)),
        compiler_params=_params("arbitrary", "arbitrary", "arbitrary"),
        name="sb_attention",
    )(q, kv, kv)


ROUTER_FINE0 = N_GROUPS


ROUTE_E1, ROUTE_E2, ROUTE_W1, ROUTE_W2, ROUTE_R1, ROUTE_R2 = range(6)


def _router_kernel(h_ref, g_ref, w_ref, b_ref, hn_ref, route_ref, counts_ref, carry_ref):
    @pl.when(pl.program_id(0) == 0)
    def _():
        carry_ref[...] = jnp.zeros_like(carry_ref)

    x = h_ref[...]
    hn = x * _rms_scale(x) * g_ref[...]
    hn_ref[...] = hn
    x1 = hn.astype(BF16)
    x2 = (hn - x1.astype(F32)).astype(BF16)
    w = w_ref[...]
    w1 = w.astype(BF16)
    w2 = (w - w1.astype(F32)).astype(BF16)
    logits = _dot(x1, w1) + _dot(x2, w1) + _dot(x1, w2) + b_ref[...]

    lane_i = lax.broadcasted_iota(jnp.int32, logits.shape, 1)
    lane = lane_i.astype(F32)
    neg = -jnp.inf

    def first_argmax(v, vmax):
        return jnp.min(jnp.where(v == vmax, lane, float(LANES)), axis=1, keepdims=True)

    coarse = jnp.where(lane_i < N_GROUPS, logits, neg)
    cmax = jnp.max(coarse, axis=1, keepdims=True)
    p_g = 1.0 / jnp.sum(jnp.exp(coarse - cmax), axis=1, keepdims=True)
    g_idx = first_argmax(coarse, cmax)
    fine_lo = ROUTER_FINE0 + g_idx * EXPERTS_PER_GROUP
    in_group = (lane >= fine_lo) & (lane < fine_lo + EXPERTS_PER_GROUP)
    fine = jnp.where(in_group, logits, neg)
    m1 = jnp.max(fine, axis=1, keepdims=True)
    denom = jnp.sum(jnp.exp(fine - m1), axis=1, keepdims=True)
    i1 = first_argmax(fine, m1)
    rest = jnp.where(lane == i1, neg, fine)
    m2 = jnp.max(rest, axis=1, keepdims=True)
    i2 = first_argmax(rest, m2)
    p1 = 1.0 / denom
    p2 = jnp.exp(m2 - m1) / denom
    norm = p1 + p2
    gate1 = p_g * (p1 / norm)
    gate2 = p_g * (p2 / norm)

    tm = x.shape[0]
    chosen = (lane == i1) | (lane == i2)
    r = lax.broadcasted_iota(jnp.int32, (tm, tm), 0)
    c = lax.broadcasted_iota(jnp.int32, (tm, tm), 1)
    before = _dot((c < r).astype(BF16), chosen.astype(BF16)) + carry_ref[...]
    rank1 = jnp.sum(jnp.where(lane == i1, before, 0.0), axis=1, keepdims=True)
    rank2 = jnp.sum(jnp.where(lane == i2, before, 0.0), axis=1, keepdims=True)
    carry_ref[...] += jnp.sum(chosen.astype(F32), axis=0, keepdims=True)
    counts_ref[...] = carry_ref[...]

    route = jnp.zeros_like(logits)
    for slot, val in ((ROUTE_E1, i1 - ROUTER_FINE0), (ROUTE_E2, i2 - ROUTER_FINE0),
                      (ROUTE_W1, gate1), (ROUTE_W2, gate2), (ROUTE_R1, rank1), (ROUTE_R2, rank2)):
        route = jnp.where(lane_i == slot, val, route)
    route_ref[...] = route


def _router(h, g, w_router, b_router, *, tm):
    m, k = h.shape
    return pl.pallas_call(
        _router_kernel,
        out_shape=(jax.ShapeDtypeStruct((m, k), F32), jax.ShapeDtypeStruct((m, LANES), F32),
                   jax.ShapeDtypeStruct((1, LANES), F32)),
        grid=(m // tm,),
        in_specs=[
            pl.BlockSpec((tm, k), lambda i: (i, 0)),
            pl.BlockSpec((1, k), lambda i: (0, 0)),
            pl.BlockSpec((k, LANES), lambda i: (0, 0)),
            pl.BlockSpec((1, LANES), lambda i: (0, 0)),
        ],
        out_specs=(pl.BlockSpec((tm, k), lambda i: (i, 0)),
                   pl.BlockSpec((tm, LANES), lambda i: (i, 0)),
                   pl.BlockSpec((1, LANES), lambda i: (0, 0))),
        scratch_shapes=[pltpu.VMEM((1, LANES), F32)],
        compiler_params=_params("arbitrary"),
        name="moe_router",
    )(h, g.reshape(1, k), w_router, b_router)


DISPATCH_TOKENS = 512
DMA_ISSUE_UNROLL = 8


def _row_copy(src, src_row, dst, dst_row, sem):
    return pltpu.make_async_copy(src.at[pl.ds(src_row, 1)], dst.at[pl.ds(dst_row, 1)], sem)


def _dispatch_kernel(pos1_ref, pos2_ref, hn_ref, xs_ref, sem):
    base = pl.program_id(0) * DISPATCH_TOKENS

    def issue(t, _):
        tok = base + t
        _row_copy(hn_ref, t, xs_ref, pos1_ref[tok], sem).start()
        _row_copy(hn_ref, t, xs_ref, pos2_ref[tok], sem).start()
        return 0

    lax.fori_loop(0, DISPATCH_TOKENS, issue, 0, unroll=DMA_ISSUE_UNROLL)

    for _ in range(2):
        pltpu.make_async_copy(hn_ref, xs_ref.at[pl.ds(0, DISPATCH_TOKENS)], sem).wait()


def _dispatch(hn, pos1, pos2):
    m, k = hn.shape
    return pl.pallas_call(
        _dispatch_kernel,
        out_shape=jax.ShapeDtypeStruct((2 * m, k), F32),
        grid_spec=pltpu.PrefetchScalarGridSpec(
            num_scalar_prefetch=2,
            grid=(m // DISPATCH_TOKENS,),
            in_specs=[pl.BlockSpec((DISPATCH_TOKENS, k), lambda i, *_: (i, 0))],
            out_specs=pl.BlockSpec(memory_space=pl.ANY),
            scratch_shapes=[pltpu.SemaphoreType.DMA],
        ),
        compiler_params=_params("arbitrary"),
        name="moe_dispatch",
    )(pos1, pos2, hn)


EXPERT_TILE = 256
EXPERT_HIDDEN_SLICES = 2


def _experts_kernel(tile_ref, expert_ref, lo_ref, hi_ref, first_ref, valid_ref,
                    x_ref, wg_ref, wu_ref, wd_ref, o_ref):
    del tile_ref, expert_ref
    n = pl.program_id(0)

    @pl.when(valid_ref[n] == 1)
    def _():
        x = x_ref[...].astype(BF16)
        rows = lax.broadcasted_iota(jnp.int32, (EXPERT_TILE, 1), 0)
        inside = (rows >= lo_ref[n]) & (rows < hi_ref[n])
        f = wg_ref.shape[-1]
        fs = f // EXPERT_HIDDEN_SLICES
        cols = [slice(s * fs, (s + 1) * fs) for s in range(EXPERT_HIDDEN_SLICES)]
        pre = [(_dot(x, wg_ref[0, 0, :, c].astype(BF16)), _dot(x, wu_ref[0, 0, :, c].astype(BF16)))
               for c in cols]
        hid = [jnp.where(inside, _silu(g) * u, 0.0).astype(BF16) for g, u in pre]
        y = sum(_dot(hid[s], wd_ref[0, 0, c, :].astype(BF16)) for s, c in enumerate(cols))

        @pl.when(first_ref[n] == 1)
        def _():
            o_ref[...] = y

        @pl.when(first_ref[n] == 0)
        def _():
            o_ref[...] += y


def _experts(work, xs, w_gate, w_up, w_down, layer):
    rows, k = xs.shape
    f = w_gate.shape[-1]
    n_items = work[0].shape[0]
    return pl.pallas_call(
        _experts_kernel,
        out_shape=jax.ShapeDtypeStruct((rows, k), F32),
        grid_spec=pltpu.PrefetchScalarGridSpec(
            num_scalar_prefetch=6,
            grid=(n_items,),
            in_specs=[
                pl.BlockSpec((EXPERT_TILE, k), lambda n, tile, *_: (tile[n], 0)),
                pl.BlockSpec((1, 1, k, f), lambda n, tile, expert, *_: (layer, expert[n], 0, 0)),
                pl.BlockSpec((1, 1, k, f), lambda n, tile, expert, *_: (layer, expert[n], 0, 0)),
                pl.BlockSpec((1, 1, f, k), lambda n, tile, expert, *_: (layer, expert[n], 0, 0)),
            ],
            out_specs=pl.BlockSpec((EXPERT_TILE, k), lambda n, tile, *_: (tile[n], 0)),
        ),
        compiler_params=_params("arbitrary"),
        name="moe_experts",
    )(*work, xs, w_gate, w_up, w_down)


COMBINE_TOKENS = 256


def _combine_kernel(pos1_ref, pos2_ref, h_ref, route_ref, ys_ref, *rest, final_norm):
    if final_norm:
        g_ref, o_ref, buf1, buf2, sem = rest
    else:
        o_ref, buf1, buf2, sem = rest
    base = pl.program_id(0) * COMBINE_TOKENS

    def issue(t, _):
        tok = base + t
        _row_copy(ys_ref, pos1_ref[tok], buf1, t, sem).start()
        _row_copy(ys_ref, pos2_ref[tok], buf2, t, sem).start()
        return 0

    lax.fori_loop(0, COMBINE_TOKENS, issue, 0, unroll=DMA_ISSUE_UNROLL)

    pltpu.make_async_copy(ys_ref.at[pl.ds(0, COMBINE_TOKENS)], buf1, sem).wait()
    pltpu.make_async_copy(ys_ref.at[pl.ds(0, COMBINE_TOKENS)], buf2, sem).wait()
    route = route_ref[...]
    w1 = route[:, ROUTE_W1:ROUTE_W1 + 1]
    w2 = route[:, ROUTE_W2:ROUTE_W2 + 1]
    out = h_ref[...] + w1 * buf1[...] + w2 * buf2[...]
    if final_norm:
        out = out * _rms_scale(out) * g_ref[...]
    o_ref[...] = out


def _combine(h, route, ys, pos1, pos2, final_gain=None):
    m, k = h.shape
    tm = COMBINE_TOKENS
    final_norm = final_gain is not None
    in_specs = [
        pl.BlockSpec((tm, k), lambda i, *_: (i, 0)),
        pl.BlockSpec((tm, LANES), lambda i, *_: (i, 0)),
        pl.BlockSpec(memory_space=pl.ANY),
    ]
    operands = [h, route, ys]
    if final_norm:
        in_specs.append(pl.BlockSpec((1, k), lambda i, *_: (0, 0)))
        operands.append(final_gain.reshape(1, k))
    return pl.pallas_call(
        functools.partial(_combine_kernel, final_norm=final_norm),
        out_shape=jax.ShapeDtypeStruct((m, k), F32),
        grid_spec=pltpu.PrefetchScalarGridSpec(
            num_scalar_prefetch=2,
            grid=(m // tm,),
            in_specs=in_specs,
            out_specs=pl.BlockSpec((tm, k), lambda i, *_: (i, 0)),
            scratch_shapes=[pltpu.VMEM((tm, k), F32), pltpu.VMEM((tm, k), F32),
                            pltpu.SemaphoreType.DMA],
        ),
        compiler_params=_params("arbitrary"),
        name="moe_combine",
    )(pos1, pos2, *operands)


def _work_list(counts, n_rows):
    tm = EXPERT_TILE
    n_tiles = n_rows // tm
    n_items = n_tiles + N_EXPERTS - 1
    i32 = jnp.int32
    count_le = lambda sorted_vals, q: jnp.sum((sorted_vals[None, :] <= q[:, None]).astype(i32), axis=1)
    pick = lambda table, idx: jnp.sum(
        jnp.where(idx[:, None] == jnp.arange(table.shape[0], dtype=i32)[None, :], table[None, :], 0), axis=1)
    e_ids = jnp.arange(N_EXPERTS, dtype=i32)
    ends = jnp.sum(jnp.where(e_ids[None, :] <= e_ids[:, None], counts[None, :], 0), axis=1)
    starts = ends - counts
    tile_ids = jnp.arange(n_tiles, dtype=i32)
    tile_lo = tile_ids * tm
    e_first = count_le(ends, tile_lo)
    e_last = count_le(ends, tile_lo + (tm - 1))
    per_tile = e_last - e_first + 1
    item0 = jnp.sum(jnp.where(tile_ids[None, :] < tile_ids[:, None], per_tile[None, :], 0), axis=1)
    total = jnp.sum(per_tile)
    n = jnp.arange(n_items, dtype=i32)
    valid = n < total
    tile = jnp.where(valid, count_le(item0, n) - 1, n_tiles - 1)
    expert = jnp.where(valid, pick(e_first, tile) + (n - pick(item0, tile)), e_last[-1])
    expert = jnp.clip(expert, 0, N_EXPERTS - 1)
    lo = jnp.clip(pick(starts, expert) - tile * tm, 0, tm)
    hi = jnp.clip(pick(ends, expert) - tile * tm, 0, tm)
    first = (n == pick(item0, tile)) & valid
    as_i32 = lambda a: a.astype(i32)
    return tuple(map(as_i32, (tile, expert, lo, hi, first, valid))), starts


def _router_weights(w_coarse, b_coarse, w_fine, b_fine):
    pad = LANES - N_GROUPS - N_EXPERTS
    w = jnp.concatenate([w_coarse, w_fine.reshape(D_MODEL, N_EXPERTS),
                         jnp.zeros((D_MODEL, pad), F32)], axis=1)
    b = jnp.concatenate([b_coarse, b_fine.reshape(N_EXPERTS), jnp.zeros((pad,), F32)])
    return w, b.reshape(1, LANES)


def _moe(h, norm_g, w_coarse, b_coarse, w_fine, b_fine, w_gate, w_up, w_down, layer, final_gain=None):
    w_r, b_r = _router_weights(w_coarse, b_coarse, w_fine, b_fine)
    hn, route, counts = _router(h, norm_g, w_r, b_r, tm=256)
    counts = counts[0, ROUTER_FINE0:ROUTER_FINE0 + N_EXPERTS].astype(jnp.int32)
    work, starts = _work_list(counts, 2 * h.shape[0])
    ids = route[:, :ROUTE_R2 + 1].astype(jnp.int32)
    e_ids = jnp.arange(N_EXPERTS, dtype=jnp.int32)[None, :]
    start_of = lambda e: jnp.sum(jnp.where(e[:, None] == e_ids, starts[None, :], 0), axis=1)
    pos1 = start_of(ids[:, ROUTE_E1]) + ids[:, ROUTE_R1]
    pos2 = start_of(ids[:, ROUTE_E2]) + ids[:, ROUTE_R2]
    xs = _dispatch(hn, pos1, pos2)
    ys = _experts(work, xs, w_gate, w_up, w_down, layer)
    return _combine(h, route, ys, pos1, pos2, final_gain)


def kernel(x, mix_norm, ffn_norm, ssm_w_in, ssm_conv_w, ssm_conv_b, ssm_dt_bias, ssm_a_log, ssm_d,
           ssm_norm_w, ssm_w_out, kv_norm, w_k, w_v, sb_w_q, sb_w_out, moe_w_coarse, moe_b_coarse,
           moe_w_fine, moe_b_fine, moe_w_gate, moe_w_up, moe_w_down, final_norm):
    batch, seq, d = x.shape
    h = x.reshape(batch * seq, d)

    w_in = jnp.pad(ssm_w_in[0], ((0, 0), (0, IN_PROJ_PAD - IN_PROJ_DIM))).astype(BF16)
    zx = _norm_matmul(h, mix_norm[0], w_in, tm=1024, tn=1152, out_dtype=F32)
    expand = lambda p: jnp.repeat(p, SSM_HEAD_DIM).reshape(1, D_INNER)
    yn = _ssd(zx, ssm_conv_w[0], ssm_conv_b[0], expand(ssm_dt_bias[0]), expand(ssm_a_log[0]),
              expand(ssm_d[0]), ssm_norm_w[0], batch=batch, seq=seq)
    h = _matmul_res(yn, ssm_w_out[0].astype(BF16), h, tm=1024, tn=512)
    h = _moe(h, ffn_norm[0], moe_w_coarse[0], moe_b_coarse[0], moe_w_fine[0], moe_b_fine[0],
             moe_w_gate, moe_w_up, moe_w_down, 0)

    w_kv = jnp.concatenate([w_k.astype(BF16), w_v.astype(BF16)], axis=1)
    kv = _norm_matmul(h, kv_norm, w_kv, tm=1024, tn=1024, out_dtype=BF16)

    q = _norm_matmul(h, mix_norm[1], sb_w_q[0].astype(BF16), tm=1024, tn=1024, out_dtype=BF16)
    o = _attention(q, kv, batch=batch, seq=seq)
    h = _matmul_res(o, sb_w_out[0].astype(BF16), h, tm=1024, tn=512)
    h = _moe(h, ffn_norm[1], moe_w_coarse[1], moe_b_coarse[1], moe_w_fine[1], moe_b_fine[1],
             moe_w_gate, moe_w_up, moe_w_down, 1, final_gain=final_norm)
    return h.reshape(batch, seq, d)
```

```python
import functools

import jax
import jax.numpy as jnp
from jax import lax
from jax.experimental import pallas as pl
from jax.experimental.pallas import tpu as pltpu

F32 = jnp.float32
BF16 = jnp.bfloat16
EPS = 1e-5

D_MODEL = 2048
D_INNER = 4096
SSM_HEAD_DIM = 64
SSM_HEADS = 64
SSM_GROUPS = 8
SSM_HEADS_PER_GROUP = 8
SSM_STATE = 128
GROUP_WIDTH = D_INNER // SSM_GROUPS
CONV_WIDTH = 4
CONV_DIM = D_INNER + 2 * SSM_GROUPS * SSM_STATE
IN_PROJ_DIM = D_INNER + CONV_DIM + SSM_HEADS
SB_HEADS = 16
SB_HEAD_DIM = 128
N_GROUPS = 4
EXPERTS_PER_GROUP = 4
N_EXPERTS = 16
D_EXPERT = 512

LANES = 128
SUBLANES = 8
DT_COL_BLOCK = (D_INNER + CONV_DIM) // LANES
SSD_CHUNK = 128
SSD_GROUPS_PER_STEP = 4
ATT_BLOCK = 256
VMEM_LIMIT = 56 * 1024 * 1024


def _dot(a, b):
    return jnp.dot(a, b, preferred_element_type=F32)


def _dot_nt(a, b):
    return lax.dot_general(a, b, (((1,), (1,)), ((), ())), preferred_element_type=F32)


def _split3(v):
    v1 = v.astype(BF16)
    r = v - v1.astype(F32)
    v2 = r.astype(BF16)
    r = r - v2.astype(F32)
    return v1, v2, r.astype(BF16)


def _split2(v):
    hi = v.astype(BF16)
    return hi, (v - hi.astype(F32)).astype(BF16)


def _dot3_left(m, v):
    v1, v2, v3 = _split3(v)
    return _dot(m, v1) + _dot(m, v2) + _dot(m, v3)


def _dot3_right(v, m):
    v1, v2, v3 = _split3(v)
    return _dot(v1, m) + _dot(v2, m) + _dot(v3, m)


def _softplus(x):
    return jnp.maximum(x, 0.0) + jnp.log(1.0 + jnp.exp(-jnp.abs(x)))


def _silu(x):
    half = 0.5 * x
    return half + half * jnp.tanh(half)


def _rms_scale(x):
    return lax.rsqrt(jnp.mean(x * x, axis=-1, keepdims=True) + EPS)


def _params(*sem):
    return pltpu.CompilerParams(dimension_semantics=sem, vmem_limit_bytes=VMEM_LIMIT)


def _norm_matmul_kernel(x_ref, g_ref, *rest, tiles):
    w_refs, (o_ref, xn_ref) = rest[:len(tiles)], rest[len(tiles):]
    j = pl.program_id(1)

    @pl.when(j == 0)
    def _():
        x = x_ref[...]
        xn_ref[...] = (x * _rms_scale(x) * g_ref[...]).astype(BF16)

    first = 0
    for w_ref, count in zip(w_refs, tiles):
        @pl.when((j >= first) & (j < first + count))
        def _(w_ref=w_ref):
            o_ref[...] = _dot(xn_ref[...], w_ref[...].astype(BF16)).astype(o_ref.dtype)
        first += count


def _norm_matmul(x, g, weights, *, tm, tn, out_dtype):
    m, k = x.shape
    tiles = tuple(pl.cdiv(w.shape[1], tn) for w in weights)
    assert all(w.shape[1] % tn == 0 for w in weights[:-1])
    n = sum(w.shape[1] for w in weights)
    firsts = [sum(tiles[:a]) for a in range(len(tiles))]
    w_specs = [
        pl.BlockSpec((k, tn), lambda i, j, lo=lo, count=count: (0, jnp.clip(j - lo, 0, count - 1)))
        for lo, count in zip(firsts, tiles)
    ]
    return pl.pallas_call(
        functools.partial(_norm_matmul_kernel, tiles=tiles),
        out_shape=jax.ShapeDtypeStruct((m, n), out_dtype),
        grid=(m // tm, sum(tiles)),
        in_specs=[
            pl.BlockSpec((tm, k), lambda i, j: (i, 0)),
            pl.BlockSpec((1, k), lambda i, j: (0, 0)),
            *w_specs,
        ],
        out_specs=pl.BlockSpec((tm, tn), lambda i, j: (i, j)),
        scratch_shapes=[pltpu.VMEM((tm, k), BF16)],
        compiler_params=_params("arbitrary", "arbitrary"),
        name="norm_matmul",
    )(x, g.reshape(1, k), *weights)


def _matmul_res_kernel(x_ref, w_ref, r_ref, o_ref):
    o_ref[...] = r_ref[...] + _dot(x_ref[...], w_ref[...].astype(BF16))


def _matmul_res(x, w, res, *, tm, tn):
    m, k = x.shape
    n = w.shape[1]
    return pl.pallas_call(
        _matmul_res_kernel,
        out_shape=jax.ShapeDtypeStruct((m, n), F32),
        grid=(m // tm, n // tn),
        in_specs=[
            pl.BlockSpec((tm, k), lambda i, j: (i, 0)),
            pl.BlockSpec((k, tn), lambda i, j: (0, j)),
            pl.BlockSpec((tm, tn), lambda i, j: (i, j)),
        ],
        out_specs=pl.BlockSpec((tm, tn), lambda i, j: (i, j)),
        compiler_params=_params("arbitrary", "arbitrary"),
        name="matmul_res",
    )(x, w, res)


def _ssd_kernel(z_ref, x_ref, b_ref, c_ref, dt_ref,
                wx_ref, wb_ref, wc_ref, bx_ref, bb_ref, bc_ref,
                dtb_ref, alog_ref, dskip_ref, nw_ref, expand_ref,
                o_ref,
                h_ref, xbuf, bbuf, cbuf):
    L = SSD_CHUNK
    gw = GROUP_WIDTH
    ns = SSM_STATE
    c = pl.program_id(2)
    halo = SUBLANES
    groups = range(SSD_GROUPS_PER_STEP)

    @pl.when(c == 0)
    def _():
        h_ref[...] = jnp.zeros_like(h_ref)
        xbuf[0:halo, :] = jnp.zeros((halo, xbuf.shape[1]), F32)
        bbuf[0:halo, :] = jnp.zeros((halo, bbuf.shape[1]), F32)
        cbuf[0:halo, :] = jnp.zeros((halo, cbuf.shape[1]), F32)

    def conv_silu(raw_ref, buf, w_ref, bias_ref):
        raw = raw_ref[...]
        buf[halo:halo + L, :] = raw
        padded = buf[...]
        acc = bias_ref[...] + w_ref[CONV_WIDTH - 1:CONV_WIDTH, :] * raw
        for back in range(1, CONV_WIDTH):
            tap = pltpu.roll(padded, back, axis=0)[halo:halo + L, :]
            k = CONV_WIDTH - 1 - back
            acc = acc + w_ref[k:k + 1, :] * tap
        buf[0:halo, :] = raw[L - halo:L, :]
        return _silu(acc)

    bs = conv_silu(b_ref, bbuf, wb_ref, bb_ref)
    cs = conv_silu(c_ref, cbuf, wc_ref, bc_ref)
    bs_b = [bs[:, g * ns:(g + 1) * ns].astype(BF16) for g in groups]
    cs_b = [cs[:, g * ns:(g + 1) * ns].astype(BF16) for g in groups]
    cb = [_dot_nt(cs_b[g], bs_b[g]) for g in groups]
    h_prev = [h_ref[:, g * gw:(g + 1) * gw] for g in groups]
    y_off = [_dot(cs_b[g], h_prev[g].astype(BF16)) for g in groups]

    dt_lane = lax.broadcasted_iota(jnp.int32, (L, LANES), 1)
    dt_parts = _split3(jnp.where(dt_lane < SSM_HEADS, dt_ref[...], 0.0))
    dt_raw = [sum(_dot(p, expand_ref[g]) for p in dt_parts) for g in groups]

    xs = conv_silu(x_ref, xbuf, wx_ref, bx_ref)

    row = lax.broadcasted_iota(jnp.int32, (L, L), 0)
    col = lax.broadcasted_iota(jnp.int32, (L, L), 1)
    causal = row >= col
    tril = causal.astype(BF16)
    triu = (row <= col).astype(BF16)
    hrow = lax.broadcasted_iota(jnp.int32, (SUBLANES, gw), 0)
    hcol = lax.broadcasted_iota(jnp.int32, (SUBLANES, gw), 1)
    pick = (hcol == hrow * SSM_HEAD_DIM).astype(BF16)

    dt, a_parts = [], []
    for g in groups:
        lanes = slice(g * gw, (g + 1) * gw)
        dt.append(_softplus(dt_raw[g] + dtb_ref[:, lanes]))
        a_parts.append(_split2(dt[g] * (-jnp.exp(alog_ref[:, lanes]))))
    acs = [sum(_dot(tril, p) for p in a_parts[g]) for g in groups]
    a_rows = [[_dot_nt(pick, p).astype(BF16) for p in a_parts[g]] for g in groups]
    acs_rows = [sum(_dot(p, triu) for p in a_rows[g]) for g in groups]

    xdt = [xs[:, g * gw:(g + 1) * gw] * dt[g] for g in groups]
    xdt_b = [v.astype(BF16) for v in xdt]
    y_diag = []
    for g in groups:
        parts = []
        for r in range(SSM_HEADS_PER_GROUP):
            lo = r * SSM_HEAD_DIM
            seg = acs[g][:, lo:lo + 1] - acs_rows[g][r:r + 1, :]
            decay = jnp.exp(jnp.where(causal, seg, -jnp.inf))
            m = (cb[g] * decay).astype(BF16)
            parts.append(_dot(m, xdt_b[g][:, lo:lo + SSM_HEAD_DIM]))
        y_diag.append(jnp.concatenate(parts, axis=1))

    a_last = [acs[g][L - 1:L, :] for g in groups]
    xd = [(xdt[g] * jnp.exp(a_last[g] - acs[g])).astype(BF16) for g in groups]
    states = [_dot(bs[:, g * ns:(g + 1) * ns].T.astype(BF16), xd[g]) for g in groups]
    for g in groups:
        lanes = slice(g * gw, (g + 1) * gw)
        h_ref[:, lanes] = h_prev[g] * jnp.exp(a_last[g]) + states[g]
        y = y_diag[g] + y_off[g] * jnp.exp(acs[g]) + xs[:, lanes] * dskip_ref[:, lanes]
        yz = y * _silu(z_ref[:, lanes])
        o_ref[:, lanes] = (yz * _rms_scale(yz) * nw_ref[:, lanes]).astype(o_ref.dtype)


def _ssd(zx, conv_w, conv_b, dtb_e, alog_e, dskip_e, norm_w, *, batch, seq):
    L = SSD_CHUNK
    nc = seq // L
    xw = SSD_GROUPS_PER_STEP * GROUP_WIDTH
    sw = SSD_GROUPS_PER_STEP * SSM_STATE
    x_blk = D_INNER // xw
    b_blk = (2 * D_INNER) // sw
    c_blk = b_blk + SSM_GROUPS * SSM_STATE // sw
    cw_b = D_INNER // sw
    cw_c = cw_b + SSM_GROUPS * SSM_STATE // sw

    def rows(b, g, c):
        return b * nc + c

    in_specs = [
        pl.BlockSpec((L, xw), lambda b, g, c: (rows(b, g, c), g)),
        pl.BlockSpec((L, xw), lambda b, g, c: (rows(b, g, c), x_blk + g)),
        pl.BlockSpec((L, sw), lambda b, g, c: (rows(b, g, c), b_blk + g)),
        pl.BlockSpec((L, sw), lambda b, g, c: (rows(b, g, c), c_blk + g)),
        pl.BlockSpec((L, LANES), lambda b, g, c: (rows(b, g, c), DT_COL_BLOCK)),
        pl.BlockSpec((CONV_WIDTH, xw), lambda b, g, c: (0, g)),
        pl.BlockSpec((CONV_WIDTH, sw), lambda b, g, c: (0, cw_b + g)),
        pl.BlockSpec((CONV_WIDTH, sw), lambda b, g, c: (0, cw_c + g)),
        pl.BlockSpec((1, xw), lambda b, g, c: (0, g)),
        pl.BlockSpec((1, sw), lambda b, g, c: (0, cw_b + g)),
        pl.BlockSpec((1, sw), lambda b, g, c: (0, cw_c + g)),
        pl.BlockSpec((1, xw), lambda b, g, c: (0, g)),
        pl.BlockSpec((1, xw), lambda b, g, c: (0, g)),
        pl.BlockSpec((1, xw), lambda b, g, c: (0, g)),
        pl.BlockSpec((1, xw), lambda b, g, c: (0, g)),
        pl.BlockSpec((SSD_GROUPS_PER_STEP, LANES, GROUP_WIDTH), lambda b, g, c: (g, 0, 0)),
    ]
    head_of_lane = (jnp.arange(SSM_GROUPS)[:, None, None] * SSM_HEADS_PER_GROUP
                    + jnp.arange(GROUP_WIDTH)[None, None, :] // SSM_HEAD_DIM)
    expand = (jnp.arange(LANES)[None, :, None] == head_of_lane).astype(BF16)
    cb2 = conv_b.reshape(1, CONV_DIM)
    return pl.pallas_call(
        _ssd_kernel,
        out_shape=jax.ShapeDtypeStruct((batch * seq, D_INNER), BF16),
        grid=(batch, SSM_GROUPS // SSD_GROUPS_PER_STEP, nc),
        in_specs=in_specs,
        out_specs=pl.BlockSpec((L, xw), lambda b, g, c: (rows(b, g, c), g)),
        scratch_shapes=[
            pltpu.VMEM((SSM_STATE, xw), F32),
            pltpu.VMEM((L + SUBLANES, xw), F32),
            pltpu.VMEM((L + SUBLANES, sw), F32),
            pltpu.VMEM((L + SUBLANES, sw), F32),
        ],
        compiler_params=_params("arbitrary", "arbitrary", "arbitrary"),
        name="ssd",
    )(zx, zx, zx, zx, zx, conv_w, conv_w, conv_w, cb2, cb2, cb2,
      dtb_e, alog_e, dskip_e, norm_w.reshape(1, D_INNER), expand)


ATT_HEADS_PER_STEP = 4


def _attn_kernel(q_ref, k_ref, v_ref, o_ref):
    tq = ATT_BLOCK
    dh = SB_HEAD_DIM
    i = pl.program_id(2)
    scale = dh ** -0.5
    row = lax.broadcasted_iota(jnp.int32, (tq, tq), 0)
    col = lax.broadcasted_iota(jnp.int32, (tq, tq), 1)
    strict = col < row
    row2 = lax.broadcasted_iota(jnp.int32, (2 * tq, tq), 0) & (tq - 1)
    col2 = lax.broadcasted_iota(jnp.int32, (2 * tq, tq), 1)
    neg_later = jnp.where(row2 > col2, -1.0, 0.0).astype(BF16)

    heads = range(ATT_HEADS_PER_STEP)

    def block(j, state, diagonal):
        start = pl.multiple_of(j * tq, tq)
        cols = [slice(h * dh, (h + 1) * dh) for h in heads]
        z = [_dot_nt(q_ref[:, cols[h]], k_ref[pl.ds(start, tq), cols[h]]) * scale for h in heads]
        drop, logit, parts = [], [], []
        for h in heads:
            neg_abs = lax.bitcast_convert_type(
                lax.bitcast_convert_type(z[h], jnp.uint32) | jnp.uint32(0x80000000), F32)
            sp = jnp.maximum(z[h], 0.0) + jnp.log(1.0 + jnp.exp(neg_abs))
            logit.append(z[h] - sp)
            if diagonal:
                sp = jnp.where(strict, sp, 0.0)
            hi, lo = _split2(sp)
            parts.append(jnp.concatenate([hi, lo], axis=1))
            drop.append(sp)
        suffix = [_dot(parts[h], neg_later) for h in heads]
        weights = []
        for h in heads:
            w = jnp.exp(logit[h] + suffix[h] + state[h][0])
            if diagonal:
                w = jnp.where(strict, w, 0.0)
            weights.append(w.astype(BF16))
        out = [_dot(weights[h], v_ref[pl.ds(start, tq), cols[h]]) for h in heads]
        return tuple((state[h][0] - jnp.sum(drop[h], axis=1, keepdims=True), state[h][1] + out[h])
                     for h in heads)

    zero = (jnp.zeros((tq, 1), F32), jnp.zeros((tq, dh), F32))
    state = block(i, (zero,) * ATT_HEADS_PER_STEP, True)
    state = lax.fori_loop(0, i, lambda t, s: block(i - 1 - t, s, False), state)
    for h in range(ATT_HEADS_PER_STEP):
        o_ref[:, h * dh:(h + 1) * dh] = state[h][1].astype(o_ref.dtype)


def _attention(q, kv, *, batch, seq):
    tq = ATT_BLOCK
    nq = seq // tq
    width = ATT_HEADS_PER_STEP * SB_HEAD_DIM
    n_steps = SB_HEADS // ATT_HEADS_PER_STEP
    return pl.pallas_call(
        _attn_kernel,
        out_shape=jax.ShapeDtypeStruct((batch * seq, D_MODEL), BF16),
        grid=(batch, n_steps, nq),
        in_specs=[
            pl.BlockSpec((tq, width), lambda b, h, i: (b * nq + i, h)),
            pl.BlockSpec((seq, width), lambda b, h, i: (b, h)),
            pl.BlockSpec((seq, width), lambda b, h, i: (b, n_steps + h)),
        ],
        out_specs=pl.BlockSpec((tq, width), lambda b, h, i: (b * nq + i, h)),
        compiler_params=_params("arbitrary", "arbitrary", "arbitrary"),
        name="sb_attention",
    )(q, kv, kv)


ROUTER_FINE0 = N_GROUPS


ROUTE_E1, ROUTE_E2, ROUTE_W1, ROUTE_W2, ROUTE_R1, ROUTE_R2 = range(6)


def _router_kernel(h_ref, g_ref, w_ref, b_ref, hn_ref, route_ref, counts_ref, carry_ref):
    @pl.when(pl.program_id(0) == 0)
    def _():
        carry_ref[...] = jnp.zeros_like(carry_ref)

    x = h_ref[...]
    hn = x * _rms_scale(x) * g_ref[...]
    hn_ref[...] = hn
    x1 = hn.astype(BF16)
    x2 = (hn - x1.astype(F32)).astype(BF16)
    w = w_ref[...]
    w1 = w.astype(BF16)
    w2 = (w - w1.astype(F32)).astype(BF16)
    logits = _dot(x1, w1) + _dot(x2, w1) + _dot(x1, w2) + b_ref[...]

    lane_i = lax.broadcasted_iota(jnp.int32, logits.shape, 1)
    lane = lane_i.astype(F32)
    neg = -jnp.inf

    def first_argmax(v, vmax):
        return jnp.min(jnp.where(v == vmax, lane, float(LANES)), axis=1, keepdims=True)

    coarse = jnp.where(lane_i < N_GROUPS, logits, neg)
    cmax = jnp.max(coarse, axis=1, keepdims=True)
    p_g = 1.0 / jnp.sum(jnp.exp(coarse - cmax), axis=1, keepdims=True)
    g_idx = first_argmax(coarse, cmax)
    fine_lo = ROUTER_FINE0 + g_idx * EXPERTS_PER_GROUP
    in_group = (lane >= fine_lo) & (lane < fine_lo + EXPERTS_PER_GROUP)
    fine = jnp.where(in_group, logits, neg)
    m1 = jnp.max(fine, axis=1, keepdims=True)
    denom = jnp.sum(jnp.exp(fine - m1), axis=1, keepdims=True)
    i1 = first_argmax(fine, m1)
    rest = jnp.where(lane == i1, neg, fine)
    m2 = jnp.max(rest, axis=1, keepdims=True)
    i2 = first_argmax(rest, m2)
    p1 = 1.0 / denom
    p2 = jnp.exp(m2 - m1) / denom
    norm = p1 + p2
    gate1 = p_g * (p1 / norm)
    gate2 = p_g * (p2 / norm)

    tm = x.shape[0]
    chosen = (lane == i1) | (lane == i2)
    r = lax.broadcasted_iota(jnp.int32, (tm, tm), 0)
    c = lax.broadcasted_iota(jnp.int32, (tm, tm), 1)
    before = _dot((c < r).astype(BF16), chosen.astype(BF16)) + carry_ref[...]
    rank1 = jnp.sum(jnp.where(lane == i1, before, 0.0), axis=1, keepdims=True)
    rank2 = jnp.sum(jnp.where(lane == i2, before, 0.0), axis=1, keepdims=True)
    carry_ref[...] += jnp.sum(chosen.astype(F32), axis=0, keepdims=True)
    counts_ref[...] = carry_ref[...]

    route = jnp.zeros_like(logits)
    for slot, val in ((ROUTE_E1, i1 - ROUTER_FINE0), (ROUTE_E2, i2 - ROUTER_FINE0),
                      (ROUTE_W1, gate1), (ROUTE_W2, gate2), (ROUTE_R1, rank1), (ROUTE_R2, rank2)):
        route = jnp.where(lane_i == slot, val, route)
    route_ref[...] = route


def _router(h, g, w_router, b_router, *, tm):
    m, k = h.shape
    return pl.pallas_call(
        _router_kernel,
        out_shape=(jax.ShapeDtypeStruct((m, k), F32), jax.ShapeDtypeStruct((m, LANES), F32),
                   jax.ShapeDtypeStruct((1, LANES), F32)),
        grid=(m // tm,),
        in_specs=[
            pl.BlockSpec((tm, k), lambda i: (i, 0)),
            pl.BlockSpec((1, k), lambda i: (0, 0)),
            pl.BlockSpec((k, LANES), lambda i: (0, 0)),
            pl.BlockSpec((1, LANES), lambda i: (0, 0)),
        ],
        out_specs=(pl.BlockSpec((tm, k), lambda i: (i, 0)),
                   pl.BlockSpec((tm, LANES), lambda i: (i, 0)),
                   pl.BlockSpec((1, LANES), lambda i: (0, 0))),
        scratch_shapes=[pltpu.VMEM((1, LANES), F32)],
        compiler_params=_params("arbitrary"),
        name="moe_router",
    )(h, g.reshape(1, k), w_router, b_router)


DISPATCH_TOKENS = 512
DMA_ISSUE_UNROLL = 8


def _row_copy(src, src_row, dst, dst_row, sem):
    return pltpu.make_async_copy(src.at[pl.ds(src_row, 1)], dst.at[pl.ds(dst_row, 1)], sem)


def _dispatch_kernel(pos1_ref, pos2_ref, hn_ref, xs_ref, sem):
    base = pl.program_id(0) * DISPATCH_TOKENS

    def issue(t, _):
        tok = base + t
        _row_copy(hn_ref, t, xs_ref, pos1_ref[tok], sem).start()
        _row_copy(hn_ref, t, xs_ref, pos2_ref[tok], sem).start()
        return 0

    lax.fori_loop(0, DISPATCH_TOKENS, issue, 0, unroll=DMA_ISSUE_UNROLL)

    for _ in range(2):
        pltpu.make_async_copy(hn_ref, xs_ref.at[pl.ds(0, DISPATCH_TOKENS)], sem).wait()


def _dispatch(hn, pos1, pos2):
    m, k = hn.shape
    return pl.pallas_call(
        _dispatch_kernel,
        out_shape=jax.ShapeDtypeStruct((2 * m, k), F32),
        grid_spec=pltpu.PrefetchScalarGridSpec(
            num_scalar_prefetch=2,
            grid=(m // DISPATCH_TOKENS,),
            in_specs=[pl.BlockSpec((DISPATCH_TOKENS, k), lambda i, *_: (i, 0))],
            out_specs=pl.BlockSpec(memory_space=pl.ANY),
            scratch_shapes=[pltpu.SemaphoreType.DMA],
        ),
        compiler_params=_params("arbitrary"),
        name="moe_dispatch",
    )(pos1, pos2, hn)


EXPERT_TILE = 256
EXPERT_HIDDEN_SLICES = 2


def _experts_kernel(tile_ref, expert_ref, lo_ref, hi_ref, first_ref, valid_ref,
                    x_ref, wg_ref, wu_ref, wd_ref, o_ref):
    del tile_ref, expert_ref
    n = pl.program_id(0)

    @pl.when(valid_ref[n] == 1)
    def _():
        x = x_ref[...].astype(BF16)
        rows = lax.broadcasted_iota(jnp.int32, (EXPERT_TILE, 1), 0)
        inside = (rows >= lo_ref[n]) & (rows < hi_ref[n])
        f = wg_ref.shape[-1]
        fs = f // EXPERT_HIDDEN_SLICES
        cols = [slice(s * fs, (s + 1) * fs) for s in range(EXPERT_HIDDEN_SLICES)]
        pre = [(_dot(x, wg_ref[0, 0, :, c].astype(BF16)), _dot(x, wu_ref[0, 0, :, c].astype(BF16)))
               for c in cols]
        hid = [jnp.where(inside, _silu(g) * u, 0.0).astype(BF16) for g, u in pre]
        y = sum(_dot(hid[s], wd_ref[0, 0, c, :].astype(BF16)) for s, c in enumerate(cols))

        @pl.when(first_ref[n] == 1)
        def _():
            o_ref[...] = y

        @pl.when(first_ref[n] == 0)
        def _():
            o_ref[...] += y


def _experts(work, xs, w_gate, w_up, w_down, layer):
    rows, k = xs.shape
    f = w_gate.shape[-1]
    n_items = work[0].shape[0]
    return pl.pallas_call(
        _experts_kernel,
        out_shape=jax.ShapeDtypeStruct((rows, k), F32),
        grid_spec=pltpu.PrefetchScalarGridSpec(
            num_scalar_prefetch=6,
            grid=(n_items,),
            in_specs=[
                pl.BlockSpec((EXPERT_TILE, k), lambda n, tile, *_: (tile[n], 0)),
                pl.BlockSpec((1, 1, k, f), lambda n, tile, expert, *_: (layer, expert[n], 0, 0)),
                pl.BlockSpec((1, 1, k, f), lambda n, tile, expert, *_: (layer, expert[n], 0, 0)),
                pl.BlockSpec((1, 1, f, k), lambda n, tile, expert, *_: (layer, expert[n], 0, 0)),
            ],
            out_specs=pl.BlockSpec((EXPERT_TILE, k), lambda n, tile, *_: (tile[n], 0)),
        ),
        compiler_params=_params("arbitrary"),
        name="moe_experts",
    )(*work, xs, w_gate, w_up, w_down)


COMBINE_TOKENS = 256


def _combine_kernel(pos1_ref, pos2_ref, h_ref, route_ref, ys_ref, *rest, final_norm):
    if final_norm:
        g_ref, o_ref, buf1, buf2, sem = rest
    else:
        o_ref, buf1, buf2, sem = rest
    base = pl.program_id(0) * COMBINE_TOKENS

    def issue(t, _):
        tok = base + t
        _row_copy(ys_ref, pos1_ref[tok], buf1, t, sem).start()
        _row_copy(ys_ref, pos2_ref[tok], buf2, t, sem).start()
        return 0

    lax.fori_loop(0, COMBINE_TOKENS, issue, 0, unroll=DMA_ISSUE_UNROLL)

    pltpu.make_async_copy(ys_ref.at[pl.ds(0, COMBINE_TOKENS)], buf1, sem).wait()
    pltpu.make_async_copy(ys_ref.at[pl.ds(0, COMBINE_TOKENS)], buf2, sem).wait()
    route = route_ref[...]
    w1 = route[:, ROUTE_W1:ROUTE_W1 + 1]
    w2 = route[:, ROUTE_W2:ROUTE_W2 + 1]
    out = h_ref[...] + w1 * buf1[...] + w2 * buf2[...]
    if final_norm:
        out = out * _rms_scale(out) * g_ref[...]
    o_ref[...] = out


def _combine(h, route, ys, pos1, pos2, final_gain=None):
    m, k = h.shape
    tm = COMBINE_TOKENS
    final_norm = final_gain is not None
    in_specs = [
        pl.BlockSpec((tm, k), lambda i, *_: (i, 0)),
        pl.BlockSpec((tm, LANES), lambda i, *_: (i, 0)),
        pl.BlockSpec(memory_space=pl.ANY),
    ]
    operands = [h, route, ys]
    if final_norm:
        in_specs.append(pl.BlockSpec((1, k), lambda i, *_: (0, 0)))
        operands.append(final_gain.reshape(1, k))
    return pl.pallas_call(
        functools.partial(_combine_kernel, final_norm=final_norm),
        out_shape=jax.ShapeDtypeStruct((m, k), F32),
        grid_spec=pltpu.PrefetchScalarGridSpec(
            num_scalar_prefetch=2,
            grid=(m // tm,),
            in_specs=in_specs,
            out_specs=pl.BlockSpec((tm, k), lambda i, *_: (i, 0)),
            scratch_shapes=[pltpu.VMEM((tm, k), F32), pltpu.VMEM((tm, k), F32),
                            pltpu.SemaphoreType.DMA],
        ),
        compiler_params=_params("arbitrary"),
        name="moe_combine",
    )(pos1, pos2, *operands)


def _work_list(counts, n_rows):
    tm = EXPERT_TILE
    n_tiles = n_rows // tm
    n_items = n_tiles + N_EXPERTS - 1
    i32 = jnp.int32
    count_le = lambda sorted_vals, q: jnp.sum((sorted_vals[None, :] <= q[:, None]).astype(i32), axis=1)
    pick = lambda table, idx: jnp.sum(
        jnp.where(idx[:, None] == jnp.arange(table.shape[0], dtype=i32)[None, :], table[None, :], 0), axis=1)
    e_ids = jnp.arange(N_EXPERTS, dtype=i32)
    ends = jnp.sum(jnp.where(e_ids[None, :] <= e_ids[:, None], counts[None, :], 0), axis=1)
    starts = ends - counts
    tile_ids = jnp.arange(n_tiles, dtype=i32)
    tile_lo = tile_ids * tm
    e_first = count_le(ends, tile_lo)
    e_last = count_le(ends, tile_lo + (tm - 1))
    per_tile = e_last - e_first + 1
    item0 = jnp.sum(jnp.where(tile_ids[None, :] < tile_ids[:, None], per_tile[None, :], 0), axis=1)
    total = jnp.sum(per_tile)
    n = jnp.arange(n_items, dtype=i32)
    valid = n < total
    tile = jnp.where(valid, count_le(item0, n) - 1, n_tiles - 1)
    expert = jnp.where(valid, pick(e_first, tile) + (n - pick(item0, tile)), e_last[-1])
    expert = jnp.clip(expert, 0, N_EXPERTS - 1)
    lo = jnp.clip(pick(starts, expert) - tile * tm, 0, tm)
    hi = jnp.clip(pick(ends, expert) - tile * tm, 0, tm)
    first = (n == pick(item0, tile)) & valid
    as_i32 = lambda a: a.astype(i32)
    return tuple(map(as_i32, (tile, expert, lo, hi, first, valid))), starts


def _router_weights(w_coarse, b_coarse, w_fine, b_fine):
    pad = LANES - N_GROUPS - N_EXPERTS
    w = jnp.concatenate([w_coarse, w_fine.reshape(D_MODEL, N_EXPERTS),
                         jnp.zeros((D_MODEL, pad), F32)], axis=1)
    b = jnp.concatenate([b_coarse, b_fine.reshape(N_EXPERTS), jnp.zeros((pad,), F32)])
    return w, b.reshape(1, LANES)


def _moe(h, norm_g, w_coarse, b_coarse, w_fine, b_fine, w_gate, w_up, w_down, layer, final_gain=None):
    w_r, b_r = _router_weights(w_coarse, b_coarse, w_fine, b_fine)
    hn, route, counts = _router(h, norm_g, w_r, b_r, tm=256)
    counts = counts[0, ROUTER_FINE0:ROUTER_FINE0 + N_EXPERTS].astype(jnp.int32)
    work, starts = _work_list(counts, 2 * h.shape[0])
    ids = route[:, :ROUTE_R2 + 1].astype(jnp.int32)
    e_ids = jnp.arange(N_EXPERTS, dtype=jnp.int32)[None, :]
    start_of = lambda e: jnp.sum(jnp.where(e[:, None] == e_ids, starts[None, :], 0), axis=1)
    pos1 = start_of(ids[:, ROUTE_E1]) + ids[:, ROUTE_R1]
    pos2 = start_of(ids[:, ROUTE_E2]) + ids[:, ROUTE_R2]
    xs = _dispatch(hn, pos1, pos2)
    ys = _experts(work, xs, w_gate, w_up, w_down, layer)
    return _combine(h, route, ys, pos1, pos2, final_gain)


def kernel(x, mix_norm, ffn_norm, ssm_w_in, ssm_conv_w, ssm_conv_b, ssm_dt_bias, ssm_a_log, ssm_d,
           ssm_norm_w, ssm_w_out, kv_norm, w_k, w_v, sb_w_q, sb_w_out, moe_w_coarse, moe_b_coarse,
           moe_w_fine, moe_b_fine, moe_w_gate, moe_w_up, moe_w_down, final_norm):
    batch, seq, d = x.shape
    h = x.reshape(batch * seq, d)

    zx = _norm_matmul(h, mix_norm[0], [ssm_w_in[0]], tm=1024, tn=768, out_dtype=F32)
    expand = lambda p: jnp.repeat(p, SSM_HEAD_DIM).reshape(1, D_INNER)
    yn = _ssd(zx, ssm_conv_w[0], ssm_conv_b[0], expand(ssm_dt_bias[0]), expand(ssm_a_log[0]),
              expand(ssm_d[0]), ssm_norm_w[0], batch=batch, seq=seq)
    h = _matmul_res(yn, ssm_w_out[0], h, tm=1024, tn=512)
    h = _moe(h, ffn_norm[0], moe_w_coarse[0], moe_b_coarse[0], moe_w_fine[0], moe_b_fine[0],
             moe_w_gate, moe_w_up, moe_w_down, 0)

    kv = _norm_matmul(h, kv_norm, [w_k, w_v], tm=1024, tn=512, out_dtype=BF16)

    q = _norm_matmul(h, mix_norm[1], [sb_w_q[0]], tm=1024, tn=1024, out_dtype=BF16)
    o = _attention(q, kv, batch=batch, seq=seq)
    h = _matmul_res(o, sb_w_out[0], h, tm=1024, tn=512)
    h = _moe(h, ffn_norm[1], moe_w_coarse[1], moe_b_coarse[1], moe_w_fine[1], moe_b_fine[1],
             moe_w_gate, moe_w_up, moe_w_down, 1, final_gain=final_norm)
    return h.reshape(batch, seq, d)
```

```python
import functools

import jax
import jax.numpy as jnp
from jax import lax
from jax.experimental import pallas as pl
from jax.experimental.pallas import tpu as pltpu

F32 = jnp.float32
BF16 = jnp.bfloat16
EPS = 1e-5

D_MODEL = 2048
D_INNER = 4096
SSM_HEAD_DIM = 64
SSM_HEADS = 64
SSM_GROUPS = 8
SSM_HEADS_PER_GROUP = 8
SSM_STATE = 128
GROUP_WIDTH = D_INNER // SSM_GROUPS
CONV_WIDTH = 4
CONV_DIM = D_INNER + 2 * SSM_GROUPS * SSM_STATE
IN_PROJ_DIM = D_INNER + CONV_DIM + SSM_HEADS
SB_HEADS = 16
SB_HEAD_DIM = 128
N_GROUPS = 4
EXPERTS_PER_GROUP = 4
N_EXPERTS = 16
D_EXPERT = 512

LANES = 128
SUBLANES = 8
DT_COL_BLOCK = (D_INNER + CONV_DIM) // LANES
SSD_CHUNK = 128
SSD_GROUPS_PER_STEP = 4
ATT_BLOCK = 256
VMEM_LIMIT = 56 * 1024 * 1024


def _dot(a, b):
    return jnp.dot(a, b, preferred_element_type=F32)


def _dot_nt(a, b):
    return lax.dot_general(a, b, (((1,), (1,)), ((), ())), preferred_element_type=F32)


def _split3(v):
    v1 = v.astype(BF16)
    r = v - v1.astype(F32)
    v2 = r.astype(BF16)
    r = r - v2.astype(F32)
    return v1, v2, r.astype(BF16)


def _split2(v):
    hi = v.astype(BF16)
    return hi, (v - hi.astype(F32)).astype(BF16)


def _dot3_left(m, v):
    v1, v2, v3 = _split3(v)
    return _dot(m, v1) + _dot(m, v2) + _dot(m, v3)


def _dot3_right(v, m):
    v1, v2, v3 = _split3(v)
    return _dot(v1, m) + _dot(v2, m) + _dot(v3, m)


def _softplus(x):
    return jnp.maximum(x, 0.0) + jnp.log(1.0 + jnp.exp(-jnp.abs(x)))


def _silu(x):
    half = 0.5 * x
    return half + half * jnp.tanh(half)


def _rms_scale(x):
    return lax.rsqrt(jnp.mean(x * x, axis=-1, keepdims=True) + EPS)


def _params(*sem):
    return pltpu.CompilerParams(dimension_semantics=sem, vmem_limit_bytes=VMEM_LIMIT)


def _weight_spec(w, k, tn, col_tile):
    if w.ndim == 2:
        return pl.BlockSpec((k, tn), lambda i, j: (0, col_tile(i, j)))
    return pl.BlockSpec((None, k, tn), lambda i, j: (0, 0, col_tile(i, j)))


def _norm_matmul_kernel(x_ref, g_ref, *rest, tiles):
    w_refs, (o_ref, xn_ref) = rest[:len(tiles)], rest[len(tiles):]
    j = pl.program_id(1)

    @pl.when(j == 0)
    def _():
        x = x_ref[...]
        xn_ref[...] = (x * _rms_scale(x) * g_ref[...]).astype(BF16)

    first = 0
    for w_ref, count in zip(w_refs, tiles):
        @pl.when((j >= first) & (j < first + count))
        def _(w_ref=w_ref):
            o_ref[...] = _dot(xn_ref[...], w_ref[...].astype(BF16)).astype(o_ref.dtype)
        first += count


def _norm_matmul(x, g, weights, *, tm, tn, out_dtype):
    m, k = x.shape
    assert all(w.shape[-1] % tn == 0 for w in weights)
    tiles = tuple(w.shape[-1] // tn for w in weights)
    n = sum(w.shape[-1] for w in weights)
    firsts = [sum(tiles[:a]) for a in range(len(tiles))]
    w_specs = [_weight_spec(w, k, tn, lambda i, j, lo=lo, count=count: jnp.clip(j - lo, 0, count - 1))
               for w, lo, count in zip(weights, firsts, tiles)]
    return pl.pallas_call(
        functools.partial(_norm_matmul_kernel, tiles=tiles),
        out_shape=jax.ShapeDtypeStruct((m, n), out_dtype),
        grid=(m // tm, sum(tiles)),
        in_specs=[
            pl.BlockSpec((tm, k), lambda i, j: (i, 0)),
            pl.BlockSpec((1, k), lambda i, j: (0, 0)),
            *w_specs,
        ],
        out_specs=pl.BlockSpec((tm, tn), lambda i, j: (i, j)),
        scratch_shapes=[pltpu.VMEM((tm, k), BF16)],
        compiler_params=_params("arbitrary", "arbitrary"),
        name="norm_matmul",
    )(x, g.reshape(1, k), *weights)


IN_PROJ_ROWS = 1024
IN_PROJ_COLS = 512


def _in_proj_kernel(x_ref, g_ref, w_ref, wdt_ref, o_ref, dt_ref, xn_ref):
    @pl.when(pl.program_id(1) == 0)
    def _():
        x = x_ref[...]
        xn = (x * _rms_scale(x) * g_ref[...]).astype(BF16)
        xn_ref[...] = xn
        dt_ref[...] = _dot_nt(xn, wdt_ref[...].astype(BF16))

    o_ref[...] = _dot_nt(xn_ref[...], w_ref[...].astype(BF16))


def _in_proj(x, g, w_in):
    m, k = x.shape
    w_in_t = jnp.swapaxes(w_in, 1, 2)
    tm, tn = IN_PROJ_ROWS, IN_PROJ_COLS
    n_main = D_INNER + CONV_DIM
    return pl.pallas_call(
        _in_proj_kernel,
        out_shape=(jax.ShapeDtypeStruct((m, n_main), F32), jax.ShapeDtypeStruct((m, LANES), F32)),
        grid=(m // tm, n_main // tn),
        in_specs=[
            pl.BlockSpec((tm, k), lambda i, j: (i, 0)),
            pl.BlockSpec((1, k), lambda i, j: (0, 0)),
            pl.BlockSpec((None, tn, k), lambda i, j: (0, j, 0)),
            pl.BlockSpec((None, LANES, k), lambda i, j: (0, DT_COL_BLOCK, 0)),
        ],
        out_specs=(pl.BlockSpec((tm, tn), lambda i, j: (i, j)),
                   pl.BlockSpec((tm, LANES), lambda i, j: (i, 0))),
        scratch_shapes=[pltpu.VMEM((tm, k), BF16)],
        compiler_params=_params("arbitrary", "arbitrary"),
        name="in_proj",
    )(x, g.reshape(1, k), w_in_t, w_in_t)


def _matmul_res_kernel(x_ref, w_ref, r_ref, o_ref):
    o_ref[...] = r_ref[...] + _dot(x_ref[...], w_ref[...].astype(BF16))


def _matmul_res(x, w, res, *, tm, tn):
    m, k = x.shape
    n = w.shape[-1]
    return pl.pallas_call(
        _matmul_res_kernel,
        out_shape=jax.ShapeDtypeStruct((m, n), F32),
        grid=(m // tm, n // tn),
        in_specs=[
            pl.BlockSpec((tm, k), lambda i, j: (i, 0)),
            _weight_spec(w, k, tn, lambda i, j: j),
            pl.BlockSpec((tm, tn), lambda i, j: (i, j)),
        ],
        out_specs=pl.BlockSpec((tm, tn), lambda i, j: (i, j)),
        compiler_params=_params("arbitrary", "arbitrary"),
        name="matmul_res",
    )(x, w, res)


def _ssd_kernel(z_ref, x_ref, b_ref, c_ref, dt_ref,
                wx_ref, wb_ref, wc_ref, bx_ref, bb_ref, bc_ref,
                dtb_ref, alog_ref, dskip_ref, nw_ref, expand_ref,
                o_ref,
                h_ref, xbuf, bbuf, cbuf):
    L = SSD_CHUNK
    gw = GROUP_WIDTH
    ns = SSM_STATE
    halo = SUBLANES
    groups = range(SSD_GROUPS_PER_STEP)

    @pl.when(pl.program_id(2) == 0)
    def _():
        h_ref[...] = jnp.zeros_like(h_ref)
        xbuf[0:halo, :] = jnp.zeros((halo, xbuf.shape[1]), F32)
        bbuf[0:halo, :] = jnp.zeros((halo, bbuf.shape[1]), F32)
        cbuf[0:halo, :] = jnp.zeros((halo, cbuf.shape[1]), F32)

    def conv_silu(raw_ref, buf, w_ref, bias_ref):
        raw = raw_ref[...]
        buf[halo:halo + L, :] = raw
        padded = buf[...]
        acc = bias_ref[...] + w_ref[CONV_WIDTH - 1:CONV_WIDTH, :] * raw
        for back in range(1, CONV_WIDTH):
            tap = pltpu.roll(padded, back, axis=0)[halo:halo + L, :]
            k = CONV_WIDTH - 1 - back
            acc = acc + w_ref[k:k + 1, :] * tap
        buf[0:halo, :] = raw[L - halo:L, :]
        return _silu(acc)

    bs = conv_silu(b_ref, bbuf, wb_ref, bb_ref)
    cs = conv_silu(c_ref, cbuf, wc_ref, bc_ref)
    bs_b = [bs[:, g * ns:(g + 1) * ns].astype(BF16) for g in groups]
    cs_b = [cs[:, g * ns:(g + 1) * ns].astype(BF16) for g in groups]
    cb = [_dot_nt(cs_b[g], bs_b[g]) for g in groups]
    h_prev = [h_ref[:, g * gw:(g + 1) * gw] for g in groups]
    y_off = [_dot(cs_b[g], h_prev[g].astype(BF16)) for g in groups]

    dt_lane = lax.broadcasted_iota(jnp.int32, (L, LANES), 1)
    dt_parts = _split3(jnp.where(dt_lane < SSM_HEADS, dt_ref[...], 0.0))
    dt_raw = [sum(_dot(p, expand_ref[g]) for p in dt_parts) for g in groups]

    xs = conv_silu(x_ref, xbuf, wx_ref, bx_ref)

    row = lax.broadcasted_iota(jnp.int32, (L, L), 0)
    col = lax.broadcasted_iota(jnp.int32, (L, L), 1)
    causal = row >= col
    tril = causal.astype(BF16)
    triu = (row <= col).astype(BF16)
    hrow = lax.broadcasted_iota(jnp.int32, (SUBLANES, gw), 0)
    hcol = lax.broadcasted_iota(jnp.int32, (SUBLANES, gw), 1)
    pick = (hcol == hrow * SSM_HEAD_DIM).astype(BF16)

    dt, a_parts = [], []
    for g in groups:
        lanes = slice(g * gw, (g + 1) * gw)
        dt.append(_softplus(dt_raw[g] + dtb_ref[:, lanes]))
        a_parts.append(_split2(dt[g] * (-jnp.exp(alog_ref[:, lanes]))))
    acs = [sum(_dot(tril, p) for p in a_parts[g]) for g in groups]
    a_rows = [[_dot_nt(pick, p).astype(BF16) for p in a_parts[g]] for g in groups]
    acs_rows = [sum(_dot(p, triu) for p in a_rows[g]) for g in groups]

    xdt = [xs[:, g * gw:(g + 1) * gw] * dt[g] for g in groups]
    xdt_b = [v.astype(BF16) for v in xdt]
    y_diag = []
    for g in groups:
        parts = []
        for r in range(SSM_HEADS_PER_GROUP):
            lo = r * SSM_HEAD_DIM
            seg = acs[g][:, lo:lo + 1] - acs_rows[g][r:r + 1, :]
            decay = jnp.exp(jnp.where(causal, seg, -jnp.inf))
            m = (cb[g] * decay).astype(BF16)
            parts.append(_dot(m, xdt_b[g][:, lo:lo + SSM_HEAD_DIM]))
        y_diag.append(jnp.concatenate(parts, axis=1))

    a_last = [acs[g][L - 1:L, :] for g in groups]
    xd = [(xdt[g] * jnp.exp(a_last[g] - acs[g])).astype(BF16) for g in groups]
    states = [_dot(bs[:, g * ns:(g + 1) * ns].T.astype(BF16), xd[g]) for g in groups]
    for g in groups:
        lanes = slice(g * gw, (g + 1) * gw)
        h_ref[:, lanes] = h_prev[g] * jnp.exp(a_last[g]) + states[g]
        y = y_diag[g] + y_off[g] * jnp.exp(acs[g]) + xs[:, lanes] * dskip_ref[:, lanes]
        yz = y * _silu(z_ref[:, lanes])
        o_ref[:, lanes] = (yz * _rms_scale(yz) * nw_ref[:, lanes]).astype(o_ref.dtype)


def _ssd(zx, dt, conv_w, conv_b, dtb_e, alog_e, dskip_e, norm_w, *, batch, seq):
    L = SSD_CHUNK
    nc = seq // L
    xw = SSD_GROUPS_PER_STEP * GROUP_WIDTH
    sw = SSD_GROUPS_PER_STEP * SSM_STATE
    x_blk = D_INNER // xw
    b_blk = (2 * D_INNER) // sw
    c_blk = b_blk + SSM_GROUPS * SSM_STATE // sw
    cw_b = D_INNER // sw
    cw_c = cw_b + SSM_GROUPS * SSM_STATE // sw

    def rows(b, g, c):
        return b * nc + c

    in_specs = [
        pl.BlockSpec((L, xw), lambda b, g, c: (rows(b, g, c), g)),
        pl.BlockSpec((L, xw), lambda b, g, c: (rows(b, g, c), x_blk + g)),
        pl.BlockSpec((L, sw), lambda b, g, c: (rows(b, g, c), b_blk + g)),
        pl.BlockSpec((L, sw), lambda b, g, c: (rows(b, g, c), c_blk + g)),
        pl.BlockSpec((L, LANES), lambda b, g, c: (rows(b, g, c), 0)),
        pl.BlockSpec((CONV_WIDTH, xw), lambda b, g, c: (0, g)),
        pl.BlockSpec((CONV_WIDTH, sw), lambda b, g, c: (0, cw_b + g)),
        pl.BlockSpec((CONV_WIDTH, sw), lambda b, g, c: (0, cw_c + g)),
        pl.BlockSpec((1, xw), lambda b, g, c: (0, g)),
        pl.BlockSpec((1, sw), lambda b, g, c: (0, cw_b + g)),
        pl.BlockSpec((1, sw), lambda b, g, c: (0, cw_c + g)),
        pl.BlockSpec((1, xw), lambda b, g, c: (0, g)),
        pl.BlockSpec((1, xw), lambda b, g, c: (0, g)),
        pl.BlockSpec((1, xw), lambda b, g, c: (0, g)),
        pl.BlockSpec((1, xw), lambda b, g, c: (0, g)),
        pl.BlockSpec((SSD_GROUPS_PER_STEP, LANES, GROUP_WIDTH), lambda b, g, c: (g, 0, 0)),
    ]
    head_of_lane = (jnp.arange(SSM_GROUPS)[:, None, None] * SSM_HEADS_PER_GROUP
                    + jnp.arange(GROUP_WIDTH)[None, None, :] // SSM_HEAD_DIM)
    expand = (jnp.arange(LANES)[None, :, None] == head_of_lane).astype(BF16)
    cb2 = conv_b.reshape(1, CONV_DIM)
    return pl.pallas_call(
        _ssd_kernel,
        out_shape=jax.ShapeDtypeStruct((batch * seq, D_INNER), BF16),
        grid=(batch, SSM_GROUPS // SSD_GROUPS_PER_STEP, nc),
        in_specs=in_specs,
        out_specs=pl.BlockSpec((L, xw), lambda b, g, c: (rows(b, g, c), g)),
        scratch_shapes=[
            pltpu.VMEM((SSM_STATE, xw), F32),
            pltpu.VMEM((L + SUBLANES, xw), F32),
            pltpu.VMEM((L + SUBLANES, sw), F32),
            pltpu.VMEM((L + SUBLANES, sw), F32),
        ],
        compiler_params=_params("arbitrary", "arbitrary", "arbitrary"),
        name="ssd",
    )(zx, zx, zx, zx, dt, conv_w, conv_w, conv_w, cb2, cb2, cb2,
      dtb_e, alog_e, dskip_e, norm_w.reshape(1, D_INNER), expand)


ATT_HEADS_PER_STEP = 8


def _attn_kernel(q_ref, k_ref, v_ref, o_ref):
    tq = ATT_BLOCK
    dh = SB_HEAD_DIM
    i = pl.program_id(2)
    scale = dh ** -0.5
    row = lax.broadcasted_iota(jnp.int32, (tq, tq), 0)
    col = lax.broadcasted_iota(jnp.int32, (tq, tq), 1)
    strict = col < row
    row2 = lax.broadcasted_iota(jnp.int32, (2 * tq, tq), 0) & (tq - 1)
    col2 = lax.broadcasted_iota(jnp.int32, (2 * tq, tq), 1)
    neg_later = jnp.where(row2 > col2, -1.0, 0.0).astype(BF16)

    heads = range(ATT_HEADS_PER_STEP)

    def block(j, state, diagonal):
        start = pl.multiple_of(j * tq, tq)
        cols = [slice(h * dh, (h + 1) * dh) for h in heads]
        z = [_dot_nt(q_ref[:, cols[h]], k_ref[pl.ds(start, tq), cols[h]]) * scale for h in heads]
        drop, logit, parts = [], [], []
        for h in heads:
            neg_abs = lax.bitcast_convert_type(
                lax.bitcast_convert_type(z[h], jnp.uint32) | jnp.uint32(0x80000000), F32)
            sp = jnp.maximum(z[h], 0.0) + jnp.log(1.0 + jnp.exp(neg_abs))
            logit.append(z[h] - sp)
            if diagonal:
                sp = jnp.where(strict, sp, 0.0)
            hi, lo = _split2(sp)
            parts.append(jnp.concatenate([hi, lo], axis=1))
            drop.append(sp)
        suffix = [_dot(parts[h], neg_later) for h in heads]
        weights = []
        for h in heads:
            w = jnp.exp(logit[h] + suffix[h] + state[h][0])
            if diagonal:
                w = jnp.where(strict, w, 0.0)
            weights.append(w.astype(BF16))
        out = [_dot(weights[h], v_ref[pl.ds(start, tq), cols[h]]) for h in heads]
        return tuple((state[h][0] - jnp.sum(drop[h], axis=1, keepdims=True), state[h][1] + out[h])
                     for h in heads)

    zero = (jnp.zeros((tq, 1), F32), jnp.zeros((tq, dh), F32))
    state = block(i, (zero,) * ATT_HEADS_PER_STEP, True)
    state = lax.fori_loop(0, i, lambda t, s: block(i - 1 - t, s, False), state)
    for h in range(ATT_HEADS_PER_STEP):
        o_ref[:, h * dh:(h + 1) * dh] = state[h][1].astype(o_ref.dtype)


def _attention(q, kv, *, batch, seq):
    tq = ATT_BLOCK
    nq = seq // tq
    width = ATT_HEADS_PER_STEP * SB_HEAD_DIM
    n_steps = SB_HEADS // ATT_HEADS_PER_STEP
    return pl.pallas_call(
        _attn_kernel,
        out_shape=jax.ShapeDtypeStruct((batch * seq, D_MODEL), BF16),
        grid=(batch, n_steps, nq),
        in_specs=[
            pl.BlockSpec((tq, width), lambda b, h, i: (b * nq + i, h)),
            pl.BlockSpec((seq, width), lambda b, h, i: (b, h)),
            pl.BlockSpec((seq, width), lambda b, h, i: (b, n_steps + h)),
        ],
        out_specs=pl.BlockSpec((tq, width), lambda b, h, i: (b * nq + i, h)),
        compiler_params=_params("arbitrary", "arbitrary", "arbitrary"),
        name="sb_attention",
    )(q, kv, kv)


ROUTER_FINE0 = N_GROUPS


ROUTE_E1, ROUTE_E2, ROUTE_W1, ROUTE_W2, ROUTE_R1, ROUTE_R2 = range(6)


def _router_kernel(h_ref, g_ref, w_ref, b_ref, hn_ref, route_ref, counts_ref, carry_ref):
    @pl.when(pl.program_id(0) == 0)
    def _():
        carry_ref[...] = jnp.zeros_like(carry_ref)

    x = h_ref[...]
    hn = x * _rms_scale(x) * g_ref[...]
    hn_ref[...] = hn
    x1 = hn.astype(BF16)
    x2 = (hn - x1.astype(F32)).astype(BF16)
    w = w_ref[...]
    w1 = w.astype(BF16)
    w2 = (w - w1.astype(F32)).astype(BF16)
    logits = _dot(x1, w1) + _dot(x2, w1) + _dot(x1, w2) + b_ref[...]

    lane_i = lax.broadcasted_iota(jnp.int32, logits.shape, 1)
    lane = lane_i.astype(F32)
    neg = -jnp.inf

    def first_argmax(v, vmax):
        return jnp.min(jnp.where(v == vmax, lane, float(LANES)), axis=1, keepdims=True)

    coarse = jnp.where(lane_i < N_GROUPS, logits, neg)
    cmax = jnp.max(coarse, axis=1, keepdims=True)
    p_g = 1.0 / jnp.sum(jnp.exp(coarse - cmax), axis=1, keepdims=True)
    g_idx = first_argmax(coarse, cmax)
    fine_lo = ROUTER_FINE0 + g_idx * EXPERTS_PER_GROUP
    in_group = (lane >= fine_lo) & (lane < fine_lo + EXPERTS_PER_GROUP)
    fine = jnp.where(in_group, logits, neg)
    m1 = jnp.max(fine, axis=1, keepdims=True)
    denom = jnp.sum(jnp.exp(fine - m1), axis=1, keepdims=True)
    i1 = first_argmax(fine, m1)
    rest = jnp.where(lane == i1, neg, fine)
    m2 = jnp.max(rest, axis=1, keepdims=True)
    i2 = first_argmax(rest, m2)
    p1 = 1.0 / denom
    p2 = jnp.exp(m2 - m1) / denom
    norm = p1 + p2
    gate1 = p_g * (p1 / norm)
    gate2 = p_g * (p2 / norm)

    tm = x.shape[0]
    chosen = (lane == i1) | (lane == i2)
    r = lax.broadcasted_iota(jnp.int32, (tm, tm), 0)
    c = lax.broadcasted_iota(jnp.int32, (tm, tm), 1)
    before = _dot((c < r).astype(BF16), chosen.astype(BF16)) + carry_ref[...]
    rank1 = jnp.sum(jnp.where(lane == i1, before, 0.0), axis=1, keepdims=True)
    rank2 = jnp.sum(jnp.where(lane == i2, before, 0.0), axis=1, keepdims=True)
    carry_ref[...] += jnp.sum(chosen.astype(F32), axis=0, keepdims=True)
    counts_ref[...] = carry_ref[...]

    route = jnp.zeros_like(logits)
    for slot, val in ((ROUTE_E1, i1 - ROUTER_FINE0), (ROUTE_E2, i2 - ROUTER_FINE0),
                      (ROUTE_W1, gate1), (ROUTE_W2, gate2), (ROUTE_R1, rank1), (ROUTE_R2, rank2)):
        route = jnp.where(lane_i == slot, val, route)
    route_ref[...] = route


def _router(h, g, w_router, b_router, *, tm):
    m, k = h.shape
    return pl.pallas_call(
        _router_kernel,
        out_shape=(jax.ShapeDtypeStruct((m, k), F32), jax.ShapeDtypeStruct((m, LANES), F32),
                   jax.ShapeDtypeStruct((1, LANES), F32)),
        grid=(m // tm,),
        in_specs=[
            pl.BlockSpec((tm, k), lambda i: (i, 0)),
            pl.BlockSpec((1, k), lambda i: (0, 0)),
            pl.BlockSpec((k, LANES), lambda i: (0, 0)),
            pl.BlockSpec((1, LANES), lambda i: (0, 0)),
        ],
        out_specs=(pl.BlockSpec((tm, k), lambda i: (i, 0)),
                   pl.BlockSpec((tm, LANES), lambda i: (i, 0)),
                   pl.BlockSpec((1, LANES), lambda i: (0, 0))),
        scratch_shapes=[pltpu.VMEM((1, LANES), F32)],
        compiler_params=_params("arbitrary"),
        name="moe_router",
    )(h, g.reshape(1, k), w_router, b_router)


DISPATCH_TOKENS = 512
DMA_ISSUE_UNROLL = 8


def _row_copy(src, src_row, dst, dst_row, sem):
    return pltpu.make_async_copy(src.at[pl.ds(src_row, 1)], dst.at[pl.ds(dst_row, 1)], sem)


def _dispatch_kernel(pos1_ref, pos2_ref, hn_ref, xs_ref, sem):
    base = pl.program_id(0) * DISPATCH_TOKENS

    def issue(t, _):
        tok = base + t
        _row_copy(hn_ref, t, xs_ref, pos1_ref[tok], sem).start()
        _row_copy(hn_ref, t, xs_ref, pos2_ref[tok], sem).start()
        return 0

    lax.fori_loop(0, DISPATCH_TOKENS, issue, 0, unroll=DMA_ISSUE_UNROLL)

    for _ in range(2):
        pltpu.make_async_copy(hn_ref, xs_ref.at[pl.ds(0, DISPATCH_TOKENS)], sem).wait()


def _dispatch(hn, pos1, pos2):
    m, k = hn.shape
    return pl.pallas_call(
        _dispatch_kernel,
        out_shape=jax.ShapeDtypeStruct((2 * m, k), F32),
        grid_spec=pltpu.PrefetchScalarGridSpec(
            num_scalar_prefetch=2,
            grid=(m // DISPATCH_TOKENS,),
            in_specs=[pl.BlockSpec((DISPATCH_TOKENS, k), lambda i, *_: (i, 0))],
            out_specs=pl.BlockSpec(memory_space=pl.ANY),
            scratch_shapes=[pltpu.SemaphoreType.DMA],
        ),
        compiler_params=_params("arbitrary"),
        name="moe_dispatch",
    )(pos1, pos2, hn)


EXPERT_TILE = 256
EXPERT_HIDDEN_SLICES = 2


def _experts_kernel(tile_ref, expert_ref, lo_ref, hi_ref, first_ref, valid_ref,
                    x_ref, wg_ref, wu_ref, wd_ref, o_ref):
    del tile_ref, expert_ref
    n = pl.program_id(0)

    @pl.when(valid_ref[n] == 1)
    def _():
        x = x_ref[...].astype(BF16)
        rows = lax.broadcasted_iota(jnp.int32, (EXPERT_TILE, 1), 0)
        inside = (rows >= lo_ref[n]) & (rows < hi_ref[n])
        f = wg_ref.shape[-1]
        fs = f // EXPERT_HIDDEN_SLICES
        cols = [slice(s * fs, (s + 1) * fs) for s in range(EXPERT_HIDDEN_SLICES)]
        pre = [(_dot(x, wg_ref[0, 0, :, c].astype(BF16)), _dot(x, wu_ref[0, 0, :, c].astype(BF16)))
               for c in cols]
        hid = [jnp.where(inside, _silu(g) * u, 0.0).astype(BF16) for g, u in pre]
        y = sum(_dot(hid[s], wd_ref[0, 0, c, :].astype(BF16)) for s, c in enumerate(cols))

        @pl.when(first_ref[n] == 1)
        def _():
            o_ref[...] = y

        @pl.when(first_ref[n] == 0)
        def _():
            o_ref[...] += y


def _experts(work, xs, w_gate, w_up, w_down, layer):
    rows, k = xs.shape
    f = w_gate.shape[-1]
    n_items = work[0].shape[0]
    return pl.pallas_call(
        _experts_kernel,
        out_shape=jax.ShapeDtypeStruct((rows, k), F32),
        grid_spec=pltpu.PrefetchScalarGridSpec(
            num_scalar_prefetch=6,
            grid=(n_items,),
            in_specs=[
                pl.BlockSpec((EXPERT_TILE, k), lambda n, tile, *_: (tile[n], 0)),
                pl.BlockSpec((1, 1, k, f), lambda n, tile, expert, *_: (layer, expert[n], 0, 0)),
                pl.BlockSpec((1, 1, k, f), lambda n, tile, expert, *_: (layer, expert[n], 0, 0)),
                pl.BlockSpec((1, 1, f, k), lambda n, tile, expert, *_: (layer, expert[n], 0, 0)),
            ],
            out_specs=pl.BlockSpec((EXPERT_TILE, k), lambda n, tile, *_: (tile[n], 0)),
        ),
        compiler_params=_params("arbitrary"),
        name="moe_experts",
    )(*work, xs, w_gate, w_up, w_down)


COMBINE_TOKENS = 256


def _combine_kernel(pos1_ref, pos2_ref, h_ref, route_ref, ys_ref, *rest, final_norm):
    if final_norm:
        g_ref, o_ref, buf1, buf2, sem = rest
    else:
        o_ref, buf1, buf2, sem = rest
    base = pl.program_id(0) * COMBINE_TOKENS

    def issue(t, _):
        tok = base + t
        _row_copy(ys_ref, pos1_ref[tok], buf1, t, sem).start()
        _row_copy(ys_ref, pos2_ref[tok], buf2, t, sem).start()
        return 0

    lax.fori_loop(0, COMBINE_TOKENS, issue, 0, unroll=DMA_ISSUE_UNROLL)

    pltpu.make_async_copy(ys_ref.at[pl.ds(0, COMBINE_TOKENS)], buf1, sem).wait()
    pltpu.make_async_copy(ys_ref.at[pl.ds(0, COMBINE_TOKENS)], buf2, sem).wait()
    route = route_ref[...]
    w1 = route[:, ROUTE_W1:ROUTE_W1 + 1]
    w2 = route[:, ROUTE_W2:ROUTE_W2 + 1]
    out = h_ref[...] + w1 * buf1[...] + w2 * buf2[...]
    if final_norm:
        out = out * _rms_scale(out) * g_ref[...]
    o_ref[...] = out


def _combine(h, route, ys, pos1, pos2, final_gain=None):
    m, k = h.shape
    tm = COMBINE_TOKENS
    final_norm = final_gain is not None
    in_specs = [
        pl.BlockSpec((tm, k), lambda i, *_: (i, 0)),
        pl.BlockSpec((tm, LANES), lambda i, *_: (i, 0)),
        pl.BlockSpec(memory_space=pl.ANY),
    ]
    operands = [h, route, ys]
    if final_norm:
        in_specs.append(pl.BlockSpec((1, k), lambda i, *_: (0, 0)))
        operands.append(final_gain.reshape(1, k))
    return pl.pallas_call(
        functools.partial(_combine_kernel, final_norm=final_norm),
        out_shape=jax.ShapeDtypeStruct((m, k), F32),
        grid_spec=pltpu.PrefetchScalarGridSpec(
            num_scalar_prefetch=2,
            grid=(m // tm,),
            in_specs=in_specs,
            out_specs=pl.BlockSpec((tm, k), lambda i, *_: (i, 0)),
            scratch_shapes=[pltpu.VMEM((tm, k), F32), pltpu.VMEM((tm, k), F32),
                            pltpu.SemaphoreType.DMA],
        ),
        compiler_params=_params("arbitrary"),
        name="moe_combine",
    )(pos1, pos2, *operands)


def _work_list(counts, n_rows):
    tm = EXPERT_TILE
    n_tiles = n_rows // tm
    n_items = n_tiles + N_EXPERTS - 1
    i32 = jnp.int32
    count_le = lambda sorted_vals, q: jnp.sum((sorted_vals[None, :] <= q[:, None]).astype(i32), axis=1)
    pick = lambda table, idx: jnp.sum(
        jnp.where(idx[:, None] == jnp.arange(table.shape[0], dtype=i32)[None, :], table[None, :], 0), axis=1)
    e_ids = jnp.arange(N_EXPERTS, dtype=i32)
    ends = jnp.sum(jnp.where(e_ids[None, :] <= e_ids[:, None], counts[None, :], 0), axis=1)
    starts = ends - counts
    tile_ids = jnp.arange(n_tiles, dtype=i32)
    tile_lo = tile_ids * tm
    e_first = count_le(ends, tile_lo)
    e_last = count_le(ends, tile_lo + (tm - 1))
    per_tile = e_last - e_first + 1
    item0 = jnp.sum(jnp.where(tile_ids[None, :] < tile_ids[:, None], per_tile[None, :], 0), axis=1)
    total = jnp.sum(per_tile)
    n = jnp.arange(n_items, dtype=i32)
    valid = n < total
    tile = jnp.where(valid, count_le(item0, n) - 1, n_tiles - 1)
    expert = jnp.where(valid, pick(e_first, tile) + (n - pick(item0, tile)), e_last[-1])
    expert = jnp.clip(expert, 0, N_EXPERTS - 1)
    lo = jnp.clip(pick(starts, expert) - tile * tm, 0, tm)
    hi = jnp.clip(pick(ends, expert) - tile * tm, 0, tm)
    first = (n == pick(item0, tile)) & valid
    as_i32 = lambda a: a.astype(i32)
    return tuple(map(as_i32, (tile, expert, lo, hi, first, valid))), starts


def _router_weights(w_coarse, b_coarse, w_fine, b_fine):
    pad = LANES - N_GROUPS - N_EXPERTS
    w = jnp.concatenate([w_coarse, w_fine.reshape(D_MODEL, N_EXPERTS),
                         jnp.zeros((D_MODEL, pad), F32)], axis=1)
    b = jnp.concatenate([b_coarse, b_fine.reshape(N_EXPERTS), jnp.zeros((pad,), F32)])
    return w, b.reshape(1, LANES)


def _moe(h, norm_g, w_coarse, b_coarse, w_fine, b_fine, w_gate, w_up, w_down, layer, final_gain=None):
    w_r, b_r = _router_weights(w_coarse, b_coarse, w_fine, b_fine)
    hn, route, counts = _router(h, norm_g, w_r, b_r, tm=256)
    counts = counts[0, ROUTER_FINE0:ROUTER_FINE0 + N_EXPERTS].astype(jnp.int32)
    work, starts = _work_list(counts, 2 * h.shape[0])
    ids = route[:, :ROUTE_R2 + 1].astype(jnp.int32)
    e_ids = jnp.arange(N_EXPERTS, dtype=jnp.int32)[None, :]
    start_of = lambda e: jnp.sum(jnp.where(e[:, None] == e_ids, starts[None, :], 0), axis=1)
    pos1 = start_of(ids[:, ROUTE_E1]) + ids[:, ROUTE_R1]
    pos2 = start_of(ids[:, ROUTE_E2]) + ids[:, ROUTE_R2]
    xs = _dispatch(hn, pos1, pos2)
    ys = _experts(work, xs, w_gate, w_up, w_down, layer)
    return _combine(h, route, ys, pos1, pos2, final_gain)


def kernel(x, mix_norm, ffn_norm, ssm_w_in, ssm_conv_w, ssm_conv_b, ssm_dt_bias, ssm_a_log, ssm_d,
           ssm_norm_w, ssm_w_out, kv_norm, w_k, w_v, sb_w_q, sb_w_out, moe_w_coarse, moe_b_coarse,
           moe_w_fine, moe_b_fine, moe_w_gate, moe_w_up, moe_w_down, final_norm):
    batch, seq, d = x.shape
    h = x.reshape(batch * seq, d)

    zx, dt = _in_proj(h, mix_norm[0], ssm_w_in)
    expand = lambda p: jnp.repeat(p, SSM_HEAD_DIM).reshape(1, D_INNER)
    yn = _ssd(zx, dt, ssm_conv_w[0], ssm_conv_b[0], expand(ssm_dt_bias[0]), expand(ssm_a_log[0]),
              expand(ssm_d[0]), ssm_norm_w[0], batch=batch, seq=seq)
    h = _matmul_res(yn, ssm_w_out, h, tm=1024, tn=512)
    h = _moe(h, ffn_norm[0], moe_w_coarse[0], moe_b_coarse[0], moe_w_fine[0], moe_b_fine[0],
             moe_w_gate, moe_w_up, moe_w_down, 0)

    kv = _norm_matmul(h, kv_norm, [w_k, w_v], tm=1024, tn=512, out_dtype=BF16)

    q = _norm_matmul(h, mix_norm[1], [sb_w_q], tm=1024, tn=1024, out_dtype=BF16)
    o = _attention(q, kv, batch=batch, seq=seq)
    h = _matmul_res(o, sb_w_out, h, tm=1024, tn=512)
    h = _moe(h, ffn_norm[1], moe_w_coarse[1], moe_b_coarse[1], moe_w_fine[1], moe_b_fine[1],
             moe_w_gate, moe_w_up, moe_w_down, 1, final_gain=final_norm)
    return h.reshape(batch, seq, d)
```

```python
import functools

import jax
import jax.numpy as jnp
from jax import lax
from jax.experimental import pallas as pl
from jax.experimental.pallas import tpu as pltpu

F32 = jnp.float32
BF16 = jnp.bfloat16
EPS = 1e-5

D_MODEL = 2048
D_INNER = 4096
SSM_HEAD_DIM = 64
SSM_HEADS = 64
SSM_GROUPS = 8
SSM_HEADS_PER_GROUP = 8
SSM_STATE = 128
GROUP_WIDTH = D_INNER // SSM_GROUPS
CONV_WIDTH = 4
CONV_DIM = D_INNER + 2 * SSM_GROUPS * SSM_STATE
IN_PROJ_DIM = D_INNER + CONV_DIM + SSM_HEADS
SB_HEADS = 16
SB_HEAD_DIM = 128
N_GROUPS = 4
EXPERTS_PER_GROUP = 4
N_EXPERTS = 16
D_EXPERT = 512

LANES = 128
SUBLANES = 8
DT_COL_BLOCK = (D_INNER + CONV_DIM) // LANES
SSD_CHUNK = 128
SSD_GROUPS_PER_STEP = 8
ATT_BLOCK = 256
VMEM_LIMIT = 56 * 1024 * 1024


def _dot(a, b):
    return jnp.dot(a, b, preferred_element_type=F32)


def _dot_nt(a, b):
    return lax.dot_general(a, b, (((1,), (1,)), ((), ())), preferred_element_type=F32)


def _split3(v):
    v1 = v.astype(BF16)
    r = v - v1.astype(F32)
    v2 = r.astype(BF16)
    r = r - v2.astype(F32)
    return v1, v2, r.astype(BF16)


def _split2(v):
    hi = v.astype(BF16)
    return hi, (v - hi.astype(F32)).astype(BF16)


def _dot3_left(m, v):
    v1, v2, v3 = _split3(v)
    return _dot(m, v1) + _dot(m, v2) + _dot(m, v3)


def _dot3_right(v, m):
    v1, v2, v3 = _split3(v)
    return _dot(v1, m) + _dot(v2, m) + _dot(v3, m)


def _softplus(x):
    return jnp.maximum(x, 0.0) + jnp.log(1.0 + jnp.exp(-jnp.abs(x)))


def _silu(x):
    half = 0.5 * x
    return half + half * jnp.tanh(half)


def _rms_scale(x):
    return lax.rsqrt(jnp.mean(x * x, axis=-1, keepdims=True) + EPS)


def _params(*sem):
    return pltpu.CompilerParams(dimension_semantics=sem, vmem_limit_bytes=VMEM_LIMIT)


def _weight_spec(w, k, tn, col_tile):
    if w.ndim == 2:
        return pl.BlockSpec((k, tn), lambda i, j: (0, col_tile(i, j)))
    return pl.BlockSpec((None, k, tn), lambda i, j: (0, 0, col_tile(i, j)))


def _norm_matmul_kernel(x_ref, g_ref, *rest, tiles):
    w_refs, (o_ref, xn_ref) = rest[:len(tiles)], rest[len(tiles):]
    j = pl.program_id(1)

    @pl.when(j == 0)
    def _():
        x = x_ref[...]
        xn_ref[...] = (x * _rms_scale(x) * g_ref[...]).astype(BF16)

    first = 0
    for w_ref, count in zip(w_refs, tiles):
        @pl.when((j >= first) & (j < first + count))
        def _(w_ref=w_ref):
            o_ref[...] = _dot(xn_ref[...], w_ref[...].astype(BF16)).astype(o_ref.dtype)
        first += count


def _norm_matmul(x, g, weights, *, tm, tn, out_dtype):
    m, k = x.shape
    assert all(w.shape[-1] % tn == 0 for w in weights)
    tiles = tuple(w.shape[-1] // tn for w in weights)
    n = sum(w.shape[-1] for w in weights)
    firsts = [sum(tiles[:a]) for a in range(len(tiles))]
    w_specs = [_weight_spec(w, k, tn, lambda i, j, lo=lo, count=count: jnp.clip(j - lo, 0, count - 1))
               for w, lo, count in zip(weights, firsts, tiles)]
    return pl.pallas_call(
        functools.partial(_norm_matmul_kernel, tiles=tiles),
        out_shape=jax.ShapeDtypeStruct((m, n), out_dtype),
        grid=(m // tm, sum(tiles)),
        in_specs=[
            pl.BlockSpec((tm, k), lambda i, j: (i, 0)),
            pl.BlockSpec((1, k), lambda i, j: (0, 0)),
            *w_specs,
        ],
        out_specs=pl.BlockSpec((tm, tn), lambda i, j: (i, j)),
        scratch_shapes=[pltpu.VMEM((tm, k), BF16)],
        compiler_params=_params("arbitrary", "arbitrary"),
        name="norm_matmul",
    )(x, g.reshape(1, k), *weights)


IN_PROJ_ROWS = 1024
IN_PROJ_COLS = 512


def _in_proj_kernel(x_ref, g_ref, w_ref, wdt_ref, o_ref, dt_ref, xn_ref):
    @pl.when(pl.program_id(1) == 0)
    def _():
        x = x_ref[...]
        xn = (x * _rms_scale(x) * g_ref[...]).astype(BF16)
        xn_ref[...] = xn
        head = lax.broadcasted_iota(jnp.int32, dt_ref.shape, 1)
        dt_ref[...] = jnp.where(head < SSM_HEADS, _dot_nt(xn, wdt_ref[...].astype(BF16)), 0.0)

    o_ref[...] = _dot_nt(xn_ref[...], w_ref[...].astype(BF16))


def _in_proj(x, g, w_in):
    m, k = x.shape
    w_in_t = jnp.swapaxes(w_in, 1, 2)
    tm, tn = IN_PROJ_ROWS, IN_PROJ_COLS
    n_main = D_INNER + CONV_DIM
    return pl.pallas_call(
        _in_proj_kernel,
        out_shape=(jax.ShapeDtypeStruct((m, n_main), F32), jax.ShapeDtypeStruct((m, LANES), F32)),
        grid=(m // tm, n_main // tn),
        in_specs=[
            pl.BlockSpec((tm, k), lambda i, j: (i, 0)),
            pl.BlockSpec((1, k), lambda i, j: (0, 0)),
            pl.BlockSpec((None, tn, k), lambda i, j: (0, j, 0)),
            pl.BlockSpec((None, LANES, k), lambda i, j: (0, DT_COL_BLOCK, 0)),
        ],
        out_specs=(pl.BlockSpec((tm, tn), lambda i, j: (i, j)),
                   pl.BlockSpec((tm, LANES), lambda i, j: (i, 0))),
        scratch_shapes=[pltpu.VMEM((tm, k), BF16)],
        compiler_params=_params("arbitrary", "arbitrary"),
        name="in_proj",
    )(x, g.reshape(1, k), w_in_t, w_in_t)


def _matmul_res_kernel(x_ref, w_ref, r_ref, o_ref):
    o_ref[...] = r_ref[...] + _dot(x_ref[...], w_ref[...].astype(BF16))


def _matmul_res(x, w, res, *, tm, tn):
    m, k = x.shape
    n = w.shape[-1]
    return pl.pallas_call(
        _matmul_res_kernel,
        out_shape=jax.ShapeDtypeStruct((m, n), F32),
        grid=(m // tm, n // tn),
        in_specs=[
            pl.BlockSpec((tm, k), lambda i, j: (i, 0)),
            _weight_spec(w, k, tn, lambda i, j: j),
            pl.BlockSpec((tm, tn), lambda i, j: (i, j)),
        ],
        out_specs=pl.BlockSpec((tm, tn), lambda i, j: (i, j)),
        compiler_params=_params("arbitrary", "arbitrary"),
        name="matmul_res",
    )(x, w, res)


def _ssd_kernel(z_ref, x_ref, b_ref, c_ref, dt_ref,
                wx_ref, wb_ref, wc_ref, bx_ref, bb_ref, bc_ref,
                dtb_ref, alog_ref, dskip_ref, nw_ref, expand_ref,
                o_ref,
                h_ref, xbuf, bbuf, cbuf):
    L = SSD_CHUNK
    gw = GROUP_WIDTH
    ns = SSM_STATE
    halo = SUBLANES
    groups = range(SSD_GROUPS_PER_STEP)

    @pl.when(pl.program_id(2) == 0)
    def _():
        h_ref[...] = jnp.zeros_like(h_ref)
        xbuf[0:halo, :] = jnp.zeros((halo, xbuf.shape[1]), F32)
        bbuf[0:halo, :] = jnp.zeros((halo, bbuf.shape[1]), F32)
        cbuf[0:halo, :] = jnp.zeros((halo, cbuf.shape[1]), F32)

    def conv_silu(raw_ref, buf, w_ref, bias_ref):
        raw = raw_ref[...]
        buf[halo:halo + L, :] = raw
        padded = buf[...]
        acc = bias_ref[...] + w_ref[CONV_WIDTH - 1:CONV_WIDTH, :] * raw
        for back in range(1, CONV_WIDTH):
            tap = pltpu.roll(padded, back, axis=0)[halo:halo + L, :]
            k = CONV_WIDTH - 1 - back
            acc = acc + w_ref[k:k + 1, :] * tap
        buf[0:halo, :] = raw[L - halo:L, :]
        return _silu(acc)

    bs = conv_silu(b_ref, bbuf, wb_ref, bb_ref)
    cs = conv_silu(c_ref, cbuf, wc_ref, bc_ref)
    bs_b = [bs[:, g * ns:(g + 1) * ns].astype(BF16) for g in groups]
    cs_b = [cs[:, g * ns:(g + 1) * ns].astype(BF16) for g in groups]
    cb = [_dot_nt(cs_b[g], bs_b[g]) for g in groups]
    h_prev = [h_ref[:, g * gw:(g + 1) * gw] for g in groups]
    y_off = [_dot(cs_b[g], h_prev[g].astype(BF16)) for g in groups]

    dt_parts = _split3(dt_ref[...])
    dt_raw = [sum(_dot(p, expand_ref[g]) for p in dt_parts) for g in groups]

    xs = conv_silu(x_ref, xbuf, wx_ref, bx_ref)

    row = lax.broadcasted_iota(jnp.int32, (L, L), 0)
    col = lax.broadcasted_iota(jnp.int32, (L, L), 1)
    causal = row >= col
    tril = causal.astype(BF16)
    triu = (row <= col).astype(BF16)
    hrow = lax.broadcasted_iota(jnp.int32, (SUBLANES, gw), 0)
    hcol = lax.broadcasted_iota(jnp.int32, (SUBLANES, gw), 1)
    pick = (hcol == hrow * SSM_HEAD_DIM).astype(BF16)

    dt, a_parts = [], []
    for g in groups:
        lanes = slice(g * gw, (g + 1) * gw)
        dt.append(_softplus(dt_raw[g] + dtb_ref[:, lanes]))
        a_parts.append(_split2(dt[g] * (-jnp.exp(alog_ref[:, lanes]))))
    acs = [sum(_dot(tril, p) for p in a_parts[g]) for g in groups]
    a_rows = [[_dot_nt(pick, p).astype(BF16) for p in a_parts[g]] for g in groups]
    acs_rows = [sum(_dot(p, triu) for p in a_rows[g]) for g in groups]

    xdt = [xs[:, g * gw:(g + 1) * gw] * dt[g] for g in groups]
    xdt_b = [v.astype(BF16) for v in xdt]
    y_diag = []
    for g in groups:
        parts = []
        for r in range(SSM_HEADS_PER_GROUP):
            lo = r * SSM_HEAD_DIM
            seg = acs[g][:, lo:lo + 1] - acs_rows[g][r:r + 1, :]
            decay = jnp.exp(jnp.where(causal, seg, -jnp.inf))
            m = (cb[g] * decay).astype(BF16)
            parts.append(_dot(m, xdt_b[g][:, lo:lo + SSM_HEAD_DIM]))
        y_diag.append(jnp.concatenate(parts, axis=1))

    a_last = [acs[g][L - 1:L, :] for g in groups]
    xd = [(xdt[g] * jnp.exp(a_last[g] - acs[g])).astype(BF16) for g in groups]
    states = [_dot(bs[:, g * ns:(g + 1) * ns].T.astype(BF16), xd[g]) for g in groups]
    for g in groups:
        lanes = slice(g * gw, (g + 1) * gw)
        h_ref[:, lanes] = h_prev[g] * jnp.exp(a_last[g]) + states[g]
        y = y_diag[g] + y_off[g] * jnp.exp(acs[g]) + xs[:, lanes] * dskip_ref[:, lanes]
        yz = y * _silu(z_ref[:, lanes])
        o_ref[:, lanes] = (yz * _rms_scale(yz) * nw_ref[:, lanes]).astype(o_ref.dtype)


def _ssd(zx, dt, conv_w, conv_b, dtb_e, alog_e, dskip_e, norm_w, *, batch, seq):
    L = SSD_CHUNK
    nc = seq // L
    xw = SSD_GROUPS_PER_STEP * GROUP_WIDTH
    sw = SSD_GROUPS_PER_STEP * SSM_STATE
    x_blk = D_INNER // xw
    b_blk = (2 * D_INNER) // sw
    c_blk = b_blk + SSM_GROUPS * SSM_STATE // sw
    cw_b = D_INNER // sw
    cw_c = cw_b + SSM_GROUPS * SSM_STATE // sw

    def rows(b, g, c):
        return b * nc + c

    in_specs = [
        pl.BlockSpec((L, xw), lambda b, g, c: (rows(b, g, c), g)),
        pl.BlockSpec((L, xw), lambda b, g, c: (rows(b, g, c), x_blk + g)),
        pl.BlockSpec((L, sw), lambda b, g, c: (rows(b, g, c), b_blk + g)),
        pl.BlockSpec((L, sw), lambda b, g, c: (rows(b, g, c), c_blk + g)),
        pl.BlockSpec((L, LANES), lambda b, g, c: (rows(b, g, c), 0)),
        pl.BlockSpec((CONV_WIDTH, xw), lambda b, g, c: (0, g)),
        pl.BlockSpec((CONV_WIDTH, sw), lambda b, g, c: (0, cw_b + g)),
        pl.BlockSpec((CONV_WIDTH, sw), lambda b, g, c: (0, cw_c + g)),
        pl.BlockSpec((1, xw), lambda b, g, c: (0, g)),
        pl.BlockSpec((1, sw), lambda b, g, c: (0, cw_b + g)),
        pl.BlockSpec((1, sw), lambda b, g, c: (0, cw_c + g)),
        pl.BlockSpec((1, xw), lambda b, g, c: (0, g)),
        pl.BlockSpec((1, xw), lambda b, g, c: (0, g)),
        pl.BlockSpec((1, xw), lambda b, g, c: (0, g)),
        pl.BlockSpec((1, xw), lambda b, g, c: (0, g)),
        pl.BlockSpec((SSD_GROUPS_PER_STEP, LANES, GROUP_WIDTH), lambda b, g, c: (g, 0, 0)),
    ]
    head_of_lane = (jnp.arange(SSM_GROUPS)[:, None, None] * SSM_HEADS_PER_GROUP
                    + jnp.arange(GROUP_WIDTH)[None, None, :] // SSM_HEAD_DIM)
    expand = (jnp.arange(LANES)[None, :, None] == head_of_lane).astype(BF16)
    cb2 = conv_b.reshape(1, CONV_DIM)
    return pl.pallas_call(
        _ssd_kernel,
        out_shape=jax.ShapeDtypeStruct((batch * seq, D_INNER), BF16),
        grid=(batch, SSM_GROUPS // SSD_GROUPS_PER_STEP, nc),
        in_specs=in_specs,
        out_specs=pl.BlockSpec((L, xw), lambda b, g, c: (rows(b, g, c), g)),
        scratch_shapes=[
            pltpu.VMEM((SSM_STATE, xw), F32),
            pltpu.VMEM((L + SUBLANES, xw), F32),
            pltpu.VMEM((L + SUBLANES, sw), F32),
            pltpu.VMEM((L + SUBLANES, sw), F32),
        ],
        compiler_params=_params("arbitrary", "arbitrary", "arbitrary"),
        name="ssd",
    )(zx, zx, zx, zx, dt, conv_w, conv_w, conv_w, cb2, cb2, cb2,
      dtb_e, alog_e, dskip_e, norm_w.reshape(1, D_INNER), expand)


ATT_HEADS_PER_STEP = 8


def _attn_kernel(q_ref, k_ref, v_ref, o_ref):
    tq = ATT_BLOCK
    dh = SB_HEAD_DIM
    i = pl.program_id(2)
    scale = dh ** -0.5
    row = lax.broadcasted_iota(jnp.int32, (tq, tq), 0)
    col = lax.broadcasted_iota(jnp.int32, (tq, tq), 1)
    strict = col < row
    row2 = lax.broadcasted_iota(jnp.int32, (2 * tq, tq), 0) & (tq - 1)
    col2 = lax.broadcasted_iota(jnp.int32, (2 * tq, tq), 1)
    neg_later = jnp.where(row2 > col2, -1.0, 0.0).astype(BF16)

    heads = range(ATT_HEADS_PER_STEP)

    def block(j, state, diagonal):
        start = pl.multiple_of(j * tq, tq)
        cols = [slice(h * dh, (h + 1) * dh) for h in heads]
        z = [_dot_nt(q_ref[:, cols[h]], k_ref[pl.ds(start, tq), cols[h]]) * scale for h in heads]
        drop, logit, parts = [], [], []
        for h in heads:
            neg_abs = lax.bitcast_convert_type(
                lax.bitcast_convert_type(z[h], jnp.uint32) | jnp.uint32(0x80000000), F32)
            sp = jnp.maximum(z[h], 0.0) + jnp.log(1.0 + jnp.exp(neg_abs))
            logit.append(z[h] - sp)
            if diagonal:
                sp = jnp.where(strict, sp, 0.0)
            hi, lo = _split2(sp)
            parts.append(jnp.concatenate([hi, lo], axis=1))
            drop.append(sp)
        suffix = [_dot(parts[h], neg_later) for h in heads]
        weights = []
        for h in heads:
            w = jnp.exp(logit[h] + suffix[h] + state[h][0])
            if diagonal:
                w = jnp.where(strict, w, 0.0)
            weights.append(w.astype(BF16))
        out = [_dot(weights[h], v_ref[pl.ds(start, tq), cols[h]]) for h in heads]
        return tuple((state[h][0] - jnp.sum(drop[h], axis=1, keepdims=True), state[h][1] + out[h])
                     for h in heads)

    zero = (jnp.zeros((tq, 1), F32), jnp.zeros((tq, dh), F32))
    state = block(i, (zero,) * ATT_HEADS_PER_STEP, True)
    state = lax.fori_loop(0, i, lambda t, s: block(i - 1 - t, s, False), state)
    for h in range(ATT_HEADS_PER_STEP):
        o_ref[:, h * dh:(h + 1) * dh] = state[h][1].astype(o_ref.dtype)


def _attention(q, kv, *, batch, seq):
    tq = ATT_BLOCK
    nq = seq // tq
    width = ATT_HEADS_PER_STEP * SB_HEAD_DIM
    n_steps = SB_HEADS // ATT_HEADS_PER_STEP
    return pl.pallas_call(
        _attn_kernel,
        out_shape=jax.ShapeDtypeStruct((batch * seq, D_MODEL), BF16),
        grid=(batch, n_steps, nq),
        in_specs=[
            pl.BlockSpec((tq, width), lambda b, h, i: (b * nq + i, h)),
            pl.BlockSpec((seq, width), lambda b, h, i: (b, h)),
            pl.BlockSpec((seq, width), lambda b, h, i: (b, n_steps + h)),
        ],
        out_specs=pl.BlockSpec((tq, width), lambda b, h, i: (b * nq + i, h)),
        compiler_params=_params("arbitrary", "arbitrary", "arbitrary"),
        name="sb_attention",
    )(q, kv, kv)


ROUTER_FINE0 = N_GROUPS


ROUTE_E1, ROUTE_E2, ROUTE_W1, ROUTE_W2, ROUTE_R1, ROUTE_R2 = range(6)


def _router_kernel(h_ref, g_ref, w_ref, b_ref, hn_ref, route_ref, counts_ref, carry_ref):
    @pl.when(pl.program_id(0) == 0)
    def _():
        carry_ref[...] = jnp.zeros_like(carry_ref)

    x = h_ref[...]
    hn = x * _rms_scale(x) * g_ref[...]
    hn_ref[...] = hn
    x1 = hn.astype(BF16)
    x2 = (hn - x1.astype(F32)).astype(BF16)
    w = w_ref[...]
    w1 = w.astype(BF16)
    w2 = (w - w1.astype(F32)).astype(BF16)
    logits = _dot(x1, w1) + _dot(x2, w1) + _dot(x1, w2) + b_ref[...]

    lane_i = lax.broadcasted_iota(jnp.int32, logits.shape, 1)
    lane = lane_i.astype(F32)
    neg = -jnp.inf

    def first_argmax(v, vmax):
        return jnp.min(jnp.where(v == vmax, lane, float(LANES)), axis=1, keepdims=True)

    coarse = jnp.where(lane_i < N_GROUPS, logits, neg)
    cmax = jnp.max(coarse, axis=1, keepdims=True)
    p_g = 1.0 / jnp.sum(jnp.exp(coarse - cmax), axis=1, keepdims=True)
    g_idx = first_argmax(coarse, cmax)
    fine_lo = ROUTER_FINE0 + g_idx * EXPERTS_PER_GROUP
    in_group = (lane >= fine_lo) & (lane < fine_lo + EXPERTS_PER_GROUP)
    fine = jnp.where(in_group, logits, neg)
    m1 = jnp.max(fine, axis=1, keepdims=True)
    denom = jnp.sum(jnp.exp(fine - m1), axis=1, keepdims=True)
    i1 = first_argmax(fine, m1)
    rest = jnp.where(lane == i1, neg, fine)
    m2 = jnp.max(rest, axis=1, keepdims=True)
    i2 = first_argmax(rest, m2)
    p1 = 1.0 / denom
    p2 = jnp.exp(m2 - m1) / denom
    norm = p1 + p2
    gate1 = p_g * (p1 / norm)
    gate2 = p_g * (p2 / norm)

    tm = x.shape[0]
    chosen = (lane == i1) | (lane == i2)
    r = lax.broadcasted_iota(jnp.int32, (tm, tm), 0)
    c = lax.broadcasted_iota(jnp.int32, (tm, tm), 1)
    before = _dot((c < r).astype(BF16), chosen.astype(BF16)) + carry_ref[...]
    rank1 = jnp.sum(jnp.where(lane == i1, before, 0.0), axis=1, keepdims=True)
    rank2 = jnp.sum(jnp.where(lane == i2, before, 0.0), axis=1, keepdims=True)
    carry_ref[...] += jnp.sum(chosen.astype(F32), axis=0, keepdims=True)
    counts_ref[...] = carry_ref[...]

    route = jnp.zeros_like(logits)
    for slot, val in ((ROUTE_E1, i1 - ROUTER_FINE0), (ROUTE_E2, i2 - ROUTER_FINE0),
                      (ROUTE_W1, gate1), (ROUTE_W2, gate2), (ROUTE_R1, rank1), (ROUTE_R2, rank2)):
        route = jnp.where(lane_i == slot, val, route)
    route_ref[...] = route


def _router(h, g, w_router, b_router, *, tm):
    m, k = h.shape
    return pl.pallas_call(
        _router_kernel,
        out_shape=(jax.ShapeDtypeStruct((m, k), F32), jax.ShapeDtypeStruct((m, LANES), F32),
                   jax.ShapeDtypeStruct((1, LANES), F32)),
        grid=(m // tm,),
        in_specs=[
            pl.BlockSpec((tm, k), lambda i: (i, 0)),
            pl.BlockSpec((1, k), lambda i: (0, 0)),
            pl.BlockSpec((k, LANES), lambda i: (0, 0)),
            pl.BlockSpec((1, LANES), lambda i: (0, 0)),
        ],
        out_specs=(pl.BlockSpec((tm, k), lambda i: (i, 0)),
                   pl.BlockSpec((tm, LANES), lambda i: (i, 0)),
                   pl.BlockSpec((1, LANES), lambda i: (0, 0))),
        scratch_shapes=[pltpu.VMEM((1, LANES), F32)],
        compiler_params=_params("arbitrary"),
        name="moe_router",
    )(h, g.reshape(1, k), w_router, b_router)


DISPATCH_TOKENS = 1024
DMA_ISSUE_UNROLL = 8


def _row_copy(src, src_row, dst, dst_row, sem):
    return pltpu.make_async_copy(src.at[pl.ds(src_row, 1)], dst.at[pl.ds(dst_row, 1)], sem)


def _dispatch_kernel(pos1_ref, pos2_ref, hn_ref, xs_ref, sem):
    base = pl.program_id(0) * DISPATCH_TOKENS

    def issue(t, _):
        tok = base + t
        _row_copy(hn_ref, t, xs_ref, pos1_ref[tok], sem).start()
        _row_copy(hn_ref, t, xs_ref, pos2_ref[tok], sem).start()
        return 0

    lax.fori_loop(0, DISPATCH_TOKENS, issue, 0, unroll=DMA_ISSUE_UNROLL)

    for _ in range(2):
        pltpu.make_async_copy(hn_ref, xs_ref.at[pl.ds(0, DISPATCH_TOKENS)], sem).wait()


def _dispatch(hn, pos1, pos2):
    m, k = hn.shape
    return pl.pallas_call(
        _dispatch_kernel,
        out_shape=jax.ShapeDtypeStruct((2 * m, k), F32),
        grid_spec=pltpu.PrefetchScalarGridSpec(
            num_scalar_prefetch=2,
            grid=(m // DISPATCH_TOKENS,),
            in_specs=[pl.BlockSpec((DISPATCH_TOKENS, k), lambda i, *_: (i, 0))],
            out_specs=pl.BlockSpec(memory_space=pl.ANY),
            scratch_shapes=[pltpu.SemaphoreType.DMA],
        ),
        compiler_params=_params("arbitrary"),
        name="moe_dispatch",
    )(pos1, pos2, hn)


EXPERT_TILE = 256
EXPERT_HIDDEN_SLICES = 2


def _experts_kernel(tile_ref, expert_ref, lo_ref, hi_ref, first_ref, valid_ref,
                    x_ref, wg_ref, wu_ref, wd_ref, o_ref):
    del tile_ref, expert_ref
    n = pl.program_id(0)

    @pl.when(valid_ref[n] == 1)
    def _():
        x = x_ref[...].astype(BF16)
        rows = lax.broadcasted_iota(jnp.int32, (EXPERT_TILE, 1), 0)
        inside = (rows >= lo_ref[n]) & (rows < hi_ref[n])
        f = wg_ref.shape[-1]
        fs = f // EXPERT_HIDDEN_SLICES
        cols = [slice(s * fs, (s + 1) * fs) for s in range(EXPERT_HIDDEN_SLICES)]
        pre = [(_dot(x, wg_ref[0, 0, :, c].astype(BF16)), _dot(x, wu_ref[0, 0, :, c].astype(BF16)))
               for c in cols]
        hid = [jnp.where(inside, _silu(g) * u, 0.0).astype(BF16) for g, u in pre]
        y = sum(_dot(hid[s], wd_ref[0, 0, c, :].astype(BF16)) for s, c in enumerate(cols))

        @pl.when(first_ref[n] == 1)
        def _():
            o_ref[...] = y

        @pl.when(first_ref[n] == 0)
        def _():
            o_ref[...] += y


def _experts(work, xs, w_gate, w_up, w_down, layer):
    rows, k = xs.shape
    f = w_gate.shape[-1]
    n_items = work[0].shape[0]
    return pl.pallas_call(
        _experts_kernel,
        out_shape=jax.ShapeDtypeStruct((rows, k), F32),
        grid_spec=pltpu.PrefetchScalarGridSpec(
            num_scalar_prefetch=6,
            grid=(n_items,),
            in_specs=[
                pl.BlockSpec((EXPERT_TILE, k), lambda n, tile, *_: (tile[n], 0)),
                pl.BlockSpec((1, 1, k, f), lambda n, tile, expert, *_: (layer, expert[n], 0, 0)),
                pl.BlockSpec((1, 1, k, f), lambda n, tile, expert, *_: (layer, expert[n], 0, 0)),
                pl.BlockSpec((1, 1, f, k), lambda n, tile, expert, *_: (layer, expert[n], 0, 0)),
            ],
            out_specs=pl.BlockSpec((EXPERT_TILE, k), lambda n, tile, *_: (tile[n], 0)),
        ),
        compiler_params=_params("arbitrary"),
        name="moe_experts",
    )(*work, xs, w_gate, w_up, w_down)


COMBINE_TOKENS = 512


def _combine_kernel(pos1_ref, pos2_ref, h_ref, route_ref, ys_ref, *rest, final_norm):
    if final_norm:
        g_ref, o_ref, buf1, buf2, sem = rest
    else:
        o_ref, buf1, buf2, sem = rest
    base = pl.program_id(0) * COMBINE_TOKENS

    def issue(t, _):
        tok = base + t
        _row_copy(ys_ref, pos1_ref[tok], buf1, t, sem).start()
        _row_copy(ys_ref, pos2_ref[tok], buf2, t, sem).start()
        return 0

    lax.fori_loop(0, COMBINE_TOKENS, issue, 0, unroll=DMA_ISSUE_UNROLL)

    pltpu.make_async_copy(ys_ref.at[pl.ds(0, COMBINE_TOKENS)], buf1, sem).wait()
    pltpu.make_async_copy(ys_ref.at[pl.ds(0, COMBINE_TOKENS)], buf2, sem).wait()
    route = route_ref[...]
    w1 = route[:, ROUTE_W1:ROUTE_W1 + 1]
    w2 = route[:, ROUTE_W2:ROUTE_W2 + 1]
    out = h_ref[...] + w1 * buf1[...] + w2 * buf2[...]
    if final_norm:
        out = out * _rms_scale(out) * g_ref[...]
    o_ref[...] = out


def _combine(h, route, ys, pos1, pos2, final_gain=None):
    m, k = h.shape
    tm = COMBINE_TOKENS
    final_norm = final_gain is not None
    in_specs = [
        pl.BlockSpec((tm, k), lambda i, *_: (i, 0)),
        pl.BlockSpec((tm, LANES), lambda i, *_: (i, 0)),
        pl.BlockSpec(memory_space=pl.ANY),
    ]
    operands = [h, route, ys]
    if final_norm:
        in_specs.append(pl.BlockSpec((1, k), lambda i, *_: (0, 0)))
        operands.append(final_gain.reshape(1, k))
    return pl.pallas_call(
        functools.partial(_combine_kernel, final_norm=final_norm),
        out_shape=jax.ShapeDtypeStruct((m, k), F32),
        grid_spec=pltpu.PrefetchScalarGridSpec(
            num_scalar_prefetch=2,
            grid=(m // tm,),
            in_specs=in_specs,
            out_specs=pl.BlockSpec((tm, k), lambda i, *_: (i, 0)),
            scratch_shapes=[pltpu.VMEM((tm, k), F32), pltpu.VMEM((tm, k), F32),
                            pltpu.SemaphoreType.DMA],
        ),
        compiler_params=_params("arbitrary"),
        name="moe_combine",
    )(pos1, pos2, *operands)


def _work_list(counts, n_rows):
    tm = EXPERT_TILE
    n_tiles = n_rows // tm
    n_items = n_tiles + N_EXPERTS - 1
    i32 = jnp.int32
    count_le = lambda sorted_vals, q: jnp.sum((sorted_vals[None, :] <= q[:, None]).astype(i32), axis=1)
    pick = lambda table, idx: jnp.sum(
        jnp.where(idx[:, None] == jnp.arange(table.shape[0], dtype=i32)[None, :], table[None, :], 0), axis=1)
    e_ids = jnp.arange(N_EXPERTS, dtype=i32)
    ends = jnp.sum(jnp.where(e_ids[None, :] <= e_ids[:, None], counts[None, :], 0), axis=1)
    starts = ends - counts
    tile_ids = jnp.arange(n_tiles, dtype=i32)
    tile_lo = tile_ids * tm
    e_first = count_le(ends, tile_lo)
    e_last = count_le(ends, tile_lo + (tm - 1))
    per_tile = e_last - e_first + 1
    item0 = jnp.sum(jnp.where(tile_ids[None, :] < tile_ids[:, None], per_tile[None, :], 0), axis=1)
    total = jnp.sum(per_tile)
    n = jnp.arange(n_items, dtype=i32)
    valid = n < total
    tile = jnp.where(valid, count_le(item0, n) - 1, n_tiles - 1)
    expert = jnp.where(valid, pick(e_first, tile) + (n - pick(item0, tile)), e_last[-1])
    expert = jnp.clip(expert, 0, N_EXPERTS - 1)
    lo = jnp.clip(pick(starts, expert) - tile * tm, 0, tm)
    hi = jnp.clip(pick(ends, expert) - tile * tm, 0, tm)
    first = (n == pick(item0, tile)) & valid
    as_i32 = lambda a: a.astype(i32)
    return tuple(map(as_i32, (tile, expert, lo, hi, first, valid))), starts


def _router_weights(w_coarse, b_coarse, w_fine, b_fine):
    pad = LANES - N_GROUPS - N_EXPERTS
    w = jnp.concatenate([w_coarse, w_fine.reshape(D_MODEL, N_EXPERTS),
                         jnp.zeros((D_MODEL, pad), F32)], axis=1)
    b = jnp.concatenate([b_coarse, b_fine.reshape(N_EXPERTS), jnp.zeros((pad,), F32)])
    return w, b.reshape(1, LANES)


def _moe(h, norm_g, w_coarse, b_coarse, w_fine, b_fine, w_gate, w_up, w_down, layer, final_gain=None):
    w_r, b_r = _router_weights(w_coarse, b_coarse, w_fine, b_fine)
    hn, route, counts = _router(h, norm_g, w_r, b_r, tm=256)
    counts = counts[0, ROUTER_FINE0:ROUTER_FINE0 + N_EXPERTS].astype(jnp.int32)
    work, starts = _work_list(counts, 2 * h.shape[0])
    ids = route[:, :ROUTE_R2 + 1].astype(jnp.int32)
    e_ids = jnp.arange(N_EXPERTS, dtype=jnp.int32)[None, :]
    start_of = lambda e: jnp.sum(jnp.where(e[:, None] == e_ids, starts[None, :], 0), axis=1)
    pos1 = start_of(ids[:, ROUTE_E1]) + ids[:, ROUTE_R1]
    pos2 = start_of(ids[:, ROUTE_E2]) + ids[:, ROUTE_R2]
    xs = _dispatch(hn, pos1, pos2)
    ys = _experts(work, xs, w_gate, w_up, w_down, layer)
    return _combine(h, route, ys, pos1, pos2, final_gain)


def kernel(x, mix_norm, ffn_norm, ssm_w_in, ssm_conv_w, ssm_conv_b, ssm_dt_bias, ssm_a_log, ssm_d,
           ssm_norm_w, ssm_w_out, kv_norm, w_k, w_v, sb_w_q, sb_w_out, moe_w_coarse, moe_b_coarse,
           moe_w_fine, moe_b_fine, moe_w_gate, moe_w_up, moe_w_down, final_norm):
    batch, seq, d = x.shape
    h = x.reshape(batch * seq, d)

    zx, dt = _in_proj(h, mix_norm[0], ssm_w_in)
    expand = lambda p: jnp.repeat(p, SSM_HEAD_DIM).reshape(1, D_INNER)
    yn = _ssd(zx, dt, ssm_conv_w[0], ssm_conv_b[0], expand(ssm_dt_bias[0]), expand(ssm_a_log[0]),
              expand(ssm_d[0]), ssm_norm_w[0], batch=batch, seq=seq)
    h = _matmul_res(yn, ssm_w_out, h, tm=1024, tn=512)
    h = _moe(h, ffn_norm[0], moe_w_coarse[0], moe_b_coarse[0], moe_w_fine[0], moe_b_fine[0],
             moe_w_gate, moe_w_up, moe_w_down, 0)

    kv = _norm_matmul(h, kv_norm, [w_k, w_v], tm=1024, tn=512, out_dtype=BF16)

    q = _norm_matmul(h, mix_norm[1], [sb_w_q], tm=1024, tn=1024, out_dtype=BF16)
    o = _attention(q, kv, batch=batch, seq=seq)
    h = _matmul_res(o, sb_w_out, h, tm=1024, tn=512)
    h = _moe(h, ffn_norm[1], moe_w_coarse[1], moe_b_coarse[1], moe_w_fine[1], moe_b_fine[1],
             moe_w_gate, moe_w_up, moe_w_down, 1, final_gain=final_norm)
    return h.reshape(batch, seq, d)
```

```python
import functools

import jax
import jax.numpy as jnp
from jax import lax
from jax.experimental import pallas as pl
from jax.experimental.pallas import tpu as pltpu

F32 = jnp.float32
BF16 = jnp.bfloat16
EPS = 1e-5

D_MODEL = 2048
D_INNER = 4096
SSM_HEAD_DIM = 64
SSM_HEADS = 64
SSM_GROUPS = 8
SSM_HEADS_PER_GROUP = 8
SSM_STATE = 128
GROUP_WIDTH = D_INNER // SSM_GROUPS
CONV_WIDTH = 4
CONV_DIM = D_INNER + 2 * SSM_GROUPS * SSM_STATE
IN_PROJ_DIM = D_INNER + CONV_DIM + SSM_HEADS
SB_HEADS = 16
SB_HEAD_DIM = 128
N_GROUPS = 4
EXPERTS_PER_GROUP = 4
N_EXPERTS = 16
D_EXPERT = 512

LANES = 128
SUBLANES = 8
DT_COL_BLOCK = (D_INNER + CONV_DIM) // LANES
SSD_CHUNK = 128
SSD_GROUPS_PER_STEP = 8
ATT_BLOCK = 256
VMEM_LIMIT = 56 * 1024 * 1024


def _dot(a, b):
    return jnp.dot(a, b, preferred_element_type=F32)


def _dot_nt(a, b):
    return lax.dot_general(a, b, (((1,), (1,)), ((), ())), preferred_element_type=F32)


def _split3(v):
    v1 = v.astype(BF16)
    r = v - v1.astype(F32)
    v2 = r.astype(BF16)
    r = r - v2.astype(F32)
    return v1, v2, r.astype(BF16)


def _split2(v):
    hi = v.astype(BF16)
    return hi, (v - hi.astype(F32)).astype(BF16)


def _dot3_left(m, v):
    v1, v2, v3 = _split3(v)
    return _dot(m, v1) + _dot(m, v2) + _dot(m, v3)


def _dot3_right(v, m):
    v1, v2, v3 = _split3(v)
    return _dot(v1, m) + _dot(v2, m) + _dot(v3, m)


def _softplus(x):
    return jnp.maximum(x, 0.0) + jnp.log(1.0 + jnp.exp(-jnp.abs(x)))


def _silu(x):
    half = 0.5 * x
    return half + half * jnp.tanh(half)


def _rms_scale(x):
    return lax.rsqrt(jnp.mean(x * x, axis=-1, keepdims=True) + EPS)


def _params(*sem):
    return pltpu.CompilerParams(dimension_semantics=sem, vmem_limit_bytes=VMEM_LIMIT)


def _weight_spec(w, k, tn, col_tile):
    if w.ndim == 2:
        return pl.BlockSpec((k, tn), lambda i, j: (0, col_tile(i, j)))
    return pl.BlockSpec((None, k, tn), lambda i, j: (0, 0, col_tile(i, j)))


def _norm_matmul_kernel(x_ref, g_ref, *rest, tiles):
    w_refs, (o_ref, xn_ref) = rest[:len(tiles)], rest[len(tiles):]
    j = pl.program_id(1)

    @pl.when(j == 0)
    def _():
        x = x_ref[...]
        xn_ref[...] = (x * _rms_scale(x) * g_ref[...]).astype(BF16)

    first = 0
    for w_ref, count in zip(w_refs, tiles):
        @pl.when((j >= first) & (j < first + count))
        def _(w_ref=w_ref):
            o_ref[...] = _dot(xn_ref[...], w_ref[...]).astype(o_ref.dtype)
        first += count


def _norm_matmul(x, g, weights, *, tm, tn, out_dtype):
    m, k = x.shape
    assert all(w.shape[-1] % tn == 0 for w in weights)
    tiles = tuple(w.shape[-1] // tn for w in weights)
    n = sum(w.shape[-1] for w in weights)
    firsts = [sum(tiles[:a]) for a in range(len(tiles))]
    w_specs = [_weight_spec(w, k, tn, lambda i, j, lo=lo, count=count: jnp.clip(j - lo, 0, count - 1))
               for w, lo, count in zip(weights, firsts, tiles)]
    return pl.pallas_call(
        functools.partial(_norm_matmul_kernel, tiles=tiles),
        out_shape=jax.ShapeDtypeStruct((m, n), out_dtype),
        grid=(m // tm, sum(tiles)),
        in_specs=[
            pl.BlockSpec((tm, k), lambda i, j: (i, 0)),
            pl.BlockSpec((1, k), lambda i, j: (0, 0)),
            *w_specs,
        ],
        out_specs=pl.BlockSpec((tm, tn), lambda i, j: (i, j)),
        scratch_shapes=[pltpu.VMEM((tm, k), BF16)],
        compiler_params=_params("arbitrary", "arbitrary"),
        name="norm_matmul",
    )(x, g.reshape(1, k), *weights)


IN_PROJ_ROWS = 1024
IN_PROJ_COLS = 512


def _in_proj_kernel(x_ref, g_ref, w_ref, wdt_ref, o_ref, dt_ref, xn_ref):
    @pl.when(pl.program_id(1) == 0)
    def _():
        x = x_ref[...]
        xn = (x * _rms_scale(x) * g_ref[...]).astype(BF16)
        xn_ref[...] = xn
        head = lax.broadcasted_iota(jnp.int32, dt_ref.shape, 1)
        dt_ref[...] = jnp.where(head < SSM_HEADS, _dot_nt(xn, wdt_ref[...].astype(BF16)), 0.0)

    o_ref[...] = _dot_nt(xn_ref[...], w_ref[...].astype(BF16))


def _in_proj(x, g, w_in):
    m, k = x.shape
    w_in_t = jnp.swapaxes(w_in, 1, 2)
    tm, tn = IN_PROJ_ROWS, IN_PROJ_COLS
    n_main = D_INNER + CONV_DIM
    return pl.pallas_call(
        _in_proj_kernel,
        out_shape=(jax.ShapeDtypeStruct((m, n_main), F32), jax.ShapeDtypeStruct((m, LANES), F32)),
        grid=(m // tm, n_main // tn),
        in_specs=[
            pl.BlockSpec((tm, k), lambda i, j: (i, 0)),
            pl.BlockSpec((1, k), lambda i, j: (0, 0)),
            pl.BlockSpec((None, tn, k), lambda i, j: (0, j, 0)),
            pl.BlockSpec((None, LANES, k), lambda i, j: (0, DT_COL_BLOCK, 0)),
        ],
        out_specs=(pl.BlockSpec((tm, tn), lambda i, j: (i, j)),
                   pl.BlockSpec((tm, LANES), lambda i, j: (i, 0))),
        scratch_shapes=[pltpu.VMEM((tm, k), BF16)],
        compiler_params=_params("arbitrary", "arbitrary"),
        name="in_proj",
    )(x, g.reshape(1, k), w_in_t, w_in_t)


def _matmul_res_kernel(x_ref, w_ref, r_ref, o_ref):
    o_ref[...] = r_ref[...] + _dot(x_ref[...], w_ref[...])


def _matmul_res(x, w, res, *, tm, tn):
    m, k = x.shape
    n = w.shape[-1]
    return pl.pallas_call(
        _matmul_res_kernel,
        out_shape=jax.ShapeDtypeStruct((m, n), F32),
        grid=(m // tm, n // tn),
        in_specs=[
            pl.BlockSpec((tm, k), lambda i, j: (i, 0)),
            _weight_spec(w, k, tn, lambda i, j: j),
            pl.BlockSpec((tm, tn), lambda i, j: (i, j)),
        ],
        out_specs=pl.BlockSpec((tm, tn), lambda i, j: (i, j)),
        compiler_params=_params("arbitrary", "arbitrary"),
        name="matmul_res",
    )(x, w, res)


def _ssd_kernel(z_ref, x_ref, b_ref, c_ref, dt_ref,
                wx_ref, wb_ref, wc_ref, bx_ref, bb_ref, bc_ref,
                dtb_ref, alog_ref, dskip_ref, nw_ref, expand_ref,
                o_ref,
                h_ref, xbuf, bbuf, cbuf):
    L = SSD_CHUNK
    gw = GROUP_WIDTH
    ns = SSM_STATE
    halo = SUBLANES
    groups = range(SSD_GROUPS_PER_STEP)

    @pl.when(pl.program_id(2) == 0)
    def _():
        h_ref[...] = jnp.zeros_like(h_ref)
        xbuf[0:halo, :] = jnp.zeros((halo, xbuf.shape[1]), F32)
        bbuf[0:halo, :] = jnp.zeros((halo, bbuf.shape[1]), F32)
        cbuf[0:halo, :] = jnp.zeros((halo, cbuf.shape[1]), F32)

    def conv_silu(raw_ref, buf, w_ref, bias_ref):
        raw = raw_ref[...]
        buf[halo:halo + L, :] = raw
        padded = buf[...]
        acc = bias_ref[...] + w_ref[CONV_WIDTH - 1:CONV_WIDTH, :] * raw
        for back in range(1, CONV_WIDTH):
            tap = pltpu.roll(padded, back, axis=0)[halo:halo + L, :]
            k = CONV_WIDTH - 1 - back
            acc = acc + w_ref[k:k + 1, :] * tap
        buf[0:halo, :] = raw[L - halo:L, :]
        return _silu(acc)

    bs = conv_silu(b_ref, bbuf, wb_ref, bb_ref)
    cs = conv_silu(c_ref, cbuf, wc_ref, bc_ref)
    bs_b = [bs[:, g * ns:(g + 1) * ns].astype(BF16) for g in groups]
    cs_b = [cs[:, g * ns:(g + 1) * ns].astype(BF16) for g in groups]
    cb = [_dot_nt(cs_b[g], bs_b[g]) for g in groups]
    h_prev = [h_ref[:, g * gw:(g + 1) * gw] for g in groups]
    y_off = [_dot(cs_b[g], h_prev[g].astype(BF16)) for g in groups]

    dt_parts = _split3(dt_ref[...])
    dt_raw = [sum(_dot(p, expand_ref[g]) for p in dt_parts) for g in groups]

    xs = conv_silu(x_ref, xbuf, wx_ref, bx_ref)

    row = lax.broadcasted_iota(jnp.int32, (L, L), 0)
    col = lax.broadcasted_iota(jnp.int32, (L, L), 1)
    causal = row >= col
    tril = causal.astype(BF16)
    triu = (row <= col).astype(BF16)
    hrow = lax.broadcasted_iota(jnp.int32, (SUBLANES, gw), 0)
    hcol = lax.broadcasted_iota(jnp.int32, (SUBLANES, gw), 1)
    pick = (hcol == hrow * SSM_HEAD_DIM).astype(BF16)

    dt, a_parts = [], []
    for g in groups:
        lanes = slice(g * gw, (g + 1) * gw)
        dt.append(_softplus(dt_raw[g] + dtb_ref[:, lanes]))
        a_parts.append(_split2(dt[g] * (-jnp.exp(alog_ref[:, lanes]))))
    acs = [sum(_dot(tril, p) for p in a_parts[g]) for g in groups]
    a_rows = [[_dot_nt(pick, p).astype(BF16) for p in a_parts[g]] for g in groups]
    acs_rows = [sum(_dot(p, triu) for p in a_rows[g]) for g in groups]

    xdt = [xs[:, g * gw:(g + 1) * gw] * dt[g] for g in groups]
    xdt_b = [v.astype(BF16) for v in xdt]
    y_diag = []
    for g in groups:
        parts = []
        for r in range(SSM_HEADS_PER_GROUP):
            lo = r * SSM_HEAD_DIM
            seg = acs[g][:, lo:lo + 1] - acs_rows[g][r:r + 1, :]
            decay = jnp.exp(jnp.where(causal, seg, -jnp.inf))
            m = (cb[g] * decay).astype(BF16)
            parts.append(_dot(m, xdt_b[g][:, lo:lo + SSM_HEAD_DIM]))
        y_diag.append(jnp.concatenate(parts, axis=1))

    a_last = [acs[g][L - 1:L, :] for g in groups]
    xd = [(xdt[g] * jnp.exp(a_last[g] - acs[g])).astype(BF16) for g in groups]
    states = [_dot(bs[:, g * ns:(g + 1) * ns].T.astype(BF16), xd[g]) for g in groups]
    for g in groups:
        lanes = slice(g * gw, (g + 1) * gw)
        h_ref[:, lanes] = h_prev[g] * jnp.exp(a_last[g]) + states[g]
        y = y_diag[g] + y_off[g] * jnp.exp(acs[g]) + xs[:, lanes] * dskip_ref[:, lanes]
        yz = y * _silu(z_ref[:, lanes])
        o_ref[:, lanes] = (yz * _rms_scale(yz) * nw_ref[:, lanes]).astype(o_ref.dtype)


def _ssd(zx, dt, conv_w, conv_b, dtb_e, alog_e, dskip_e, norm_w, *, batch, seq):
    L = SSD_CHUNK
    nc = seq // L
    xw = SSD_GROUPS_PER_STEP * GROUP_WIDTH
    sw = SSD_GROUPS_PER_STEP * SSM_STATE
    x_blk = D_INNER // xw
    b_blk = (2 * D_INNER) // sw
    c_blk = b_blk + SSM_GROUPS * SSM_STATE // sw
    cw_b = D_INNER // sw
    cw_c = cw_b + SSM_GROUPS * SSM_STATE // sw

    def rows(b, g, c):
        return b * nc + c

    in_specs = [
        pl.BlockSpec((L, xw), lambda b, g, c: (rows(b, g, c), g)),
        pl.BlockSpec((L, xw), lambda b, g, c: (rows(b, g, c), x_blk + g)),
        pl.BlockSpec((L, sw), lambda b, g, c: (rows(b, g, c), b_blk + g)),
        pl.BlockSpec((L, sw), lambda b, g, c: (rows(b, g, c), c_blk + g)),
        pl.BlockSpec((L, LANES), lambda b, g, c: (rows(b, g, c), 0)),
        pl.BlockSpec((CONV_WIDTH, xw), lambda b, g, c: (0, g)),
        pl.BlockSpec((CONV_WIDTH, sw), lambda b, g, c: (0, cw_b + g)),
        pl.BlockSpec((CONV_WIDTH, sw), lambda b, g, c: (0, cw_c + g)),
        pl.BlockSpec((1, xw), lambda b, g, c: (0, g)),
        pl.BlockSpec((1, sw), lambda b, g, c: (0, cw_b + g)),
        pl.BlockSpec((1, sw), lambda b, g, c: (0, cw_c + g)),
        pl.BlockSpec((1, xw), lambda b, g, c: (0, g)),
        pl.BlockSpec((1, xw), lambda b, g, c: (0, g)),
        pl.BlockSpec((1, xw), lambda b, g, c: (0, g)),
        pl.BlockSpec((1, xw), lambda b, g, c: (0, g)),
        pl.BlockSpec((SSD_GROUPS_PER_STEP, LANES, GROUP_WIDTH), lambda b, g, c: (g, 0, 0)),
    ]
    head_of_lane = (jnp.arange(SSM_GROUPS)[:, None, None] * SSM_HEADS_PER_GROUP
                    + jnp.arange(GROUP_WIDTH)[None, None, :] // SSM_HEAD_DIM)
    expand = (jnp.arange(LANES)[None, :, None] == head_of_lane).astype(BF16)
    cb2 = conv_b.reshape(1, CONV_DIM)
    return pl.pallas_call(
        _ssd_kernel,
        out_shape=jax.ShapeDtypeStruct((batch * seq, D_INNER), BF16),
        grid=(batch, SSM_GROUPS // SSD_GROUPS_PER_STEP, nc),
        in_specs=in_specs,
        out_specs=pl.BlockSpec((L, xw), lambda b, g, c: (rows(b, g, c), g)),
        scratch_shapes=[
            pltpu.VMEM((SSM_STATE, xw), F32),
            pltpu.VMEM((L + SUBLANES, xw), F32),
            pltpu.VMEM((L + SUBLANES, sw), F32),
            pltpu.VMEM((L + SUBLANES, sw), F32),
        ],
        compiler_params=_params("arbitrary", "arbitrary", "arbitrary"),
        name="ssd",
    )(zx, zx, zx, zx, dt, conv_w, conv_w, conv_w, cb2, cb2, cb2,
      dtb_e, alog_e, dskip_e, norm_w.reshape(1, D_INNER), expand)


ATT_HEADS_PER_STEP = 8


def _attn_kernel(q_ref, k_ref, v_ref, o_ref):
    tq = ATT_BLOCK
    dh = SB_HEAD_DIM
    i = pl.program_id(2)
    scale = dh ** -0.5
    row = lax.broadcasted_iota(jnp.int32, (tq, tq), 0)
    col = lax.broadcasted_iota(jnp.int32, (tq, tq), 1)
    strict = col < row
    row2 = lax.broadcasted_iota(jnp.int32, (2 * tq, tq), 0) & (tq - 1)
    col2 = lax.broadcasted_iota(jnp.int32, (2 * tq, tq), 1)
    neg_later = jnp.where(row2 > col2, -1.0, 0.0).astype(BF16)

    heads = range(ATT_HEADS_PER_STEP)

    def block(j, state, diagonal):
        start = pl.multiple_of(j * tq, tq)
        cols = [slice(h * dh, (h + 1) * dh) for h in heads]
        z = [_dot_nt(q_ref[:, cols[h]], k_ref[pl.ds(start, tq), cols[h]]) * scale for h in heads]
        drop, logit, parts = [], [], []
        for h in heads:
            neg_abs = lax.bitcast_convert_type(
                lax.bitcast_convert_type(z[h], jnp.uint32) | jnp.uint32(0x80000000), F32)
            sp = jnp.maximum(z[h], 0.0) + jnp.log(1.0 + jnp.exp(neg_abs))
            logit.append(z[h] - sp)
            if diagonal:
                sp = jnp.where(strict, sp, 0.0)
            hi, lo = _split2(sp)
            parts.append(jnp.concatenate([hi, lo], axis=1))
            drop.append(sp)
        suffix = [_dot(parts[h], neg_later) for h in heads]
        weights = []
        for h in heads:
            w = jnp.exp(logit[h] + suffix[h] + state[h][0])
            if diagonal:
                w = jnp.where(strict, w, 0.0)
            weights.append(w.astype(BF16))
        out = [_dot(weights[h], v_ref[pl.ds(start, tq), cols[h]]) for h in heads]
        return tuple((state[h][0] - jnp.sum(drop[h], axis=1, keepdims=True), state[h][1] + out[h])
                     for h in heads)

    zero = (jnp.zeros((tq, 1), F32), jnp.zeros((tq, dh), F32))
    state = block(i, (zero,) * ATT_HEADS_PER_STEP, True)
    state = lax.fori_loop(0, i, lambda t, s: block(i - 1 - t, s, False), state)
    for h in range(ATT_HEADS_PER_STEP):
        o_ref[:, h * dh:(h + 1) * dh] = state[h][1].astype(o_ref.dtype)


def _attention(q, kv, *, batch, seq):
    tq = ATT_BLOCK
    nq = seq // tq
    width = ATT_HEADS_PER_STEP * SB_HEAD_DIM
    n_steps = SB_HEADS // ATT_HEADS_PER_STEP
    return pl.pallas_call(
        _attn_kernel,
        out_shape=jax.ShapeDtypeStruct((batch * seq, D_MODEL), BF16),
        grid=(batch, n_steps, nq),
        in_specs=[
            pl.BlockSpec((tq, width), lambda b, h, i: (b * nq + i, h)),
            pl.BlockSpec((seq, width), lambda b, h, i: (b, h)),
            pl.BlockSpec((seq, width), lambda b, h, i: (b, n_steps + h)),
        ],
        out_specs=pl.BlockSpec((tq, width), lambda b, h, i: (b * nq + i, h)),
        compiler_params=_params("arbitrary", "arbitrary", "arbitrary"),
        name="sb_attention",
    )(q, kv, kv)


ROUTER_FINE0 = N_GROUPS


ROUTE_E1, ROUTE_E2, ROUTE_W1, ROUTE_W2, ROUTE_R1, ROUTE_R2 = range(6)


def _router_kernel(h_ref, g_ref, w_ref, b_ref, hn_ref, route_ref, counts_ref, carry_ref):
    @pl.when(pl.program_id(0) == 0)
    def _():
        carry_ref[...] = jnp.zeros_like(carry_ref)

    x = h_ref[...]
    hn = x * _rms_scale(x) * g_ref[...]
    hn_ref[...] = hn
    x1 = hn.astype(BF16)
    x2 = (hn - x1.astype(F32)).astype(BF16)
    w = w_ref[...]
    w1 = w.astype(BF16)
    w2 = (w - w1.astype(F32)).astype(BF16)
    logits = _dot(x1, w1) + _dot(x2, w1) + _dot(x1, w2) + b_ref[...]

    lane_i = lax.broadcasted_iota(jnp.int32, logits.shape, 1)
    lane = lane_i.astype(F32)
    neg = -jnp.inf

    def first_argmax(v, vmax):
        return jnp.min(jnp.where(v == vmax, lane, float(LANES)), axis=1, keepdims=True)

    coarse = jnp.where(lane_i < N_GROUPS, logits, neg)
    cmax = jnp.max(coarse, axis=1, keepdims=True)
    p_g = 1.0 / jnp.sum(jnp.exp(coarse - cmax), axis=1, keepdims=True)
    g_idx = first_argmax(coarse, cmax)
    fine_lo = ROUTER_FINE0 + g_idx * EXPERTS_PER_GROUP
    in_group = (lane >= fine_lo) & (lane < fine_lo + EXPERTS_PER_GROUP)
    fine = jnp.where(in_group, logits, neg)
    m1 = jnp.max(fine, axis=1, keepdims=True)
    denom = jnp.sum(jnp.exp(fine - m1), axis=1, keepdims=True)
    i1 = first_argmax(fine, m1)
    rest = jnp.where(lane == i1, neg, fine)
    m2 = jnp.max(rest, axis=1, keepdims=True)
    i2 = first_argmax(rest, m2)
    p1 = 1.0 / denom
    p2 = jnp.exp(m2 - m1) / denom
    norm = p1 + p2
    gate1 = p_g * (p1 / norm)
    gate2 = p_g * (p2 / norm)

    tm = x.shape[0]
    chosen = (lane == i1) | (lane == i2)
    r = lax.broadcasted_iota(jnp.int32, (tm, tm), 0)
    c = lax.broadcasted_iota(jnp.int32, (tm, tm), 1)
    before = _dot((c < r).astype(BF16), chosen.astype(BF16)) + carry_ref[...]
    rank1 = jnp.sum(jnp.where(lane == i1, before, 0.0), axis=1, keepdims=True)
    rank2 = jnp.sum(jnp.where(lane == i2, before, 0.0), axis=1, keepdims=True)
    carry_ref[...] += jnp.sum(chosen.astype(F32), axis=0, keepdims=True)
    counts_ref[...] = carry_ref[...]

    route = jnp.zeros_like(logits)
    for slot, val in ((ROUTE_E1, i1 - ROUTER_FINE0), (ROUTE_E2, i2 - ROUTER_FINE0),
                      (ROUTE_W1, gate1), (ROUTE_W2, gate2), (ROUTE_R1, rank1), (ROUTE_R2, rank2)):
        route = jnp.where(lane_i == slot, val, route)
    route_ref[...] = route


def _router(h, g, w_router, b_router, *, tm):
    m, k = h.shape
    return pl.pallas_call(
        _router_kernel,
        out_shape=(jax.ShapeDtypeStruct((m, k), F32), jax.ShapeDtypeStruct((m, LANES), F32),
                   jax.ShapeDtypeStruct((1, LANES), F32)),
        grid=(m // tm,),
        in_specs=[
            pl.BlockSpec((tm, k), lambda i: (i, 0)),
            pl.BlockSpec((1, k), lambda i: (0, 0)),
            pl.BlockSpec((k, LANES), lambda i: (0, 0)),
            pl.BlockSpec((1, LANES), lambda i: (0, 0)),
        ],
        out_specs=(pl.BlockSpec((tm, k), lambda i: (i, 0)),
                   pl.BlockSpec((tm, LANES), lambda i: (i, 0)),
                   pl.BlockSpec((1, LANES), lambda i: (0, 0))),
        scratch_shapes=[pltpu.VMEM((1, LANES), F32)],
        compiler_params=_params("arbitrary"),
        name="moe_router",
    )(h, g.reshape(1, k), w_router, b_router)


DISPATCH_TOKENS = 1024
DMA_ISSUE_UNROLL = 8


def _row_copy(src, src_row, dst, dst_row, sem):
    return pltpu.make_async_copy(src.at[pl.ds(src_row, 1)], dst.at[pl.ds(dst_row, 1)], sem)


def _dispatch_kernel(pos1_ref, pos2_ref, hn_ref, xs_ref, sem):
    base = pl.program_id(0) * DISPATCH_TOKENS

    def issue(t, _):
        tok = base + t
        _row_copy(hn_ref, t, xs_ref, pos1_ref[tok], sem).start()
        _row_copy(hn_ref, t, xs_ref, pos2_ref[tok], sem).start()
        return 0

    lax.fori_loop(0, DISPATCH_TOKENS, issue, 0, unroll=DMA_ISSUE_UNROLL)

    for _ in range(2):
        pltpu.make_async_copy(hn_ref, xs_ref.at[pl.ds(0, DISPATCH_TOKENS)], sem).wait()


def _dispatch(hn, pos1, pos2):
    m, k = hn.shape
    return pl.pallas_call(
        _dispatch_kernel,
        out_shape=jax.ShapeDtypeStruct((2 * m, k), F32),
        grid_spec=pltpu.PrefetchScalarGridSpec(
            num_scalar_prefetch=2,
            grid=(m // DISPATCH_TOKENS,),
            in_specs=[pl.BlockSpec((DISPATCH_TOKENS, k), lambda i, *_: (i, 0))],
            out_specs=pl.BlockSpec(memory_space=pl.ANY),
            scratch_shapes=[pltpu.SemaphoreType.DMA],
        ),
        compiler_params=_params("arbitrary"),
        name="moe_dispatch",
    )(pos1, pos2, hn)


EXPERT_TILE = 256
EXPERT_HIDDEN_SLICES = 2


def _experts_kernel(tile_ref, expert_ref, lo_ref, hi_ref, first_ref, valid_ref,
                    x_ref, wg_ref, wu_ref, wd_ref, o_ref):
    del tile_ref, expert_ref
    n = pl.program_id(0)

    @pl.when(valid_ref[n] == 1)
    def _():
        x = x_ref[...].astype(BF16)
        rows = lax.broadcasted_iota(jnp.int32, (EXPERT_TILE, 1), 0)
        inside = (rows >= lo_ref[n]) & (rows < hi_ref[n])
        f = wg_ref.shape[-1]
        fs = f // EXPERT_HIDDEN_SLICES
        cols = [slice(s * fs, (s + 1) * fs) for s in range(EXPERT_HIDDEN_SLICES)]
        pre = [(_dot(x, wg_ref[0, 0, :, c].astype(BF16)), _dot(x, wu_ref[0, 0, :, c].astype(BF16)))
               for c in cols]
        hid = [jnp.where(inside, _silu(g) * u, 0.0).astype(BF16) for g, u in pre]
        y = sum(_dot(hid[s], wd_ref[0, 0, c, :].astype(BF16)) for s, c in enumerate(cols))

        @pl.when(first_ref[n] == 1)
        def _():
            o_ref[...] = y

        @pl.when(first_ref[n] == 0)
        def _():
            o_ref[...] += y


def _experts(work, xs, w_gate, w_up, w_down, layer):
    rows, k = xs.shape
    f = w_gate.shape[-1]
    n_items = work[0].shape[0]
    return pl.pallas_call(
        _experts_kernel,
        out_shape=jax.ShapeDtypeStruct((rows, k), F32),
        grid_spec=pltpu.PrefetchScalarGridSpec(
            num_scalar_prefetch=6,
            grid=(n_items,),
            in_specs=[
                pl.BlockSpec((EXPERT_TILE, k), lambda n, tile, *_: (tile[n], 0)),
                pl.BlockSpec((1, 1, k, f), lambda n, tile, expert, *_: (layer, expert[n], 0, 0)),
                pl.BlockSpec((1, 1, k, f), lambda n, tile, expert, *_: (layer, expert[n], 0, 0)),
                pl.BlockSpec((1, 1, f, k), lambda n, tile, expert, *_: (layer, expert[n], 0, 0)),
            ],
            out_specs=pl.BlockSpec((EXPERT_TILE, k), lambda n, tile, *_: (tile[n], 0)),
        ),
        compiler_params=_params("arbitrary"),
        name="moe_experts",
    )(*work, xs, w_gate, w_up, w_down)


COMBINE_TOKENS = 512


def _combine_kernel(pos1_ref, pos2_ref, h_ref, route_ref, ys_ref, *rest, final_norm):
    if final_norm:
        g_ref, o_ref, buf1, buf2, sem = rest
    else:
        o_ref, buf1, buf2, sem = rest
    base = pl.program_id(0) * COMBINE_TOKENS

    def issue(t, _):
        tok = base + t
        _row_copy(ys_ref, pos1_ref[tok], buf1, t, sem).start()
        _row_copy(ys_ref, pos2_ref[tok], buf2, t, sem).start()
        return 0

    lax.fori_loop(0, COMBINE_TOKENS, issue, 0, unroll=DMA_ISSUE_UNROLL)

    pltpu.make_async_copy(ys_ref.at[pl.ds(0, COMBINE_TOKENS)], buf1, sem).wait()
    pltpu.make_async_copy(ys_ref.at[pl.ds(0, COMBINE_TOKENS)], buf2, sem).wait()
    route = route_ref[...]
    w1 = route[:, ROUTE_W1:ROUTE_W1 + 1]
    w2 = route[:, ROUTE_W2:ROUTE_W2 + 1]
    out = h_ref[...] + w1 * buf1[...] + w2 * buf2[...]
    if final_norm:
        out = out * _rms_scale(out) * g_ref[...]
    o_ref[...] = out


def _combine(h, route, ys, pos1, pos2, final_gain=None):
    m, k = h.shape
    tm = COMBINE_TOKENS
    final_norm = final_gain is not None
    in_specs = [
        pl.BlockSpec((tm, k), lambda i, *_: (i, 0)),
        pl.BlockSpec((tm, LANES), lambda i, *_: (i, 0)),
        pl.BlockSpec(memory_space=pl.ANY),
    ]
    operands = [h, route, ys]
    if final_norm:
        in_specs.append(pl.BlockSpec((1, k), lambda i, *_: (0, 0)))
        operands.append(final_gain.reshape(1, k))
    return pl.pallas_call(
        functools.partial(_combine_kernel, final_norm=final_norm),
        out_shape=jax.ShapeDtypeStruct((m, k), F32),
        grid_spec=pltpu.PrefetchScalarGridSpec(
            num_scalar_prefetch=2,
            grid=(m // tm,),
            in_specs=in_specs,
            out_specs=pl.BlockSpec((tm, k), lambda i, *_: (i, 0)),
            scratch_shapes=[pltpu.VMEM((tm, k), F32), pltpu.VMEM((tm, k), F32),
                            pltpu.SemaphoreType.DMA],
        ),
        compiler_params=_params("arbitrary"),
        name="moe_combine",
    )(pos1, pos2, *operands)


def _work_list(counts, n_rows):
    tm = EXPERT_TILE
    n_tiles = n_rows // tm
    n_items = n_tiles + N_EXPERTS - 1
    i32 = jnp.int32
    count_le = lambda sorted_vals, q: jnp.sum((sorted_vals[None, :] <= q[:, None]).astype(i32), axis=1)
    pick = lambda table, idx: jnp.sum(
        jnp.where(idx[:, None] == jnp.arange(table.shape[0], dtype=i32)[None, :], table[None, :], 0), axis=1)
    e_ids = jnp.arange(N_EXPERTS, dtype=i32)
    ends = jnp.sum(jnp.where(e_ids[None, :] <= e_ids[:, None], counts[None, :], 0), axis=1)
    starts = ends - counts
    tile_ids = jnp.arange(n_tiles, dtype=i32)
    tile_lo = tile_ids * tm
    e_first = count_le(ends, tile_lo)
    e_last = count_le(ends, tile_lo + (tm - 1))
    per_tile = e_last - e_first + 1
    item0 = jnp.sum(jnp.where(tile_ids[None, :] < tile_ids[:, None], per_tile[None, :], 0), axis=1)
    total = jnp.sum(per_tile)
    n = jnp.arange(n_items, dtype=i32)
    valid = n < total
    tile = jnp.where(valid, count_le(item0, n) - 1, n_tiles - 1)
    expert = jnp.where(valid, pick(e_first, tile) + (n - pick(item0, tile)), e_last[-1])
    expert = jnp.clip(expert, 0, N_EXPERTS - 1)
    lo = jnp.clip(pick(starts, expert) - tile * tm, 0, tm)
    hi = jnp.clip(pick(ends, expert) - tile * tm, 0, tm)
    first = (n == pick(item0, tile)) & valid
    as_i32 = lambda a: a.astype(i32)
    return tuple(map(as_i32, (tile, expert, lo, hi, first, valid))), starts


def _router_weights(w_coarse, b_coarse, w_fine, b_fine):
    pad = LANES - N_GROUPS - N_EXPERTS
    w = jnp.concatenate([w_coarse, w_fine.reshape(D_MODEL, N_EXPERTS),
                         jnp.zeros((D_MODEL, pad), F32)], axis=1)
    b = jnp.concatenate([b_coarse, b_fine.reshape(N_EXPERTS), jnp.zeros((pad,), F32)])
    return w, b.reshape(1, LANES)


def _moe(h, norm_g, w_coarse, b_coarse, w_fine, b_fine, w_gate, w_up, w_down, layer, final_gain=None):
    w_r, b_r = _router_weights(w_coarse, b_coarse, w_fine, b_fine)
    hn, route, counts = _router(h, norm_g, w_r, b_r, tm=512)
    counts = counts[0, ROUTER_FINE0:ROUTER_FINE0 + N_EXPERTS].astype(jnp.int32)
    work, starts = _work_list(counts, 2 * h.shape[0])
    ids = route[:, :ROUTE_R2 + 1].astype(jnp.int32)
    e_ids = jnp.arange(N_EXPERTS, dtype=jnp.int32)[None, :]
    start_of = lambda e: jnp.sum(jnp.where(e[:, None] == e_ids, starts[None, :], 0), axis=1)
    pos1 = start_of(ids[:, ROUTE_E1]) + ids[:, ROUTE_R1]
    pos2 = start_of(ids[:, ROUTE_E2]) + ids[:, ROUTE_R2]
    xs = _dispatch(hn, pos1, pos2)
    ys = _experts(work, xs, w_gate, w_up, w_down, layer)
    return _combine(h, route, ys, pos1, pos2, final_gain)


def kernel(x, mix_norm, ffn_norm, ssm_w_in, ssm_conv_w, ssm_conv_b, ssm_dt_bias, ssm_a_log, ssm_d,
           ssm_norm_w, ssm_w_out, kv_norm, w_k, w_v, sb_w_q, sb_w_out, moe_w_coarse, moe_b_coarse,
           moe_w_fine, moe_b_fine, moe_w_gate, moe_w_up, moe_w_down, final_norm):
    batch, seq, d = x.shape
    h = x.reshape(batch * seq, d)

    zx, dt = _in_proj(h, mix_norm[0], ssm_w_in)
    expand = lambda p: jnp.repeat(p, SSM_HEAD_DIM).reshape(1, D_INNER)
    yn = _ssd(zx, dt, ssm_conv_w[0], ssm_conv_b[0], expand(ssm_dt_bias[0]), expand(ssm_a_log[0]),
              expand(ssm_d[0]), ssm_norm_w[0], batch=batch, seq=seq)
    h = _matmul_res(yn, ssm_w_out.astype(BF16), h, tm=1024, tn=512)
    h = _moe(h, ffn_norm[0], moe_w_coarse[0], moe_b_coarse[0], moe_w_fine[0], moe_b_fine[0],
             moe_w_gate, moe_w_up, moe_w_down, 0)

    w_kv = jnp.concatenate([w_k.astype(BF16), w_v.astype(BF16)], axis=1)
    kv = _norm_matmul(h, kv_norm, [w_kv], tm=1024, tn=1024, out_dtype=BF16)

    q = _norm_matmul(h, mix_norm[1], [sb_w_q.astype(BF16)], tm=1024, tn=1024, out_dtype=BF16)
    o = _attention(q, kv, batch=batch, seq=seq)
    h = _matmul_res(o, sb_w_out.astype(BF16), h, tm=1024, tn=512)
    h = _moe(h, ffn_norm[1], moe_w_coarse[1], moe_b_coarse[1], moe_w_fine[1], moe_b_fine[1],
             moe_w_gate, moe_w_up, moe_w_down, 1, final_gain=final_norm)
    return h.reshape(batch, seq, d)
```

```python
import functools

import jax
import jax.numpy as jnp
from jax import lax
from jax.experimental import pallas as pl
from jax.experimental.pallas import tpu as pltpu

F32 = jnp.float32
BF16 = jnp.bfloat16
EPS = 1e-5

D_MODEL = 2048
D_INNER = 4096
SSM_HEAD_DIM = 64
SSM_HEADS = 64
SSM_GROUPS = 8
SSM_HEADS_PER_GROUP = 8
SSM_STATE = 128
GROUP_WIDTH = D_INNER // SSM_GROUPS
CONV_WIDTH = 4
CONV_DIM = D_INNER + 2 * SSM_GROUPS * SSM_STATE
IN_PROJ_DIM = D_INNER + CONV_DIM + SSM_HEADS
SB_HEADS = 16
SB_HEAD_DIM = 128
N_GROUPS = 4
EXPERTS_PER_GROUP = 4
N_EXPERTS = 16
D_EXPERT = 512

LANES = 128
SUBLANES = 8
DT_COL_BLOCK = (D_INNER + CONV_DIM) // LANES
SSD_CHUNK = 128
SSD_GROUPS_PER_STEP = 8
ATT_BLOCK = 256
VMEM_LIMIT = 56 * 1024 * 1024


def _dot(a, b):
    return jnp.dot(a, b, preferred_element_type=F32)


def _dot_nt(a, b):
    return lax.dot_general(a, b, (((1,), (1,)), ((), ())), preferred_element_type=F32)


def _split3(v):
    v1 = v.astype(BF16)
    r = v - v1.astype(F32)
    v2 = r.astype(BF16)
    r = r - v2.astype(F32)
    return v1, v2, r.astype(BF16)


def _split2(v):
    hi = v.astype(BF16)
    return hi, (v - hi.astype(F32)).astype(BF16)


def _dot3_left(m, v):
    v1, v2, v3 = _split3(v)
    return _dot(m, v1) + _dot(m, v2) + _dot(m, v3)


def _dot3_right(v, m):
    v1, v2, v3 = _split3(v)
    return _dot(v1, m) + _dot(v2, m) + _dot(v3, m)


def _softplus(x):
    return jnp.maximum(x, 0.0) + jnp.log(1.0 + jnp.exp(-jnp.abs(x)))


def _silu(x):
    half = 0.5 * x
    return half + half * jnp.tanh(half)


def _rms_scale(x):
    return lax.rsqrt(jnp.mean(x * x, axis=-1, keepdims=True) + EPS)


def _params(*sem):
    return pltpu.CompilerParams(dimension_semantics=sem, vmem_limit_bytes=VMEM_LIMIT)


def _weight_spec(w, k, tn, col_tile):
    if w.ndim == 2:
        return pl.BlockSpec((k, tn), lambda i, j: (0, col_tile(i, j)))
    return pl.BlockSpec((None, k, tn), lambda i, j: (0, 0, col_tile(i, j)))


def _norm_matmul_kernel(x_ref, g_ref, *rest, tiles):
    w_refs, (o_ref, xn_ref) = rest[:len(tiles)], rest[len(tiles):]
    j = pl.program_id(1)

    @pl.when(j == 0)
    def _():
        x = x_ref[...]
        xn_ref[...] = (x * _rms_scale(x) * g_ref[...]).astype(BF16)

    first = 0
    for w_ref, count in zip(w_refs, tiles):
        @pl.when((j >= first) & (j < first + count))
        def _(w_ref=w_ref):
            o_ref[...] = _dot(xn_ref[...], w_ref[...]).astype(o_ref.dtype)
        first += count


def _norm_matmul(x, g, weights, *, tm, tn, out_dtype):
    m, k = x.shape
    assert all(w.shape[-1] % tn == 0 for w in weights)
    tiles = tuple(w.shape[-1] // tn for w in weights)
    n = sum(w.shape[-1] for w in weights)
    firsts = [sum(tiles[:a]) for a in range(len(tiles))]
    w_specs = [_weight_spec(w, k, tn, lambda i, j, lo=lo, count=count: jnp.clip(j - lo, 0, count - 1))
               for w, lo, count in zip(weights, firsts, tiles)]
    return pl.pallas_call(
        functools.partial(_norm_matmul_kernel, tiles=tiles),
        out_shape=jax.ShapeDtypeStruct((m, n), out_dtype),
        grid=(m // tm, sum(tiles)),
        in_specs=[
            pl.BlockSpec((tm, k), lambda i, j: (i, 0)),
            pl.BlockSpec((1, k), lambda i, j: (0, 0)),
            *w_specs,
        ],
        out_specs=pl.BlockSpec((tm, tn), lambda i, j: (i, j)),
        scratch_shapes=[pltpu.VMEM((tm, k), BF16)],
        compiler_params=_params("arbitrary", "arbitrary"),
        name="norm_matmul",
    )(x, g.reshape(1, k), *weights)


IN_PROJ_ROWS = 1024
IN_PROJ_COLS = 512


def _in_proj_kernel(x_ref, g_ref, w_ref, wdt_ref, o_ref, dt_ref, xn_ref):
    @pl.when(pl.program_id(1) == 0)
    def _():
        x = x_ref[...]
        xn = (x * _rms_scale(x) * g_ref[...]).astype(BF16)
        xn_ref[...] = xn
        head = lax.broadcasted_iota(jnp.int32, dt_ref.shape, 1)
        dt_ref[...] = jnp.where(head < SSM_HEADS, _dot_nt(xn, wdt_ref[...].astype(BF16)), 0.0)

    o_ref[...] = _dot_nt(xn_ref[...], w_ref[...].astype(BF16))


def _in_proj(x, g, w_in):
    m, k = x.shape
    w_in_t = jnp.swapaxes(w_in, 1, 2)
    tm, tn = IN_PROJ_ROWS, IN_PROJ_COLS
    n_main = D_INNER + CONV_DIM
    return pl.pallas_call(
        _in_proj_kernel,
        out_shape=(jax.ShapeDtypeStruct((m, n_main), F32), jax.ShapeDtypeStruct((m, LANES), F32)),
        grid=(m // tm, n_main // tn),
        in_specs=[
            pl.BlockSpec((tm, k), lambda i, j: (i, 0)),
            pl.BlockSpec((1, k), lambda i, j: (0, 0)),
            pl.BlockSpec((None, tn, k), lambda i, j: (0, j, 0)),
            pl.BlockSpec((None, LANES, k), lambda i, j: (0, DT_COL_BLOCK, 0)),
        ],
        out_specs=(pl.BlockSpec((tm, tn), lambda i, j: (i, j)),
                   pl.BlockSpec((tm, LANES), lambda i, j: (i, 0))),
        scratch_shapes=[pltpu.VMEM((tm, k), BF16)],
        compiler_params=_params("arbitrary", "arbitrary"),
        name="in_proj",
    )(x, g.reshape(1, k), w_in_t, w_in_t)


def _matmul_res_kernel(x_ref, w_ref, r_ref, o_ref):
    o_ref[...] = r_ref[...] + _dot(x_ref[...], w_ref[...])


def _matmul_res(x, w, res, *, tm, tn):
    m, k = x.shape
    n = w.shape[-1]
    return pl.pallas_call(
        _matmul_res_kernel,
        out_shape=jax.ShapeDtypeStruct((m, n), F32),
        grid=(m // tm, n // tn),
        in_specs=[
            pl.BlockSpec((tm, k), lambda i, j: (i, 0)),
            _weight_spec(w, k, tn, lambda i, j: j),
            pl.BlockSpec((tm, tn), lambda i, j: (i, j)),
        ],
        out_specs=pl.BlockSpec((tm, tn), lambda i, j: (i, j)),
        compiler_params=_params("arbitrary", "arbitrary"),
        name="matmul_res",
    )(x, w, res)


def _ssd_kernel(z_ref, x_ref, b_ref, c_ref, dt_ref,
                wx_ref, wb_ref, wc_ref, bx_ref, bb_ref, bc_ref,
                dtb_ref, alog_ref, dskip_ref, nw_ref, expand_ref,
                o_ref,
                h_ref, xbuf, bbuf, cbuf):
    L = SSD_CHUNK
    gw = GROUP_WIDTH
    ns = SSM_STATE
    halo = SUBLANES
    groups = range(SSD_GROUPS_PER_STEP)

    @pl.when(pl.program_id(2) == 0)
    def _():
        h_ref[...] = jnp.zeros_like(h_ref)
        xbuf[0:halo, :] = jnp.zeros((halo, xbuf.shape[1]), F32)
        bbuf[0:halo, :] = jnp.zeros((halo, bbuf.shape[1]), F32)
        cbuf[0:halo, :] = jnp.zeros((halo, cbuf.shape[1]), F32)

    def conv_silu(raw_ref, buf, w_ref, bias_ref):
        raw = raw_ref[...]
        buf[halo:halo + L, :] = raw
        padded = buf[...]
        acc = bias_ref[...] + w_ref[CONV_WIDTH - 1:CONV_WIDTH, :] * raw
        for back in range(1, CONV_WIDTH):
            tap = pltpu.roll(padded, back, axis=0)[halo:halo + L, :]
            k = CONV_WIDTH - 1 - back
            acc = acc + w_ref[k:k + 1, :] * tap
        buf[0:halo, :] = raw[L - halo:L, :]
        return _silu(acc)

    bs = conv_silu(b_ref, bbuf, wb_ref, bb_ref)
    cs = conv_silu(c_ref, cbuf, wc_ref, bc_ref)
    bs_b = [bs[:, g * ns:(g + 1) * ns].astype(BF16) for g in groups]
    cs_b = [cs[:, g * ns:(g + 1) * ns].astype(BF16) for g in groups]
    cb = [_dot_nt(cs_b[g], bs_b[g]) for g in groups]
    h_prev = [h_ref[:, g * gw:(g + 1) * gw] for g in groups]
    y_off = [_dot(cs_b[g], h_prev[g].astype(BF16)) for g in groups]

    dt_heads = _softplus(dt_ref[...] + dtb_ref[...])
    dt_parts = _split3(dt_heads)
    a_head_parts = _split2(dt_heads * (-jnp.exp(alog_ref[...])))
    dt = [sum(_dot(p, expand_ref[g]) for p in dt_parts) for g in groups]
    a_parts = [[_dot(p, expand_ref[g]).astype(BF16) for p in a_head_parts] for g in groups]

    xs = conv_silu(x_ref, xbuf, wx_ref, bx_ref)

    row = lax.broadcasted_iota(jnp.int32, (L, L), 0)
    col = lax.broadcasted_iota(jnp.int32, (L, L), 1)
    causal = row >= col
    tril = causal.astype(BF16)
    triu = (row <= col).astype(BF16)
    hrow = lax.broadcasted_iota(jnp.int32, (SUBLANES, gw), 0)
    hcol = lax.broadcasted_iota(jnp.int32, (SUBLANES, gw), 1)
    pick = (hcol == hrow * SSM_HEAD_DIM).astype(BF16)

    acs =[sum(_dot(tril, p) for p in a_parts[g]) for g in groups]
    a_rows = [[_dot_nt(pick, p).astype(BF16) for p in a_parts[g]] for g in groups]
    acs_rows = [sum(_dot(p, triu) for p in a_rows[g]) for g in groups]

    xdt = [xs[:, g * gw:(g + 1) * gw] * dt[g] for g in groups]
    xdt_b = [v.astype(BF16) for v in xdt]
    y_diag = []
    for g in groups:
        parts = []
        for r in range(SSM_HEADS_PER_GROUP):
            lo = r * SSM_HEAD_DIM
            seg = acs[g][:, lo:lo + 1] - acs_rows[g][r:r + 1, :]
            decay = jnp.exp(jnp.where(causal, seg, -jnp.inf))
            m = (cb[g] * decay).astype(BF16)
            parts.append(_dot(m, xdt_b[g][:, lo:lo + SSM_HEAD_DIM]))
        y_diag.append(jnp.concatenate(parts, axis=1))

    a_last = [acs[g][L - 1:L, :] for g in groups]
    xd = [(xdt[g] * jnp.exp(a_last[g] - acs[g])).astype(BF16) for g in groups]
    states = [_dot(bs[:, g * ns:(g + 1) * ns].T.astype(BF16), xd[g]) for g in groups]
    for g in groups:
        lanes = slice(g * gw, (g + 1) * gw)
        h_ref[:, lanes] = h_prev[g] * jnp.exp(a_last[g]) + states[g]
        y = y_diag[g] + y_off[g] * jnp.exp(acs[g]) + xs[:, lanes] * dskip_ref[:, lanes]
        yz = y * _silu(z_ref[:, lanes])
        o_ref[:, lanes] = (yz * _rms_scale(yz) * nw_ref[:, lanes]).astype(o_ref.dtype)


def _ssd(zx, dt, conv_w, conv_b, dtb_e, alog_e, dskip_e, norm_w, *, batch, seq):
    L = SSD_CHUNK
    nc = seq // L
    xw = SSD_GROUPS_PER_STEP * GROUP_WIDTH
    sw = SSD_GROUPS_PER_STEP * SSM_STATE
    x_blk = D_INNER // xw
    b_blk = (2 * D_INNER) // sw
    c_blk = b_blk + SSM_GROUPS * SSM_STATE // sw
    cw_b = D_INNER // sw
    cw_c = cw_b + SSM_GROUPS * SSM_STATE // sw

    def rows(b, g, c):
        return b * nc + c

    in_specs = [
        pl.BlockSpec((L, xw), lambda b, g, c: (rows(b, g, c), g)),
        pl.BlockSpec((L, xw), lambda b, g, c: (rows(b, g, c), x_blk + g)),
        pl.BlockSpec((L, sw), lambda b, g, c: (rows(b, g, c), b_blk + g)),
        pl.BlockSpec((L, sw), lambda b, g, c: (rows(b, g, c), c_blk + g)),
        pl.BlockSpec((L, LANES), lambda b, g, c: (rows(b, g, c), 0)),
        pl.BlockSpec((CONV_WIDTH, xw), lambda b, g, c: (0, g)),
        pl.BlockSpec((CONV_WIDTH, sw), lambda b, g, c: (0, cw_b + g)),
        pl.BlockSpec((CONV_WIDTH, sw), lambda b, g, c: (0, cw_c + g)),
        pl.BlockSpec((1, xw), lambda b, g, c: (0, g)),
        pl.BlockSpec((1, sw), lambda b, g, c: (0, cw_b + g)),
        pl.BlockSpec((1, sw), lambda b, g, c: (0, cw_c + g)),
        pl.BlockSpec((1, LANES), lambda b, g, c: (0, 0)),
        pl.BlockSpec((1, LANES), lambda b, g, c: (0, 0)),
        pl.BlockSpec((1, xw), lambda b, g, c: (0, g)),
        pl.BlockSpec((1, xw), lambda b, g, c: (0, g)),
        pl.BlockSpec((SSD_GROUPS_PER_STEP, LANES, GROUP_WIDTH), lambda b, g, c: (g, 0, 0)),
    ]
    head_of_lane = (jnp.arange(SSM_GROUPS)[:, None, None] * SSM_HEADS_PER_GROUP
                    + jnp.arange(GROUP_WIDTH)[None, None, :] // SSM_HEAD_DIM)
    expand = (jnp.arange(LANES)[None, :, None] == head_of_lane).astype(BF16)
    cb2 = conv_b.reshape(1, CONV_DIM)
    return pl.pallas_call(
        _ssd_kernel,
        out_shape=jax.ShapeDtypeStruct((batch * seq, D_INNER), BF16),
        grid=(batch, SSM_GROUPS // SSD_GROUPS_PER_STEP, nc),
        in_specs=in_specs,
        out_specs=pl.BlockSpec((L, xw), lambda b, g, c: (rows(b, g, c), g)),
        scratch_shapes=[
            pltpu.VMEM((SSM_STATE, xw), F32),
            pltpu.VMEM((L + SUBLANES, xw), F32),
            pltpu.VMEM((L + SUBLANES, sw), F32),
            pltpu.VMEM((L + SUBLANES, sw), F32),
        ],
        compiler_params=_params("arbitrary", "arbitrary", "arbitrary"),
        name="ssd",
    )(zx, zx, zx, zx, dt, conv_w, conv_w, conv_w, cb2, cb2, cb2,
      dtb_e, alog_e, dskip_e, norm_w.reshape(1, D_INNER), expand)


ATT_HEADS_PER_STEP = 8


def _attn_kernel(q_ref, k_ref, v_ref, o_ref):
    tq = ATT_BLOCK
    dh = SB_HEAD_DIM
    i = pl.program_id(2)
    scale = dh ** -0.5
    row = lax.broadcasted_iota(jnp.int32, (tq, tq), 0)
    col = lax.broadcasted_iota(jnp.int32, (tq, tq), 1)
    strict = col < row
    row2 = lax.broadcasted_iota(jnp.int32, (2 * tq, tq), 0) & (tq - 1)
    col2 = lax.broadcasted_iota(jnp.int32, (2 * tq, tq), 1)
    neg_later = jnp.where(row2 > col2, -1.0, 0.0).astype(BF16)

    heads = range(ATT_HEADS_PER_STEP)

    def block(j, state, diagonal):
        start = pl.multiple_of(j * tq, tq)
        cols = [slice(h * dh, (h + 1) * dh) for h in heads]
        z = [_dot_nt(q_ref[:, cols[h]], k_ref[pl.ds(start, tq), cols[h]]) * scale for h in heads]
        drop, logit, parts = [], [], []
        for h in heads:
            neg_abs = lax.bitcast_convert_type(
                lax.bitcast_convert_type(z[h], jnp.uint32) | jnp.uint32(0x80000000), F32)
            sp = jnp.maximum(z[h], 0.0) + jnp.log(1.0 + jnp.exp(neg_abs))
            logit.append(z[h] - sp)
            if diagonal:
                sp = jnp.where(strict, sp, 0.0)
            hi, lo = _split2(sp)
            parts.append(jnp.concatenate([hi, lo], axis=1))
            drop.append(sp)
        suffix = [_dot(parts[h], neg_later) for h in heads]
        weights = []
        for h in heads:
            w = jnp.exp(logit[h] + suffix[h] + state[h][0])
            if diagonal:
                w = jnp.where(strict, w, 0.0)
            weights.append(w.astype(BF16))
        out = [_dot(weights[h], v_ref[pl.ds(start, tq), cols[h]]) for h in heads]
        return tuple((state[h][0] - jnp.sum(drop[h], axis=1, keepdims=True), state[h][1] + out[h])
                     for h in heads)

    zero = (jnp.zeros((tq, 1), F32), jnp.zeros((tq, dh), F32))
    state = block(i, (zero,) * ATT_HEADS_PER_STEP, True)
    state = lax.fori_loop(0, i, lambda t, s: block(i - 1 - t, s, False), state)
    for h in range(ATT_HEADS_PER_STEP):
        o_ref[:, h * dh:(h + 1) * dh] = state[h][1].astype(o_ref.dtype)


def _attention(q, kv, *, batch, seq):
    tq = ATT_BLOCK
    nq = seq // tq
    width = ATT_HEADS_PER_STEP * SB_HEAD_DIM
    n_steps = SB_HEADS // ATT_HEADS_PER_STEP
    return pl.pallas_call(
        _attn_kernel,
        out_shape=jax.ShapeDtypeStruct((batch * seq, D_MODEL), BF16),
        grid=(batch, n_steps, nq),
        in_specs=[
            pl.BlockSpec((tq, width), lambda b, h, i: (b * nq + i, h)),
            pl.BlockSpec((seq, width), lambda b, h, i: (b, h)),
            pl.BlockSpec((seq, width), lambda b, h, i: (b, n_steps + h)),
        ],
        out_specs=pl.BlockSpec((tq, width), lambda b, h, i: (b * nq + i, h)),
        compiler_params=_params("arbitrary", "arbitrary", "arbitrary"),
        name="sb_attention",
    )(q, kv, kv)


ROUTER_FINE0 = N_GROUPS


ROUTE_E1, ROUTE_E2, ROUTE_W1, ROUTE_W2, ROUTE_R1, ROUTE_R2 = range(6)


def _router_kernel(h_ref, g_ref, w_ref, b_ref, hn_ref, route_ref, counts_ref, carry_ref):
    @pl.when(pl.program_id(0) == 0)
    def _():
        carry_ref[...] = jnp.zeros_like(carry_ref)

    x = h_ref[...]
    hn = x * _rms_scale(x) * g_ref[...]
    hn_ref[...] = hn
    x1 = hn.astype(BF16)
    x2 = (hn - x1.astype(F32)).astype(BF16)
    w = w_ref[...]
    w1 = w.astype(BF16)
    w2 = (w - w1.astype(F32)).astype(BF16)
    logits = _dot(x1, w1) + _dot(x2, w1) + _dot(x1, w2) + b_ref[...]

    lane_i = lax.broadcasted_iota(jnp.int32, logits.shape, 1)
    lane = lane_i.astype(F32)
    neg = -jnp.inf

    def first_argmax(v, vmax):
        return jnp.min(jnp.where(v == vmax, lane, float(LANES)), axis=1, keepdims=True)

    coarse = jnp.where(lane_i < N_GROUPS, logits, neg)
    cmax = jnp.max(coarse, axis=1, keepdims=True)
    p_g = 1.0 / jnp.sum(jnp.exp(coarse - cmax), axis=1, keepdims=True)
    g_idx = first_argmax(coarse, cmax)
    fine_lo = ROUTER_FINE0 + g_idx * EXPERTS_PER_GROUP
    in_group = (lane >= fine_lo) & (lane < fine_lo + EXPERTS_PER_GROUP)
    fine = jnp.where(in_group, logits, neg)
    m1 = jnp.max(fine, axis=1, keepdims=True)
    denom = jnp.sum(jnp.exp(fine - m1), axis=1, keepdims=True)
    i1 = first_argmax(fine, m1)
    rest = jnp.where(lane == i1, neg, fine)
    m2 = jnp.max(rest, axis=1, keepdims=True)
    i2 = first_argmax(rest, m2)
    p1 = 1.0 / denom
    p2 = jnp.exp(m2 - m1) / denom
    norm = p1 + p2
    gate1 = p_g * (p1 / norm)
    gate2 = p_g * (p2 / norm)

    tm = x.shape[0]
    chosen = (lane == i1) | (lane == i2)
    r = lax.broadcasted_iota(jnp.int32, (tm, tm), 0)
    c = lax.broadcasted_iota(jnp.int32, (tm, tm), 1)
    before = _dot((c < r).astype(BF16), chosen.astype(BF16)) + carry_ref[...]
    rank1 = jnp.sum(jnp.where(lane == i1, before, 0.0), axis=1, keepdims=True)
    rank2 = jnp.sum(jnp.where(lane == i2, before, 0.0), axis=1, keepdims=True)
    carry_ref[...] += jnp.sum(chosen.astype(F32), axis=0, keepdims=True)
    counts_ref[...] = carry_ref[...]

    route = jnp.zeros_like(logits)
    for slot, val in ((ROUTE_E1, i1 - ROUTER_FINE0), (ROUTE_E2, i2 - ROUTER_FINE0),
                      (ROUTE_W1, gate1), (ROUTE_W2, gate2), (ROUTE_R1, rank1), (ROUTE_R2, rank2)):
        route = jnp.where(lane_i == slot, val, route)
    route_ref[...] = route


def _router(h, g, w_router, b_router, *, tm):
    m, k = h.shape
    return pl.pallas_call(
        _router_kernel,
        out_shape=(jax.ShapeDtypeStruct((m, k), F32), jax.ShapeDtypeStruct((m, LANES), F32),
                   jax.ShapeDtypeStruct((1, LANES), F32)),
        grid=(m // tm,),
        in_specs=[
            pl.BlockSpec((tm, k), lambda i: (i, 0)),
            pl.BlockSpec((1, k), lambda i: (0, 0)),
            pl.BlockSpec((k, LANES), lambda i: (0, 0)),
            pl.BlockSpec((1, LANES), lambda i: (0, 0)),
        ],
        out_specs=(pl.BlockSpec((tm, k), lambda i: (i, 0)),
                   pl.BlockSpec((tm, LANES), lambda i: (i, 0)),
                   pl.BlockSpec((1, LANES), lambda i: (0, 0))),
        scratch_shapes=[pltpu.VMEM((1, LANES), F32)],
        compiler_params=_params("arbitrary"),
        name="moe_router",
    )(h, g.reshape(1, k), w_router, b_router)


DISPATCH_TOKENS = 1024
DMA_ISSUE_UNROLL = 8


def _row_copy(src, src_row, dst, dst_row, sem):
    return pltpu.make_async_copy(src.at[pl.ds(src_row, 1)], dst.at[pl.ds(dst_row, 1)], sem)


def _dispatch_kernel(pos1_ref, pos2_ref, hn_ref, xs_ref, sem):
    base = pl.program_id(0) * DISPATCH_TOKENS

    def issue(t, _):
        tok = base + t
        _row_copy(hn_ref, t, xs_ref, pos1_ref[tok], sem).start()
        _row_copy(hn_ref, t, xs_ref, pos2_ref[tok], sem).start()
        return 0

    lax.fori_loop(0, DISPATCH_TOKENS, issue, 0, unroll=DMA_ISSUE_UNROLL)

    for _ in range(2):
        pltpu.make_async_copy(hn_ref, xs_ref.at[pl.ds(0, DISPATCH_TOKENS)], sem).wait()


def _dispatch(hn, pos1, pos2):
    m, k = hn.shape
    return pl.pallas_call(
        _dispatch_kernel,
        out_shape=jax.ShapeDtypeStruct((2 * m, k), F32),
        grid_spec=pltpu.PrefetchScalarGridSpec(
            num_scalar_prefetch=2,
            grid=(m // DISPATCH_TOKENS,),
            in_specs=[pl.BlockSpec((DISPATCH_TOKENS, k), lambda i, *_: (i, 0))],
            out_specs=pl.BlockSpec(memory_space=pl.ANY),
            scratch_shapes=[pltpu.SemaphoreType.DMA],
        ),
        compiler_params=_params("arbitrary"),
        name="moe_dispatch",
    )(pos1, pos2, hn)


EXPERT_TILE = 256
EXPERT_HIDDEN_SLICES = 2


def _experts_kernel(tile_ref, expert_ref, lo_ref, hi_ref, first_ref, valid_ref,
                    x_ref, wg_ref, wu_ref, wd_ref, o_ref):
    del tile_ref, expert_ref
    n = pl.program_id(0)

    @pl.when(valid_ref[n] == 1)
    def _():
        x = x_ref[...].astype(BF16)
        rows = lax.broadcasted_iota(jnp.int32, (EXPERT_TILE, 1), 0)
        inside = (rows >= lo_ref[n]) & (rows < hi_ref[n])
        f = wg_ref.shape[-1]
        fs = f // EXPERT_HIDDEN_SLICES
        cols = [slice(s * fs, (s + 1) * fs) for s in range(EXPERT_HIDDEN_SLICES)]
        pre = [(_dot(x, wg_ref[0, 0, :, c].astype(BF16)), _dot(x, wu_ref[0, 0, :, c].astype(BF16)))
               for c in cols]
        hid = [jnp.where(inside, _silu(g) * u, 0.0).astype(BF16) for g, u in pre]
        y = sum(_dot(hid[s], wd_ref[0, 0, c, :].astype(BF16)) for s, c in enumerate(cols))

        @pl.when(first_ref[n] == 1)
        def _():
            o_ref[...] = y

        @pl.when(first_ref[n] == 0)
        def _():
            o_ref[...] += y


def _experts(work, xs, w_gate, w_up, w_down, layer):
    rows, k = xs.shape
    f = w_gate.shape[-1]
    n_items = work[0].shape[0]
    return pl.pallas_call(
        _experts_kernel,
        out_shape=jax.ShapeDtypeStruct((rows, k), F32),
        grid_spec=pltpu.PrefetchScalarGridSpec(
            num_scalar_prefetch=6,
            grid=(n_items,),
            in_specs=[
                pl.BlockSpec((EXPERT_TILE, k), lambda n, tile, *_: (tile[n], 0)),
                pl.BlockSpec((1, 1, k, f), lambda n, tile, expert, *_: (layer, expert[n], 0, 0)),
                pl.BlockSpec((1, 1, k, f), lambda n, tile, expert, *_: (layer, expert[n], 0, 0)),
                pl.BlockSpec((1, 1, f, k), lambda n, tile, expert, *_: (layer, expert[n], 0, 0)),
            ],
            out_specs=pl.BlockSpec((EXPERT_TILE, k), lambda n, tile, *_: (tile[n], 0)),
        ),
        compiler_params=_params("arbitrary"),
        name="moe_experts",
    )(*work, xs, w_gate, w_up, w_down)


COMBINE_TOKENS = 512


def _combine_kernel(pos1_ref, pos2_ref, h_ref, route_ref, ys_ref, *rest, final_norm):
    if final_norm:
        g_ref, o_ref, buf1, buf2, sem = rest
    else:
        o_ref, buf1, buf2, sem = rest
    base = pl.program_id(0) * COMBINE_TOKENS

    def issue(t, _):
        tok = base + t
        _row_copy(ys_ref, pos1_ref[tok], buf1, t, sem).start()
        _row_copy(ys_ref, pos2_ref[tok], buf2, t, sem).start()
        return 0

    lax.fori_loop(0, COMBINE_TOKENS, issue, 0, unroll=DMA_ISSUE_UNROLL)

    pltpu.make_async_copy(ys_ref.at[pl.ds(0, COMBINE_TOKENS)], buf1, sem).wait()
    pltpu.make_async_copy(ys_ref.at[pl.ds(0, COMBINE_TOKENS)], buf2, sem).wait()
    route = route_ref[...]
    w1 = route[:, ROUTE_W1:ROUTE_W1 + 1]
    w2 = route[:, ROUTE_W2:ROUTE_W2 + 1]
    out = h_ref[...] + w1 * buf1[...] + w2 * buf2[...]
    if final_norm:
        out = out * _rms_scale(out) * g_ref[...]
    o_ref[...] = out


def _combine(h, route, ys, pos1, pos2, final_gain=None):
    m, k = h.shape
    tm = COMBINE_TOKENS
    final_norm = final_gain is not None
    in_specs = [
        pl.BlockSpec((tm, k), lambda i, *_: (i, 0)),
        pl.BlockSpec((tm, LANES), lambda i, *_: (i, 0)),
        pl.BlockSpec(memory_space=pl.ANY),
    ]
    operands = [h, route, ys]
    if final_norm:
        in_specs.append(pl.BlockSpec((1, k), lambda i, *_: (0, 0)))
        operands.append(final_gain.reshape(1, k))
    return pl.pallas_call(
        functools.partial(_combine_kernel, final_norm=final_norm),
        out_shape=jax.ShapeDtypeStruct((m, k), F32),
        grid_spec=pltpu.PrefetchScalarGridSpec(
            num_scalar_prefetch=2,
            grid=(m // tm,),
            in_specs=in_specs,
            out_specs=pl.BlockSpec((tm, k), lambda i, *_: (i, 0)),
            scratch_shapes=[pltpu.VMEM((tm, k), F32), pltpu.VMEM((tm, k), F32),
                            pltpu.SemaphoreType.DMA],
        ),
        compiler_params=_params("arbitrary"),
        name="moe_combine",
    )(pos1, pos2, *operands)


def _work_list(counts, n_rows):
    tm = EXPERT_TILE
    n_tiles = n_rows // tm
    n_items = n_tiles + N_EXPERTS - 1
    i32 = jnp.int32
    count_le = lambda sorted_vals, q: jnp.sum((sorted_vals[None, :] <= q[:, None]).astype(i32), axis=1)
    pick = lambda table, idx: jnp.sum(
        jnp.where(idx[:, None] == jnp.arange(table.shape[0], dtype=i32)[None, :], table[None, :], 0), axis=1)
    e_ids = jnp.arange(N_EXPERTS, dtype=i32)
    ends = jnp.sum(jnp.where(e_ids[None, :] <= e_ids[:, None], counts[None, :], 0), axis=1)
    starts = ends - counts
    tile_ids = jnp.arange(n_tiles, dtype=i32)
    tile_lo = tile_ids * tm
    e_first = count_le(ends, tile_lo)
    e_last = count_le(ends, tile_lo + (tm - 1))
    per_tile = e_last - e_first + 1
    item0 = jnp.sum(jnp.where(tile_ids[None, :] < tile_ids[:, None], per_tile[None, :], 0), axis=1)
    total = jnp.sum(per_tile)
    n = jnp.arange(n_items, dtype=i32)
    valid = n < total
    tile = jnp.where(valid, count_le(item0, n) - 1, n_tiles - 1)
    expert = jnp.where(valid, pick(e_first, tile) + (n - pick(item0, tile)), e_last[-1])
    expert = jnp.clip(expert, 0, N_EXPERTS - 1)
    lo = jnp.clip(pick(starts, expert) - tile * tm, 0, tm)
    hi = jnp.clip(pick(ends, expert) - tile * tm, 0, tm)
    first = (n == pick(item0, tile)) & valid
    as_i32 = lambda a: a.astype(i32)
    return tuple(map(as_i32, (tile, expert, lo, hi, first, valid))), starts


def _router_weights(w_coarse, b_coarse, w_fine, b_fine):
    pad = LANES - N_GROUPS - N_EXPERTS
    w = jnp.concatenate([w_coarse, w_fine.reshape(D_MODEL, N_EXPERTS),
                         jnp.zeros((D_MODEL, pad), F32)], axis=1)
    b = jnp.concatenate([b_coarse, b_fine.reshape(N_EXPERTS), jnp.zeros((pad,), F32)])
    return w, b.reshape(1, LANES)


def _moe(h, norm_g, w_coarse, b_coarse, w_fine, b_fine, w_gate, w_up, w_down, layer, final_gain=None):
    w_r, b_r = _router_weights(w_coarse, b_coarse, w_fine, b_fine)
    hn, route, counts = _router(h, norm_g, w_r, b_r, tm=512)
    counts = counts[0, ROUTER_FINE0:ROUTER_FINE0 + N_EXPERTS].astype(jnp.int32)
    work, starts = _work_list(counts, 2 * h.shape[0])
    ids = route[:, :ROUTE_R2 + 1].astype(jnp.int32)
    e_ids = jnp.arange(N_EXPERTS, dtype=jnp.int32)[None, :]
    start_of = lambda e: jnp.sum(jnp.where(e[:, None] == e_ids, starts[None, :], 0), axis=1)
    pos1 = start_of(ids[:, ROUTE_E1]) + ids[:, ROUTE_R1]
    pos2 = start_of(ids[:, ROUTE_E2]) + ids[:, ROUTE_R2]
    xs = _dispatch(hn, pos1, pos2)
    ys = _experts(work, xs, w_gate, w_up, w_down, layer)
    return _combine(h, route, ys, pos1, pos2, final_gain)


def kernel(x, mix_norm, ffn_norm, ssm_w_in, ssm_conv_w, ssm_conv_b, ssm_dt_bias, ssm_a_log, ssm_d,
           ssm_norm_w, ssm_w_out, kv_norm, w_k, w_v, sb_w_q, sb_w_out, moe_w_coarse, moe_b_coarse,
           moe_w_fine, moe_b_fine, moe_w_gate, moe_w_up, moe_w_down, final_norm):
    batch, seq, d = x.shape
    h = x.reshape(batch * seq, d)

    zx, dt = _in_proj(h, mix_norm[0], ssm_w_in)
    per_head = lambda p: jnp.pad(p, (0, LANES - SSM_HEADS)).reshape(1, LANES)
    d_skip = jnp.repeat(ssm_d[0], SSM_HEAD_DIM).reshape(1, D_INNER)
    yn = _ssd(zx, dt, ssm_conv_w[0], ssm_conv_b[0], per_head(ssm_dt_bias[0]), per_head(ssm_a_log[0]),
              d_skip, ssm_norm_w[0], batch=batch, seq=seq)
    h = _matmul_res(yn, ssm_w_out.astype(BF16), h, tm=1024, tn=512)
    h = _moe(h, ffn_norm[0], moe_w_coarse[0], moe_b_coarse[0], moe_w_fine[0], moe_b_fine[0],
             moe_w_gate, moe_w_up, moe_w_down, 0)

    w_kv = jnp.concatenate([w_k.astype(BF16), w_v.astype(BF16)], axis=1)
    kv = _norm_matmul(h, kv_norm, [w_kv], tm=1024, tn=1024, out_dtype=BF16)

    q = _norm_matmul(h, mix_norm[1], [sb_w_q.astype(BF16)], tm=1024, tn=1024, out_dtype=BF16)
    o = _attention(q, kv, batch=batch, seq=seq)
    h = _matmul_res(o, sb_w_out.astype(BF16), h, tm=1024, tn=512)
    h = _moe(h, ffn_norm[1], moe_w_coarse[1], moe_b_coarse[1], moe_w_fine[1], moe_b_fine[1],
             moe_w_gate, moe_w_up, moe_w_down, 1, final_gain=final_norm)
    return h.reshape(batch, seq, d)
```

```python
import functools

import jax
import jax.numpy as jnp
from jax import lax
from jax.experimental import pallas as pl
from jax.experimental.pallas import tpu as pltpu

F32 = jnp.float32
BF16 = jnp.bfloat16
EPS = 1e-5

D_MODEL = 2048
D_INNER = 4096
SSM_HEAD_DIM = 64
SSM_HEADS = 64
SSM_GROUPS = 8
SSM_HEADS_PER_GROUP = 8
SSM_STATE = 128
GROUP_WIDTH = D_INNER // SSM_GROUPS
CONV_WIDTH = 4
CONV_DIM = D_INNER + 2 * SSM_GROUPS * SSM_STATE
IN_PROJ_DIM = D_INNER + CONV_DIM + SSM_HEADS
SB_HEADS = 16
SB_HEAD_DIM = 128
N_GROUPS = 4
EXPERTS_PER_GROUP = 4
N_EXPERTS = 16
D_EXPERT = 512

LANES = 128
SUBLANES = 8
DT_COL_BLOCK = (D_INNER + CONV_DIM) // LANES
SSD_CHUNK = 128
SSD_GROUPS_PER_STEP = 8
ATT_BLOCK = 256
VMEM_LIMIT = 56 * 1024 * 1024


def _dot(a, b):
    return jnp.dot(a, b, preferred_element_type=F32)


def _dot_nt(a, b):
    return lax.dot_general(a, b, (((1,), (1,)), ((), ())), preferred_element_type=F32)


def _split3(v):
    v1 = v.astype(BF16)
    r = v - v1.astype(F32)
    v2 = r.astype(BF16)
    r = r - v2.astype(F32)
    return v1, v2, r.astype(BF16)


def _split2(v):
    hi = v.astype(BF16)
    return hi, (v - hi.astype(F32)).astype(BF16)


def _dot3_left(m, v):
    v1, v2, v3 = _split3(v)
    return _dot(m, v1) + _dot(m, v2) + _dot(m, v3)


def _dot3_right(v, m):
    v1, v2, v3 = _split3(v)
    return _dot(v1, m) + _dot(v2, m) + _dot(v3, m)


def _softplus(x):
    return jnp.maximum(x, 0.0) + jnp.log(1.0 + jnp.exp(-jnp.abs(x)))


def _silu(x):
    half = 0.5 * x
    return half + half * jnp.tanh(half)


def _rms_scale(x):
    return lax.rsqrt(jnp.mean(x * x, axis=-1, keepdims=True) + EPS)


def _params(*sem):
    return pltpu.CompilerParams(dimension_semantics=sem, vmem_limit_bytes=VMEM_LIMIT)


def _weight_spec(w, k, tn, col_tile):
    if w.ndim == 2:
        return pl.BlockSpec((k, tn), lambda i, j: (0, col_tile(i, j)))
    return pl.BlockSpec((None, k, tn), lambda i, j: (0, 0, col_tile(i, j)))


def _norm_matmul_kernel(x_ref, g_ref, *rest, tiles):
    w_refs, (o_ref, xn_ref) = rest[:len(tiles)], rest[len(tiles):]
    j = pl.program_id(1)

    @pl.when(j == 0)
    def _():
        x = x_ref[...]
        xn_ref[...] = (x * _rms_scale(x) * g_ref[...]).astype(BF16)

    first = 0
    for w_ref, count in zip(w_refs, tiles):
        @pl.when((j >= first) & (j < first + count))
        def _(w_ref=w_ref):
            o_ref[...] = _dot(xn_ref[...], w_ref[...]).astype(o_ref.dtype)
        first += count


def _norm_matmul(x, g, weights, *, tm, tn, out_dtype):
    m, k = x.shape
    assert all(w.shape[-1] % tn == 0 for w in weights)
    tiles = tuple(w.shape[-1] // tn for w in weights)
    n = sum(w.shape[-1] for w in weights)
    firsts = [sum(tiles[:a]) for a in range(len(tiles))]
    w_specs = [_weight_spec(w, k, tn, lambda i, j, lo=lo, count=count: jnp.clip(j - lo, 0, count - 1))
               for w, lo, count in zip(weights, firsts, tiles)]
    return pl.pallas_call(
        functools.partial(_norm_matmul_kernel, tiles=tiles),
        out_shape=jax.ShapeDtypeStruct((m, n), out_dtype),
        grid=(m // tm, sum(tiles)),
        in_specs=[
            pl.BlockSpec((tm, k), lambda i, j: (i, 0)),
            pl.BlockSpec((1, k), lambda i, j: (0, 0)),
            *w_specs,
        ],
        out_specs=pl.BlockSpec((tm, tn), lambda i, j: (i, j)),
        scratch_shapes=[pltpu.VMEM((tm, k), BF16)],
        compiler_params=_params("arbitrary", "arbitrary"),
        name="norm_matmul",
    )(x, g.reshape(1, k), *weights)


IN_PROJ_ROWS = 1024
IN_PROJ_COLS = 512


def _in_proj_kernel(x_ref, g_ref, w_ref, wdt_ref, o_ref, dt_ref, xn_ref):
    @pl.when(pl.program_id(1) == 0)
    def _():
        x = x_ref[...]
        xn = (x * _rms_scale(x) * g_ref[...]).astype(BF16)
        xn_ref[...] = xn
        head = lax.broadcasted_iota(jnp.int32, dt_ref.shape, 1)
        dt_ref[...] = jnp.where(head < SSM_HEADS, _dot_nt(xn, wdt_ref[...].astype(BF16)), 0.0)

    o_ref[...] = _dot_nt(xn_ref[...], w_ref[...].astype(BF16))


def _in_proj(x, g, w_in):
    m, k = x.shape
    w_in_t = jnp.swapaxes(w_in, 1, 2)
    tm, tn = IN_PROJ_ROWS, IN_PROJ_COLS
    n_main = D_INNER + CONV_DIM
    return pl.pallas_call(
        _in_proj_kernel,
        out_shape=(jax.ShapeDtypeStruct((m, n_main), F32), jax.ShapeDtypeStruct((m, LANES), F32)),
        grid=(m // tm, n_main // tn),
        in_specs=[
            pl.BlockSpec((tm, k), lambda i, j: (i, 0)),
            pl.BlockSpec((1, k), lambda i, j: (0, 0)),
            pl.BlockSpec((None, tn, k), lambda i, j: (0, j, 0)),
            pl.BlockSpec((None, LANES, k), lambda i, j: (0, DT_COL_BLOCK, 0)),
        ],
        out_specs=(pl.BlockSpec((tm, tn), lambda i, j: (i, j)),
                   pl.BlockSpec((tm, LANES), lambda i, j: (i, 0))),
        scratch_shapes=[pltpu.VMEM((tm, k), BF16)],
        compiler_params=_params("arbitrary", "arbitrary"),
        name="in_proj",
    )(x, g.reshape(1, k), w_in_t, w_in_t)


def _matmul_res_kernel(x_ref, w_ref, r_ref, o_ref):
    o_ref[...] = r_ref[...] + _dot(x_ref[...], w_ref[...])


def _matmul_res(x, w, res, *, tm, tn):
    m, k = x.shape
    n = w.shape[-1]
    return pl.pallas_call(
        _matmul_res_kernel,
        out_shape=jax.ShapeDtypeStruct((m, n), F32),
        grid=(m // tm, n // tn),
        in_specs=[
            pl.BlockSpec((tm, k), lambda i, j: (i, 0)),
            _weight_spec(w, k, tn, lambda i, j: j),
            pl.BlockSpec((tm, tn), lambda i, j: (i, j)),
        ],
        out_specs=pl.BlockSpec((tm, tn), lambda i, j: (i, j)),
        compiler_params=_params("arbitrary", "arbitrary"),
        name="matmul_res",
    )(x, w, res)


def _ssd_kernel(z_ref, x_ref, b_ref, c_ref, dt_ref,
                wx_ref, wb_ref, wc_ref, bx_ref, bb_ref, bc_ref,
                dtb_ref, alog_ref, dskip_ref, nw_ref, expand_ref,
                o_ref,
                h_ref, xbuf, bbuf, cbuf):
    L = SSD_CHUNK
    gw = GROUP_WIDTH
    ns = SSM_STATE
    halo = SUBLANES
    groups = range(SSD_GROUPS_PER_STEP)

    @pl.when(pl.program_id(2) == 0)
    def _():
        h_ref[...] = jnp.zeros_like(h_ref)
        xbuf[0:halo, :] = jnp.zeros((halo, xbuf.shape[1]), F32)
        bbuf[0:halo, :] = jnp.zeros((halo, bbuf.shape[1]), F32)
        cbuf[0:halo, :] = jnp.zeros((halo, cbuf.shape[1]), F32)

    def conv_silu(raw_ref, buf, w_ref, bias_ref):
        raw = raw_ref[...]
        buf[halo:halo + L, :] = raw
        padded = buf[...]
        acc = bias_ref[...] + w_ref[CONV_WIDTH - 1:CONV_WIDTH, :] * raw
        for back in range(1, CONV_WIDTH):
            tap = pltpu.roll(padded, back, axis=0)[halo:halo + L, :]
            k = CONV_WIDTH - 1 - back
            acc = acc + w_ref[k:k + 1, :] * tap
        buf[0:halo, :] = raw[L - halo:L, :]
        return _silu(acc)

    bs = conv_silu(b_ref, bbuf, wb_ref, bb_ref)
    cs = conv_silu(c_ref, cbuf, wc_ref, bc_ref)
    bs_b = [bs[:, g * ns:(g + 1) * ns].astype(BF16) for g in groups]
    cs_b = [cs[:, g * ns:(g + 1) * ns].astype(BF16) for g in groups]
    cb = [_dot_nt(cs_b[g], bs_b[g]) for g in groups]
    h_prev = [h_ref[:, g * gw:(g + 1) * gw] for g in groups]
    y_off = [_dot(cs_b[g], h_prev[g].astype(BF16)) for g in groups]

    dt_heads = _softplus(dt_ref[...] + dtb_ref[...])
    dt_parts = _split3(dt_heads)
    a_head_parts = _split2(dt_heads * (-jnp.exp(alog_ref[...])))
    dt = [sum(_dot(p, expand_ref[g]) for p in dt_parts) for g in groups]
    a_parts = [[_dot(p, expand_ref[g]).astype(BF16) for p in a_head_parts] for g in groups]

    xs = conv_silu(x_ref, xbuf, wx_ref, bx_ref)

    row = lax.broadcasted_iota(jnp.int32, (L, L), 0)
    col = lax.broadcasted_iota(jnp.int32, (L, L), 1)
    causal = row >= col
    tril = causal.astype(BF16)
    triu = (row <= col).astype(BF16)
    hrow = lax.broadcasted_iota(jnp.int32, (SUBLANES, gw), 0)
    hcol = lax.broadcasted_iota(jnp.int32, (SUBLANES, gw), 1)
    pick = (hcol == hrow * SSM_HEAD_DIM).astype(BF16)

    acs =[sum(_dot(tril, p) for p in a_parts[g]) for g in groups]
    a_rows = [[_dot_nt(pick, p).astype(BF16) for p in a_parts[g]] for g in groups]
    acs_rows = [sum(_dot(p, triu) for p in a_rows[g]) for g in groups]

    xdt = [xs[:, g * gw:(g + 1) * gw] * dt[g] for g in groups]
    xdt_b = [v.astype(BF16) for v in xdt]
    y_diag = []
    for g in groups:
        parts = []
        for r in range(SSM_HEADS_PER_GROUP):
            lo = r * SSM_HEAD_DIM
            seg = acs[g][:, lo:lo + 1] - acs_rows[g][r:r + 1, :]
            decay = jnp.exp(jnp.where(causal, seg, -jnp.inf))
            m = (cb[g] * decay).astype(BF16)
            parts.append(_dot(m, xdt_b[g][:, lo:lo + SSM_HEAD_DIM]))
        y_diag.append(jnp.concatenate(parts, axis=1))

    a_last = [acs[g][L - 1:L, :] for g in groups]
    xd = [(xdt[g] * jnp.exp(a_last[g] - acs[g])).astype(BF16) for g in groups]
    states = [_dot(bs[:, g * ns:(g + 1) * ns].T.astype(BF16), xd[g]) for g in groups]
    for g in groups:
        lanes = slice(g * gw, (g + 1) * gw)
        h_ref[:, lanes] = h_prev[g] * jnp.exp(a_last[g]) + states[g]
        y = y_diag[g] + y_off[g] * jnp.exp(acs[g]) + xs[:, lanes] * dskip_ref[:, lanes]
        yz = y * _silu(z_ref[:, lanes])
        o_ref[:, lanes] = (yz * _rms_scale(yz) * nw_ref[:, lanes]).astype(o_ref.dtype)


def _ssd(zx, dt, conv_w, conv_b, dtb_e, alog_e, dskip_e, norm_w, *, batch, seq):
    L = SSD_CHUNK
    nc = seq // L
    xw = SSD_GROUPS_PER_STEP * GROUP_WIDTH
    sw = SSD_GROUPS_PER_STEP * SSM_STATE
    x_blk = D_INNER // xw
    b_blk = (2 * D_INNER) // sw
    c_blk = b_blk + SSM_GROUPS * SSM_STATE // sw
    cw_b = D_INNER // sw
    cw_c = cw_b + SSM_GROUPS * SSM_STATE // sw

    def rows(b, g, c):
        return b * nc + c

    in_specs = [
        pl.BlockSpec((L, xw), lambda b, g, c: (rows(b, g, c), g)),
        pl.BlockSpec((L, xw), lambda b, g, c: (rows(b, g, c), x_blk + g)),
        pl.BlockSpec((L, sw), lambda b, g, c: (rows(b, g, c), b_blk + g)),
        pl.BlockSpec((L, sw), lambda b, g, c: (rows(b, g, c), c_blk + g)),
        pl.BlockSpec((L, LANES), lambda b, g, c: (rows(b, g, c), 0)),
        pl.BlockSpec((CONV_WIDTH, xw), lambda b, g, c: (0, g)),
        pl.BlockSpec((CONV_WIDTH, sw), lambda b, g, c: (0, cw_b + g)),
        pl.BlockSpec((CONV_WIDTH, sw), lambda b, g, c: (0, cw_c + g)),
        pl.BlockSpec((1, xw), lambda b, g, c: (0, g)),
        pl.BlockSpec((1, sw), lambda b, g, c: (0, cw_b + g)),
        pl.BlockSpec((1, sw), lambda b, g, c: (0, cw_c + g)),
        pl.BlockSpec((1, LANES), lambda b, g, c: (0, 0)),
        pl.BlockSpec((1, LANES), lambda b, g, c: (0, 0)),
        pl.BlockSpec((1, xw), lambda b, g, c: (0, g)),
        pl.BlockSpec((1, xw), lambda b, g, c: (0, g)),
        pl.BlockSpec((SSD_GROUPS_PER_STEP, LANES, GROUP_WIDTH), lambda b, g, c: (g, 0, 0)),
    ]
    head_of_lane = (jnp.arange(SSM_GROUPS)[:, None, None] * SSM_HEADS_PER_GROUP
                    + jnp.arange(GROUP_WIDTH)[None, None, :] // SSM_HEAD_DIM)
    expand = (jnp.arange(LANES)[None, :, None] == head_of_lane).astype(BF16)
    cb2 = conv_b.reshape(1, CONV_DIM)
    return pl.pallas_call(
        _ssd_kernel,
        out_shape=jax.ShapeDtypeStruct((batch * seq, D_INNER), BF16),
        grid=(batch, SSM_GROUPS // SSD_GROUPS_PER_STEP, nc),
        in_specs=in_specs,
        out_specs=pl.BlockSpec((L, xw), lambda b, g, c: (rows(b, g, c), g)),
        scratch_shapes=[
            pltpu.VMEM((SSM_STATE, xw), F32),
            pltpu.VMEM((L + SUBLANES, xw), F32),
            pltpu.VMEM((L + SUBLANES, sw), F32),
            pltpu.VMEM((L + SUBLANES, sw), F32),
        ],
        compiler_params=_params("arbitrary", "arbitrary", "arbitrary"),
        name="ssd",
    )(zx, zx, zx, zx, dt, conv_w, conv_w, conv_w, cb2, cb2, cb2,
      dtb_e, alog_e, dskip_e, norm_w.reshape(1, D_INNER), expand)


ATT_HEADS_PER_STEP = 8


def _attn_kernel(q_ref, k_ref, v_ref, o_ref):
    tq = ATT_BLOCK
    dh = SB_HEAD_DIM
    i = pl.program_id(2)
    scale = dh ** -0.5
    row = lax.broadcasted_iota(jnp.int32, (tq, tq), 0)
    col = lax.broadcasted_iota(jnp.int32, (tq, tq), 1)
    strict = col < row
    row2 = lax.broadcasted_iota(jnp.int32, (2 * tq, tq), 0) & (tq - 1)
    col2 = lax.broadcasted_iota(jnp.int32, (2 * tq, tq), 1)
    neg_later = jnp.where(row2 > col2, -1.0, 0.0).astype(BF16)

    heads = range(ATT_HEADS_PER_STEP)

    def block(j, state, diagonal):
        start = pl.multiple_of(j * tq, tq)
        cols = [slice(h * dh, (h + 1) * dh) for h in heads]
        z = [_dot_nt(q_ref[:, cols[h]], k_ref[pl.ds(start, tq), cols[h]]) * scale for h in heads]
        drop, logit, parts = [], [], []
        for h in heads:
            neg_abs = lax.bitcast_convert_type(
                lax.bitcast_convert_type(z[h], jnp.uint32) | jnp.uint32(0x80000000), F32)
            sp = jnp.maximum(z[h], 0.0) + jnp.log(1.0 + jnp.exp(neg_abs))
            logit.append(z[h] - sp)
            if diagonal:
                sp = jnp.where(strict, sp, 0.0)
            hi, lo = _split2(sp)
            parts.append(jnp.concatenate([hi, lo], axis=1))
            drop.append(sp)
        suffix = [_dot(parts[h], neg_later) for h in heads]
        weights = []
        for h in heads:
            w = jnp.exp(logit[h] + suffix[h] + state[h][0])
            if diagonal:
                w = jnp.where(strict, w, 0.0)
            weights.append(w.astype(BF16))
        out = [_dot(weights[h], v_ref[pl.ds(start, tq), cols[h]]) for h in heads]
        return tuple((state[h][0] - jnp.sum(drop[h], axis=1, keepdims=True), state[h][1] + out[h])
                     for h in heads)

    zero = (jnp.zeros((tq, 1), F32), jnp.zeros((tq, dh), F32))
    state = block(i, (zero,) * ATT_HEADS_PER_STEP, True)
    state = lax.fori_loop(0, i, lambda t, s: block(i - 1 - t, s, False), state)
    for h in range(ATT_HEADS_PER_STEP):
        o_ref[:, h * dh:(h + 1) * dh] = state[h][1].astype(o_ref.dtype)


def _attention(q, kv, *, batch, seq):
    tq = ATT_BLOCK
    nq = seq // tq
    width = ATT_HEADS_PER_STEP * SB_HEAD_DIM
    n_steps = SB_HEADS // ATT_HEADS_PER_STEP
    return pl.pallas_call(
        _attn_kernel,
        out_shape=jax.ShapeDtypeStruct((batch * seq, D_MODEL), BF16),
        grid=(batch, n_steps, nq),
        in_specs=[
            pl.BlockSpec((tq, width), lambda b, h, i: (b * nq + i, h)),
            pl.BlockSpec((seq, width), lambda b, h, i: (b, h)),
            pl.BlockSpec((seq, width), lambda b, h, i: (b, n_steps + h)),
        ],
        out_specs=pl.BlockSpec((tq, width), lambda b, h, i: (b * nq + i, h)),
        compiler_params=_params("arbitrary", "arbitrary", "arbitrary"),
        name="sb_attention",
    )(q, kv, kv)


ROUTER_FINE0 = N_GROUPS


ROUTE_E1, ROUTE_E2, ROUTE_W1, ROUTE_W2, ROUTE_R1, ROUTE_R2 = range(6)


def _router_kernel(h_ref, g_ref, w_ref, b_ref, hn_ref, route_ref, counts_ref, carry_ref):
    @pl.when(pl.program_id(0) == 0)
    def _():
        carry_ref[...] = jnp.zeros_like(carry_ref)

    x = h_ref[...]
    hn = x * _rms_scale(x) * g_ref[...]
    hn_ref[...] = hn
    x1 = hn.astype(BF16)
    x2 = (hn - x1.astype(F32)).astype(BF16)
    w = w_ref[...]
    w1 = w.astype(BF16)
    w2 = (w - w1.astype(F32)).astype(BF16)
    logits = _dot(x1, w1) + _dot(x2, w1) + _dot(x1, w2) + b_ref[...]

    lane_i = lax.broadcasted_iota(jnp.int32, logits.shape, 1)
    lane = lane_i.astype(F32)
    neg = -jnp.inf

    def first_argmax(v, vmax):
        return jnp.min(jnp.where(v == vmax, lane, float(LANES)), axis=1, keepdims=True)

    coarse = jnp.where(lane_i < N_GROUPS, logits, neg)
    cmax = jnp.max(coarse, axis=1, keepdims=True)
    p_g = 1.0 / jnp.sum(jnp.exp(coarse - cmax), axis=1, keepdims=True)
    g_idx = first_argmax(coarse, cmax)
    fine_lo = ROUTER_FINE0 + g_idx * EXPERTS_PER_GROUP
    in_group = (lane >= fine_lo) & (lane < fine_lo + EXPERTS_PER_GROUP)
    fine = jnp.where(in_group, logits, neg)
    m1 = jnp.max(fine, axis=1, keepdims=True)
    denom = jnp.sum(jnp.exp(fine - m1), axis=1, keepdims=True)
    i1 = first_argmax(fine, m1)
    rest = jnp.where(lane == i1, neg, fine)
    m2 = jnp.max(rest, axis=1, keepdims=True)
    i2 = first_argmax(rest, m2)
    p1 = 1.0 / denom
    p2 = jnp.exp(m2 - m1) / denom
    norm = p1 + p2
    gate1 = p_g * (p1 / norm)
    gate2 = p_g * (p2 / norm)

    tm = x.shape[0]
    chosen = (lane == i1) | (lane == i2)
    r = lax.broadcasted_iota(jnp.int32, (tm, tm), 0)
    c = lax.broadcasted_iota(jnp.int32, (tm, tm), 1)
    before = _dot((c < r).astype(BF16), chosen.astype(BF16)) + carry_ref[...]
    rank1 = jnp.sum(jnp.where(lane == i1, before, 0.0), axis=1, keepdims=True)
    rank2 = jnp.sum(jnp.where(lane == i2, before, 0.0), axis=1, keepdims=True)
    carry_ref[...] += jnp.sum(chosen.astype(F32), axis=0, keepdims=True)
    counts_ref[...] = carry_ref[...]

    route = jnp.zeros_like(logits)
    for slot, val in ((ROUTE_E1, i1 - ROUTER_FINE0), (ROUTE_E2, i2 - ROUTER_FINE0),
                      (ROUTE_W1, gate1), (ROUTE_W2, gate2), (ROUTE_R1, rank1), (ROUTE_R2, rank2)):
        route = jnp.where(lane_i == slot, val, route)
    route_ref[...] = route


def _router(h, g, w_router, b_router, *, tm):
    m, k = h.shape
    return pl.pallas_call(
        _router_kernel,
        out_shape=(jax.ShapeDtypeStruct((m, k), F32), jax.ShapeDtypeStruct((m, LANES), F32),
                   jax.ShapeDtypeStruct((1, LANES), F32)),
        grid=(m // tm,),
        in_specs=[
            pl.BlockSpec((tm, k), lambda i: (i, 0)),
            pl.BlockSpec((1, k), lambda i: (0, 0)),
            pl.BlockSpec((k, LANES), lambda i: (0, 0)),
            pl.BlockSpec((1, LANES), lambda i: (0, 0)),
        ],
        out_specs=(pl.BlockSpec((tm, k), lambda i: (i, 0)),
                   pl.BlockSpec((tm, LANES), lambda i: (i, 0)),
                   pl.BlockSpec((1, LANES), lambda i: (0, 0))),
        scratch_shapes=[pltpu.VMEM((1, LANES), F32)],
        compiler_params=_params("arbitrary"),
        name="moe_router",
    )(h, g.reshape(1, k), w_router, b_router)


DISPATCH_TOKENS = 1024
DMA_ISSUE_UNROLL = 8


def _row_copy(src, src_row, dst, dst_row, sem):
    return pltpu.make_async_copy(src.at[pl.ds(src_row, 1)], dst.at[pl.ds(dst_row, 1)], sem)


def _dispatch_kernel(pos1_ref, pos2_ref, hn_ref, xs_ref, sem):
    base = pl.program_id(0) * DISPATCH_TOKENS

    def issue(t, _):
        tok = base + t
        _row_copy(hn_ref, t, xs_ref, pos1_ref[tok], sem).start()
        _row_copy(hn_ref, t, xs_ref, pos2_ref[tok], sem).start()
        return 0

    lax.fori_loop(0, DISPATCH_TOKENS, issue, 0, unroll=DMA_ISSUE_UNROLL)

    for _ in range(2):
        pltpu.make_async_copy(hn_ref, xs_ref.at[pl.ds(0, DISPATCH_TOKENS)], sem).wait()


def _dispatch(hn, pos1, pos2):
    m, k = hn.shape
    return pl.pallas_call(
        _dispatch_kernel,
        out_shape=jax.ShapeDtypeStruct((2 * m, k), F32),
        grid_spec=pltpu.PrefetchScalarGridSpec(
            num_scalar_prefetch=2,
            grid=(m // DISPATCH_TOKENS,),
            in_specs=[pl.BlockSpec((DISPATCH_TOKENS, k), lambda i, *_: (i, 0))],
            out_specs=pl.BlockSpec(memory_space=pl.ANY),
            scratch_shapes=[pltpu.SemaphoreType.DMA],
        ),
        compiler_params=_params("arbitrary"),
        name="moe_dispatch",
    )(pos1, pos2, hn)


EXPERT_TILE = 256
EXPERT_HIDDEN_SLICES = 2


def _experts_kernel(tile_ref, expert_ref, lo_ref, hi_ref, first_ref, valid_ref,
                    x_ref, wg_ref, wu_ref, wd_ref, o_ref):
    del tile_ref, expert_ref
    n = pl.program_id(0)

    @pl.when(valid_ref[n] == 1)
    def _():
        x = x_ref[...].astype(BF16)
        rows = lax.broadcasted_iota(jnp.int32, (EXPERT_TILE, 1), 0)
        inside = (rows >= lo_ref[n]) & (rows < hi_ref[n])
        f = wg_ref.shape[-1]
        fs = f // EXPERT_HIDDEN_SLICES
        cols = [slice(s * fs, (s + 1) * fs) for s in range(EXPERT_HIDDEN_SLICES)]
        pre = [(_dot(x, wg_ref[0, 0, :, c].astype(BF16)), _dot(x, wu_ref[0, 0, :, c].astype(BF16)))
               for c in cols]
        hid = [jnp.where(inside, _silu(g) * u, 0.0).astype(BF16) for g, u in pre]
        y = sum(_dot(hid[s], wd_ref[0, 0, c, :].astype(BF16)) for s, c in enumerate(cols))

        @pl.when(first_ref[n] == 1)
        def _():
            o_ref[...] = y

        @pl.when(first_ref[n] == 0)
        def _():
            o_ref[...] += y


def _experts(work, xs, w_gate, w_up, w_down, layer):
    rows, k = xs.shape
    f = w_gate.shape[-1]
    n_items = work[0].shape[0]
    return pl.pallas_call(
        _experts_kernel,
        out_shape=jax.ShapeDtypeStruct((rows, k), F32),
        grid_spec=pltpu.PrefetchScalarGridSpec(
            num_scalar_prefetch=6,
            grid=(n_items,),
            in_specs=[
                pl.BlockSpec((EXPERT_TILE, k), lambda n, tile, *_: (tile[n], 0)),
                pl.BlockSpec((1, 1, k, f), lambda n, tile, expert, *_: (layer, expert[n], 0, 0)),
                pl.BlockSpec((1, 1, k, f), lambda n, tile, expert, *_: (layer, expert[n], 0, 0)),
                pl.BlockSpec((1, 1, f, k), lambda n, tile, expert, *_: (layer, expert[n], 0, 0)),
            ],
            out_specs=pl.BlockSpec((EXPERT_TILE, k), lambda n, tile, *_: (tile[n], 0)),
        ),
        compiler_params=_params("arbitrary"),
        name="moe_experts",
    )(*work, xs, w_gate, w_up, w_down)


COMBINE_TOKENS = 512


def _combine_kernel(pos1_ref, pos2_ref, h_ref, route_ref, ys_ref, *rest, final_norm):
    if final_norm:
        g_ref, o_ref, buf, sem = rest
    else:
        o_ref, buf, sem = rest
    tm = COMBINE_TOKENS
    i = pl.program_id(0)
    slot = i % 2

    def issue(step, into):
        def body(t, _):
            tok = step * tm + t
            _row_copy(ys_ref, pos1_ref[tok], buf.at[into, 0], t, sem.at[into]).start()
            _row_copy(ys_ref, pos2_ref[tok], buf.at[into, 1], t, sem.at[into]).start()
            return 0

        lax.fori_loop(0, tm, body, 0, unroll=DMA_ISSUE_UNROLL)

    @pl.when(i == 0)
    def _():
        issue(0, 0)

    @pl.when(i + 1 < pl.num_programs(0))
    def _():
        issue(i + 1, 1 - slot)

    for pair in range(2):
        pltpu.make_async_copy(ys_ref.at[pl.ds(0, tm)], buf.at[slot, pair], sem.at[slot]).wait()
    route = route_ref[...]
    w1 = route[:, ROUTE_W1:ROUTE_W1 + 1]
    w2 = route[:, ROUTE_W2:ROUTE_W2 + 1]
    out = h_ref[...] + w1 * buf[slot, 0] + w2 * buf[slot, 1]
    if final_norm:
        out = out * _rms_scale(out) * g_ref[...]
    o_ref[...] = out


def _combine(h, route, ys, pos1, pos2, final_gain=None):
    m, k = h.shape
    tm = COMBINE_TOKENS
    final_norm = final_gain is not None
    in_specs = [
        pl.BlockSpec((tm, k), lambda i, *_: (i, 0)),
        pl.BlockSpec((tm, LANES), lambda i, *_: (i, 0)),
        pl.BlockSpec(memory_space=pl.ANY),
    ]
    operands = [h, route, ys]
    if final_norm:
        in_specs.append(pl.BlockSpec((1, k), lambda i, *_: (0, 0)))
        operands.append(final_gain.reshape(1, k))
    return pl.pallas_call(
        functools.partial(_combine_kernel, final_norm=final_norm),
        out_shape=jax.ShapeDtypeStruct((m, k), F32),
        grid_spec=pltpu.PrefetchScalarGridSpec(
            num_scalar_prefetch=2,
            grid=(m // tm,),
            in_specs=in_specs,
            out_specs=pl.BlockSpec((tm, k), lambda i, *_: (i, 0)),
            scratch_shapes=[pltpu.VMEM((2, 2, tm, k), F32), pltpu.SemaphoreType.DMA((2,))],
        ),
        compiler_params=_params("arbitrary"),
        name="moe_combine",
    )(pos1, pos2, *operands)


def _work_list(counts, n_rows):
    tm = EXPERT_TILE
    n_tiles = n_rows // tm
    n_items = n_tiles + N_EXPERTS - 1
    i32 = jnp.int32
    count_le = lambda sorted_vals, q: jnp.sum((sorted_vals[None, :] <= q[:, None]).astype(i32), axis=1)
    pick = lambda table, idx: jnp.sum(
        jnp.where(idx[:, None] == jnp.arange(table.shape[0], dtype=i32)[None, :], table[None, :], 0), axis=1)
    e_ids = jnp.arange(N_EXPERTS, dtype=i32)
    ends = jnp.sum(jnp.where(e_ids[None, :] <= e_ids[:, None], counts[None, :], 0), axis=1)
    starts = ends - counts
    tile_ids = jnp.arange(n_tiles, dtype=i32)
    tile_lo = tile_ids * tm
    e_first = count_le(ends, tile_lo)
    e_last = count_le(ends, tile_lo + (tm - 1))
    per_tile = e_last - e_first + 1
    item0 = jnp.sum(jnp.where(tile_ids[None, :] < tile_ids[:, None], per_tile[None, :], 0), axis=1)
    total = jnp.sum(per_tile)
    n = jnp.arange(n_items, dtype=i32)
    valid = n < total
    tile = jnp.where(valid, count_le(item0, n) - 1, n_tiles - 1)
    expert = jnp.where(valid, pick(e_first, tile) + (n - pick(item0, tile)), e_last[-1])
    expert = jnp.clip(expert, 0, N_EXPERTS - 1)
    lo = jnp.clip(pick(starts, expert) - tile * tm, 0, tm)
    hi = jnp.clip(pick(ends, expert) - tile * tm, 0, tm)
    first = (n == pick(item0, tile)) & valid
    as_i32 = lambda a: a.astype(i32)
    return tuple(map(as_i32, (tile, expert, lo, hi, first, valid))), starts


def _router_weights(w_coarse, b_coarse, w_fine, b_fine):
    pad = LANES - N_GROUPS - N_EXPERTS
    w = jnp.concatenate([w_coarse, w_fine.reshape(D_MODEL, N_EXPERTS),
                         jnp.zeros((D_MODEL, pad), F32)], axis=1)
    b = jnp.concatenate([b_coarse, b_fine.reshape(N_EXPERTS), jnp.zeros((pad,), F32)])
    return w, b.reshape(1, LANES)


def _moe(h, norm_g, w_coarse, b_coarse, w_fine, b_fine, w_gate, w_up, w_down, layer, final_gain=None):
    w_r, b_r = _router_weights(w_coarse, b_coarse, w_fine, b_fine)
    hn, route, counts = _router(h, norm_g, w_r, b_r, tm=512)
    counts = counts[0, ROUTER_FINE0:ROUTER_FINE0 + N_EXPERTS].astype(jnp.int32)
    work, starts = _work_list(counts, 2 * h.shape[0])
    ids = route[:, :ROUTE_R2 + 1].astype(jnp.int32)
    e_ids = jnp.arange(N_EXPERTS, dtype=jnp.int32)[None, :]
    start_of = lambda e: jnp.sum(jnp.where(e[:, None] == e_ids, starts[None, :], 0), axis=1)
    pos1 = start_of(ids[:, ROUTE_E1]) + ids[:, ROUTE_R1]
    pos2 = start_of(ids[:, ROUTE_E2]) + ids[:, ROUTE_R2]
    xs = _dispatch(hn, pos1, pos2)
    ys = _experts(work, xs, w_gate, w_up, w_down, layer)
    return _combine(h, route, ys, pos1, pos2, final_gain)


def kernel(x, mix_norm, ffn_norm, ssm_w_in, ssm_conv_w, ssm_conv_b, ssm_dt_bias, ssm_a_log, ssm_d,
           ssm_norm_w, ssm_w_out, kv_norm, w_k, w_v, sb_w_q, sb_w_out, moe_w_coarse, moe_b_coarse,
           moe_w_fine, moe_b_fine, moe_w_gate, moe_w_up, moe_w_down, final_norm):
    batch, seq, d = x.shape
    h = x.reshape(batch * seq, d)

    zx, dt = _in_proj(h, mix_norm[0], ssm_w_in)
    per_head = lambda p: jnp.pad(p, (0, LANES - SSM_HEADS)).reshape(1, LANES)
    d_skip = jnp.repeat(ssm_d[0], SSM_HEAD_DIM).reshape(1, D_INNER)
    yn = _ssd(zx, dt, ssm_conv_w[0], ssm_conv_b[0], per_head(ssm_dt_bias[0]), per_head(ssm_a_log[0]),
              d_skip, ssm_norm_w[0], batch=batch, seq=seq)
    h = _matmul_res(yn, ssm_w_out.astype(BF16), h, tm=1024, tn=512)
    h = _moe(h, ffn_norm[0], moe_w_coarse[0], moe_b_coarse[0], moe_w_fine[0], moe_b_fine[0],
             moe_w_gate, moe_w_up, moe_w_down, 0)

    w_kv = jnp.concatenate([w_k.astype(BF16), w_v.astype(BF16)], axis=1)
    kv = _norm_matmul(h, kv_norm, [w_kv], tm=1024, tn=1024, out_dtype=BF16)

    q = _norm_matmul(h, mix_norm[1], [sb_w_q.astype(BF16)], tm=1024, tn=1024, out_dtype=BF16)
    o = _attention(q, kv, batch=batch, seq=seq)
    h = _matmul_res(o, sb_w_out.astype(BF16), h, tm=1024, tn=512)
    h = _moe(h, ffn_norm[1], moe_w_coarse[1], moe_b_coarse[1], moe_w_fine[1], moe_b_fine[1],
             moe_w_gate, moe_w_up, moe_w_down, 1, final_gain=final_norm)
    return h.reshape(batch, seq, d)
```

```python
import functools

import jax
import jax.numpy as jnp
from jax import lax
from jax.experimental import pallas as pl
from jax.experimental.pallas import tpu as pltpu

F32 = jnp.float32
BF16 = jnp.bfloat16
EPS = 1e-5

D_MODEL = 2048
D_INNER = 4096
SSM_HEAD_DIM = 64
SSM_HEADS = 64
SSM_GROUPS = 8
SSM_HEADS_PER_GROUP = 8
SSM_STATE = 128
GROUP_WIDTH = D_INNER // SSM_GROUPS
CONV_WIDTH = 4
CONV_DIM = D_INNER + 2 * SSM_GROUPS * SSM_STATE
IN_PROJ_DIM = D_INNER + CONV_DIM + SSM_HEADS
SB_HEADS = 16
SB_HEAD_DIM = 128
N_GROUPS = 4
EXPERTS_PER_GROUP = 4
N_EXPERTS = 16
D_EXPERT = 512

LANES = 128
SUBLANES = 8
DT_COL_BLOCK = (D_INNER + CONV_DIM) // LANES
SSD_CHUNK = 128
SSD_GROUPS_PER_STEP = 8
ATT_BLOCK = 256
VMEM_LIMIT = 56 * 1024 * 1024


def _dot(a, b):
    return jnp.dot(a, b, preferred_element_type=F32)


def _dot_nt(a, b):
    return lax.dot_general(a, b, (((1,), (1,)), ((), ())), preferred_element_type=F32)


def _split3(v):
    v1 = v.astype(BF16)
    r = v - v1.astype(F32)
    v2 = r.astype(BF16)
    r = r - v2.astype(F32)
    return v1, v2, r.astype(BF16)


def _split2(v):
    hi = v.astype(BF16)
    return hi, (v - hi.astype(F32)).astype(BF16)


def _dot3_left(m, v):
    v1, v2, v3 = _split3(v)
    return _dot(m, v1) + _dot(m, v2) + _dot(m, v3)


def _dot3_right(v, m):
    v1, v2, v3 = _split3(v)
    return _dot(v1, m) + _dot(v2, m) + _dot(v3, m)


def _softplus(x):
    return jnp.maximum(x, 0.0) + jnp.log(1.0 + jnp.exp(-jnp.abs(x)))


def _silu(x):
    half = 0.5 * x
    return half + half * jnp.tanh(half)


def _rms_scale(x):
    return lax.rsqrt(jnp.mean(x * x, axis=-1, keepdims=True) + EPS)


def _params(*sem):
    return pltpu.CompilerParams(dimension_semantics=sem, vmem_limit_bytes=VMEM_LIMIT)


def _weight_spec(w, k, tn, col_tile):
    if w.ndim == 2:
        return pl.BlockSpec((k, tn), lambda i, j: (0, col_tile(i, j)))
    return pl.BlockSpec((None, k, tn), lambda i, j: (0, 0, col_tile(i, j)))


def _norm_matmul_kernel(x_ref, g_ref, *rest, tiles):
    w_refs, (o_ref, xn_ref) = rest[:len(tiles)], rest[len(tiles):]
    j = pl.program_id(1)

    @pl.when(j == 0)
    def _():
        x = x_ref[...]
        xn_ref[...] = (x * _rms_scale(x) * g_ref[...]).astype(BF16)

    first = 0
    for w_ref, count in zip(w_refs, tiles):
        @pl.when((j >= first) & (j < first + count))
        def _(w_ref=w_ref):
            o_ref[...] = _dot(xn_ref[...], w_ref[...]).astype(o_ref.dtype)
        first += count


def _norm_matmul(x, g, weights, *, tm, tn, out_dtype):
    m, k = x.shape
    assert all(w.shape[-1] % tn == 0 for w in weights)
    tiles = tuple(w.shape[-1] // tn for w in weights)
    n = sum(w.shape[-1] for w in weights)
    firsts = [sum(tiles[:a]) for a in range(len(tiles))]
    w_specs = [_weight_spec(w, k, tn, lambda i, j, lo=lo, count=count: jnp.clip(j - lo, 0, count - 1))
               for w, lo, count in zip(weights, firsts, tiles)]
    return pl.pallas_call(
        functools.partial(_norm_matmul_kernel, tiles=tiles),
        out_shape=jax.ShapeDtypeStruct((m, n), out_dtype),
        grid=(m // tm, sum(tiles)),
        in_specs=[
            pl.BlockSpec((tm, k), lambda i, j: (i, 0)),
            pl.BlockSpec((1, k), lambda i, j: (0, 0)),
            *w_specs,
        ],
        out_specs=pl.BlockSpec((tm, tn), lambda i, j: (i, j)),
        scratch_shapes=[pltpu.VMEM((tm, k), BF16)],
        compiler_params=_params("arbitrary", "arbitrary"),
        name="norm_matmul",
    )(x, g.reshape(1, k), *weights)


IN_PROJ_ROWS = 1024
IN_PROJ_COLS = 512


def _in_proj_kernel(x_ref, g_ref, w_ref, wdt_ref, o_ref, dt_ref, xn_ref):
    @pl.when(pl.program_id(1) == 0)
    def _():
        x = x_ref[...]
        xn = (x * _rms_scale(x) * g_ref[...]).astype(BF16)
        xn_ref[...] = xn
        head = lax.broadcasted_iota(jnp.int32, dt_ref.shape, 1)
        dt_ref[...] = jnp.where(head < SSM_HEADS, _dot_nt(xn, wdt_ref[...].astype(BF16)), 0.0)

    o_ref[...] = _dot_nt(xn_ref[...], w_ref[...].astype(BF16))


def _in_proj(x, g, w_in):
    m, k = x.shape
    w_in_t = jnp.swapaxes(w_in, 1, 2)
    tm, tn = IN_PROJ_ROWS, IN_PROJ_COLS
    n_main = D_INNER + CONV_DIM
    return pl.pallas_call(
        _in_proj_kernel,
        out_shape=(jax.ShapeDtypeStruct((m, n_main), F32), jax.ShapeDtypeStruct((m, LANES), F32)),
        grid=(m // tm, n_main // tn),
        in_specs=[
            pl.BlockSpec((tm, k), lambda i, j: (i, 0)),
            pl.BlockSpec((1, k), lambda i, j: (0, 0)),
            pl.BlockSpec((None, tn, k), lambda i, j: (0, j, 0)),
            pl.BlockSpec((None, LANES, k), lambda i, j: (0, DT_COL_BLOCK, 0)),
        ],
        out_specs=(pl.BlockSpec((tm, tn), lambda i, j: (i, j)),
                   pl.BlockSpec((tm, LANES), lambda i, j: (i, 0))),
        scratch_shapes=[pltpu.VMEM((tm, k), BF16)],
        compiler_params=_params("arbitrary", "arbitrary"),
        name="in_proj",
    )(x, g.reshape(1, k), w_in_t, w_in_t)


def _matmul_res_kernel(x_ref, w_ref, r_ref, o_ref):
    o_ref[...] = r_ref[...] + _dot(x_ref[...], w_ref[...])


def _matmul_res(x, w, res, *, tm, tn):
    m, k = x.shape
    n = w.shape[-1]
    return pl.pallas_call(
        _matmul_res_kernel,
        out_shape=jax.ShapeDtypeStruct((m, n), F32),
        grid=(m // tm, n // tn),
        in_specs=[
            pl.BlockSpec((tm, k), lambda i, j: (i, 0)),
            _weight_spec(w, k, tn, lambda i, j: j),
            pl.BlockSpec((tm, tn), lambda i, j: (i, j)),
        ],
        out_specs=pl.BlockSpec((tm, tn), lambda i, j: (i, j)),
        compiler_params=_params("arbitrary", "arbitrary"),
        name="matmul_res",
    )(x, w, res)


def _ssd_kernel(z_ref, x_ref, b_ref, c_ref, dt_ref,
                wx_ref, wb_ref, wc_ref, bx_ref, bb_ref, bc_ref,
                dtb_ref, alog_ref, dskip_ref, nw_ref, expand_ref,
                o_ref,
                h_ref, xbuf, bbuf, cbuf):
    L = SSD_CHUNK
    gw = GROUP_WIDTH
    ns = SSM_STATE
    halo = SUBLANES
    groups = range(SSD_GROUPS_PER_STEP)

    @pl.when(pl.program_id(2) == 0)
    def _():
        h_ref[...] = jnp.zeros_like(h_ref)
        xbuf[0:halo, :] = jnp.zeros((halo, xbuf.shape[1]), F32)
        bbuf[0:halo, :] = jnp.zeros((halo, bbuf.shape[1]), F32)
        cbuf[0:halo, :] = jnp.zeros((halo, cbuf.shape[1]), F32)

    def conv_silu(raw_ref, buf, w_ref, bias_ref):
        raw = raw_ref[...]
        buf[halo:halo + L, :] = raw
        padded = buf[...]
        acc = bias_ref[...] + w_ref[CONV_WIDTH - 1:CONV_WIDTH, :] * raw
        for back in range(1, CONV_WIDTH):
            tap = pltpu.roll(padded, back, axis=0)[halo:halo + L, :]
            k = CONV_WIDTH - 1 - back
            acc = acc + w_ref[k:k + 1, :] * tap
        buf[0:halo, :] = raw[L - halo:L, :]
        return _silu(acc)

    bs = conv_silu(b_ref, bbuf, wb_ref, bb_ref)
    cs = conv_silu(c_ref, cbuf, wc_ref, bc_ref)
    bs_b = [bs[:, g * ns:(g + 1) * ns].astype(BF16) for g in groups]
    cs_b = [cs[:, g * ns:(g + 1) * ns].astype(BF16) for g in groups]
    cb = [_dot_nt(cs_b[g], bs_b[g]) for g in groups]
    h_prev = [h_ref[:, g * gw:(g + 1) * gw] for g in groups]
    y_off = [_dot(cs_b[g], h_prev[g].astype(BF16)) for g in groups]

    dt_heads = _softplus(dt_ref[...] + dtb_ref[...])
    dt_parts = _split3(dt_heads)
    a_head_parts = _split2(dt_heads * (-jnp.exp(alog_ref[...])))
    dt = [sum(_dot(p, expand_ref[g]) for p in dt_parts) for g in groups]
    a_parts = [[_dot(p, expand_ref[g]).astype(BF16) for p in a_head_parts] for g in groups]

    xs = conv_silu(x_ref, xbuf, wx_ref, bx_ref)

    row = lax.broadcasted_iota(jnp.int32, (L, L), 0)
    col = lax.broadcasted_iota(jnp.int32, (L, L), 1)
    causal = row >= col
    tril = causal.astype(BF16)
    triu = (row <= col).astype(BF16)
    hrow = lax.broadcasted_iota(jnp.int32, (SUBLANES, gw), 0)
    hcol = lax.broadcasted_iota(jnp.int32, (SUBLANES, gw), 1)
    pick = (hcol == hrow * SSM_HEAD_DIM).astype(BF16)

    acs =[sum(_dot(tril, p) for p in a_parts[g]) for g in groups]
    a_rows = [[_dot_nt(pick, p).astype(BF16) for p in a_parts[g]] for g in groups]
    acs_rows = [sum(_dot(p, triu) for p in a_rows[g]) for g in groups]

    xdt = [xs[:, g * gw:(g + 1) * gw] * dt[g] for g in groups]
    xdt_b = [v.astype(BF16) for v in xdt]
    y_diag = []
    for g in groups:
        parts = []
        for r in range(SSM_HEADS_PER_GROUP):
            lo = r * SSM_HEAD_DIM
            seg = acs[g][:, lo:lo + 1] - acs_rows[g][r:r + 1, :]
            decay = jnp.exp(jnp.where(causal, seg, -jnp.inf))
            m = (cb[g] * decay).astype(BF16)
            parts.append(_dot(m, xdt_b[g][:, lo:lo + SSM_HEAD_DIM]))
        y_diag.append(jnp.concatenate(parts, axis=1))

    a_last = [acs[g][L - 1:L, :] for g in groups]
    xd = [(xdt[g] * jnp.exp(a_last[g] - acs[g])).astype(BF16) for g in groups]
    states = [_dot(bs[:, g * ns:(g + 1) * ns].T.astype(BF16), xd[g]) for g in groups]
    for g in groups:
        lanes = slice(g * gw, (g + 1) * gw)
        h_ref[:, lanes] = h_prev[g] * jnp.exp(a_last[g]) + states[g]
        y = y_diag[g] + y_off[g] * jnp.exp(acs[g]) + xs[:, lanes] * dskip_ref[:, lanes]
        yz = y * _silu(z_ref[:, lanes])
        o_ref[:, lanes] = (yz * _rms_scale(yz) * nw_ref[:, lanes]).astype(o_ref.dtype)


def _ssd(zx, dt, conv_w, conv_b, dtb_e, alog_e, dskip_e, norm_w, *, batch, seq):
    L = SSD_CHUNK
    nc = seq // L
    xw = SSD_GROUPS_PER_STEP * GROUP_WIDTH
    sw = SSD_GROUPS_PER_STEP * SSM_STATE
    x_blk = D_INNER // xw
    b_blk = (2 * D_INNER) // sw
    c_blk = b_blk + SSM_GROUPS * SSM_STATE // sw
    cw_b = D_INNER // sw
    cw_c = cw_b + SSM_GROUPS * SSM_STATE // sw

    def rows(b, g, c):
        return b * nc + c

    in_specs = [
        pl.BlockSpec((L, xw), lambda b, g, c: (rows(b, g, c), g)),
        pl.BlockSpec((L, xw), lambda b, g, c: (rows(b, g, c), x_blk + g)),
        pl.BlockSpec((L, sw), lambda b, g, c: (rows(b, g, c), b_blk + g)),
        pl.BlockSpec((L, sw), lambda b, g, c: (rows(b, g, c), c_blk + g)),
        pl.BlockSpec((L, LANES), lambda b, g, c: (rows(b, g, c), 0)),
        pl.BlockSpec((CONV_WIDTH, xw), lambda b, g, c: (0, g)),
        pl.BlockSpec((CONV_WIDTH, sw), lambda b, g, c: (0, cw_b + g)),
        pl.BlockSpec((CONV_WIDTH, sw), lambda b, g, c: (0, cw_c + g)),
        pl.BlockSpec((1, xw), lambda b, g, c: (0, g)),
        pl.BlockSpec((1, sw), lambda b, g, c: (0, cw_b + g)),
        pl.BlockSpec((1, sw), lambda b, g, c: (0, cw_c + g)),
        pl.BlockSpec((1, LANES), lambda b, g, c: (0, 0)),
        pl.BlockSpec((1, LANES), lambda b, g, c: (0, 0)),
        pl.BlockSpec((1, xw), lambda b, g, c: (0, g)),
        pl.BlockSpec((1, xw), lambda b, g, c: (0, g)),
        pl.BlockSpec((SSD_GROUPS_PER_STEP, LANES, GROUP_WIDTH), lambda b, g, c: (g, 0, 0)),
    ]
    head_of_lane = (jnp.arange(SSM_GROUPS)[:, None, None] * SSM_HEADS_PER_GROUP
                    + jnp.arange(GROUP_WIDTH)[None, None, :] // SSM_HEAD_DIM)
    expand = (jnp.arange(LANES)[None, :, None] == head_of_lane).astype(BF16)
    cb2 = conv_b.reshape(1, CONV_DIM)
    return pl.pallas_call(
        _ssd_kernel,
        out_shape=jax.ShapeDtypeStruct((batch * seq, D_INNER), BF16),
        grid=(batch, SSM_GROUPS // SSD_GROUPS_PER_STEP, nc),
        in_specs=in_specs,
        out_specs=pl.BlockSpec((L, xw), lambda b, g, c: (rows(b, g, c), g)),
        scratch_shapes=[
            pltpu.VMEM((SSM_STATE, xw), F32),
            pltpu.VMEM((L + SUBLANES, xw), F32),
            pltpu.VMEM((L + SUBLANES, sw), F32),
            pltpu.VMEM((L + SUBLANES, sw), F32),
        ],
        compiler_params=_params("arbitrary", "arbitrary", "arbitrary"),
        name="ssd",
    )(zx, zx, zx, zx, dt, conv_w, conv_w, conv_w, cb2, cb2, cb2,
      dtb_e, alog_e, dskip_e, norm_w.reshape(1, D_INNER), expand)


ATT_HEADS_PER_STEP = 8


def _attn_kernel(q_ref, k_ref, v_ref, o_ref):
    tq = ATT_BLOCK
    dh = SB_HEAD_DIM
    i = pl.program_id(2)
    scale = dh ** -0.5
    row = lax.broadcasted_iota(jnp.int32, (tq, tq), 0)
    col = lax.broadcasted_iota(jnp.int32, (tq, tq), 1)
    strict = col < row
    row2 = lax.broadcasted_iota(jnp.int32, (2 * tq, tq), 0) & (tq - 1)
    col2 = lax.broadcasted_iota(jnp.int32, (2 * tq, tq), 1)
    neg_later = jnp.where(row2 > col2, -1.0, 0.0).astype(BF16)

    heads = range(ATT_HEADS_PER_STEP)

    def block(j, state, diagonal):
        start = pl.multiple_of(j * tq, tq)
        cols = [slice(h * dh, (h + 1) * dh) for h in heads]
        z = [_dot_nt(q_ref[:, cols[h]], k_ref[pl.ds(start, tq), cols[h]]) * scale for h in heads]
        drop, logit, parts = [], [], []
        for h in heads:
            neg_abs = lax.bitcast_convert_type(
                lax.bitcast_convert_type(z[h], jnp.uint32) | jnp.uint32(0x80000000), F32)
            sp = jnp.maximum(z[h], 0.0) + jnp.log(1.0 + jnp.exp(neg_abs))
            logit.append(z[h] - sp)
            if diagonal:
                sp = jnp.where(strict, sp, 0.0)
            hi, lo = _split2(sp)
            parts.append(jnp.concatenate([hi, lo], axis=1))
            drop.append(sp)
        suffix = [_dot(parts[h], neg_later) for h in heads]
        weights = []
        for h in heads:
            w = jnp.exp(logit[h] + suffix[h] + state[h][0])
            if diagonal:
                w = jnp.where(strict, w, 0.0)
            weights.append(w.astype(BF16))
        out = [_dot(weights[h], v_ref[pl.ds(start, tq), cols[h]]) for h in heads]
        return tuple((state[h][0] - jnp.sum(drop[h], axis=1, keepdims=True), state[h][1] + out[h])
                     for h in heads)

    zero = (jnp.zeros((tq, 1), F32), jnp.zeros((tq, dh), F32))
    state = block(i, (zero,) * ATT_HEADS_PER_STEP, True)
    state = lax.fori_loop(0, i, lambda t, s: block(i - 1 - t, s, False), state)
    for h in range(ATT_HEADS_PER_STEP):
        o_ref[:, h * dh:(h + 1) * dh] = state[h][1].astype(o_ref.dtype)


def _attention(q, kv, *, batch, seq):
    tq = ATT_BLOCK
    nq = seq // tq
    width = ATT_HEADS_PER_STEP * SB_HEAD_DIM
    n_steps = SB_HEADS // ATT_HEADS_PER_STEP
    return pl.pallas_call(
        _attn_kernel,
        out_shape=jax.ShapeDtypeStruct((batch * seq, D_MODEL), BF16),
        grid=(batch, n_steps, nq),
        in_specs=[
            pl.BlockSpec((tq, width), lambda b, h, i: (b * nq + i, h)),
            pl.BlockSpec((seq, width), lambda b, h, i: (b, h)),
            pl.BlockSpec((seq, width), lambda b, h, i: (b, n_steps + h)),
        ],
        out_specs=pl.BlockSpec((tq, width), lambda b, h, i: (b * nq + i, h)),
        compiler_params=_params("arbitrary", "arbitrary", "arbitrary"),
        name="sb_attention",
    )(q, kv, kv)


ROUTER_FINE0 = N_GROUPS


ROUTE_E1, ROUTE_E2, ROUTE_W1, ROUTE_W2, ROUTE_R1, ROUTE_R2 = range(6)


def _router_kernel(h_ref, g_ref, w_ref, b_ref, hn_ref, route_ref, counts_ref, carry_ref):
    @pl.when(pl.program_id(0) == 0)
    def _():
        carry_ref[...] = jnp.zeros_like(carry_ref)

    x = h_ref[...]
    hn = x * _rms_scale(x) * g_ref[...]
    hn_ref[...] = hn
    x1 = hn.astype(BF16)
    x2 = (hn - x1.astype(F32)).astype(BF16)
    w = w_ref[...]
    w1 = w.astype(BF16)
    w2 = (w - w1.astype(F32)).astype(BF16)
    logits = _dot(x1, w1) + _dot(x2, w1) + _dot(x1, w2) + b_ref[...]

    lane_i = lax.broadcasted_iota(jnp.int32, logits.shape, 1)
    lane = lane_i.astype(F32)
    neg = -jnp.inf

    def first_argmax(v, vmax):
        return jnp.min(jnp.where(v == vmax, lane, float(LANES)), axis=1, keepdims=True)

    coarse = jnp.where(lane_i < N_GROUPS, logits, neg)
    cmax = jnp.max(coarse, axis=1, keepdims=True)
    p_g = 1.0 / jnp.sum(jnp.exp(coarse - cmax), axis=1, keepdims=True)
    g_idx = first_argmax(coarse, cmax)
    fine_lo = ROUTER_FINE0 + g_idx * EXPERTS_PER_GROUP
    in_group = (lane >= fine_lo) & (lane < fine_lo + EXPERTS_PER_GROUP)
    fine = jnp.where(in_group, logits, neg)
    m1 = jnp.max(fine, axis=1, keepdims=True)
    denom = jnp.sum(jnp.exp(fine - m1), axis=1, keepdims=True)
    i1 = first_argmax(fine, m1)
    rest = jnp.where(lane == i1, neg, fine)
    m2 = jnp.max(rest, axis=1, keepdims=True)
    i2 = first_argmax(rest, m2)
    p1 = 1.0 / denom
    p2 = jnp.exp(m2 - m1) / denom
    norm = p1 + p2
    gate1 = p_g * (p1 / norm)
    gate2 = p_g * (p2 / norm)

    tm = x.shape[0]
    chosen = (lane == i1) | (lane == i2)
    r = lax.broadcasted_iota(jnp.int32, (tm, tm), 0)
    c = lax.broadcasted_iota(jnp.int32, (tm, tm), 1)
    before = _dot((c < r).astype(BF16), chosen.astype(BF16)) + carry_ref[...]
    rank1 = jnp.sum(jnp.where(lane == i1, before, 0.0), axis=1, keepdims=True)
    rank2 = jnp.sum(jnp.where(lane == i2, before, 0.0), axis=1, keepdims=True)
    carry_ref[...] += jnp.sum(chosen.astype(F32), axis=0, keepdims=True)
    counts_ref[...] = carry_ref[...]

    route = jnp.zeros_like(logits)
    for slot, val in ((ROUTE_E1, i1 - ROUTER_FINE0), (ROUTE_E2, i2 - ROUTER_FINE0),
                      (ROUTE_W1, gate1), (ROUTE_W2, gate2), (ROUTE_R1, rank1), (ROUTE_R2, rank2)):
        route = jnp.where(lane_i == slot, val, route)
    route_ref[...] = route


def _router(h, g, w_router, b_router, *, tm):
    m, k = h.shape
    return pl.pallas_call(
        _router_kernel,
        out_shape=(jax.ShapeDtypeStruct((m, k), F32), jax.ShapeDtypeStruct((m, LANES), F32),
                   jax.ShapeDtypeStruct((1, LANES), F32)),
        grid=(m // tm,),
        in_specs=[
            pl.BlockSpec((tm, k), lambda i: (i, 0)),
            pl.BlockSpec((1, k), lambda i: (0, 0)),
            pl.BlockSpec((k, LANES), lambda i: (0, 0)),
            pl.BlockSpec((1, LANES), lambda i: (0, 0)),
        ],
        out_specs=(pl.BlockSpec((tm, k), lambda i: (i, 0)),
                   pl.BlockSpec((tm, LANES), lambda i: (i, 0)),
                   pl.BlockSpec((1, LANES), lambda i: (0, 0))),
        scratch_shapes=[pltpu.VMEM((1, LANES), F32)],
        compiler_params=_params("arbitrary"),
        name="moe_router",
    )(h, g.reshape(1, k), w_router, b_router)


DISPATCH_TOKENS = 1024
DMA_ISSUE_UNROLL = 8


def _row_copy(src, src_row, dst, dst_row, sem):
    return pltpu.make_async_copy(src.at[pl.ds(src_row, 1)], dst.at[pl.ds(dst_row, 1)], sem)


def _dispatch_kernel(pos1_ref, pos2_ref, hn_ref, xs_ref, sem):
    base = pl.program_id(0) * DISPATCH_TOKENS

    def issue(t, _):
        tok = base + t
        _row_copy(hn_ref, t, xs_ref, pos1_ref[tok], sem).start()
        _row_copy(hn_ref, t, xs_ref, pos2_ref[tok], sem).start(priority=1)
        return 0

    lax.fori_loop(0, DISPATCH_TOKENS, issue, 0, unroll=DMA_ISSUE_UNROLL)

    for _ in range(2):
        pltpu.make_async_copy(hn_ref, xs_ref.at[pl.ds(0, DISPATCH_TOKENS)], sem).wait()


def _dispatch(hn, pos1, pos2):
    m, k = hn.shape
    return pl.pallas_call(
        _dispatch_kernel,
        out_shape=jax.ShapeDtypeStruct((2 * m, k), F32),
        grid_spec=pltpu.PrefetchScalarGridSpec(
            num_scalar_prefetch=2,
            grid=(m // DISPATCH_TOKENS,),
            in_specs=[pl.BlockSpec((DISPATCH_TOKENS, k), lambda i, *_: (i, 0))],
            out_specs=pl.BlockSpec(memory_space=pl.ANY),
            scratch_shapes=[pltpu.SemaphoreType.DMA],
        ),
        compiler_params=_params("arbitrary"),
        name="moe_dispatch",
    )(pos1, pos2, hn)


EXPERT_TILE = 256
EXPERT_HIDDEN_SLICES = 2


def _experts_kernel(tile_ref, expert_ref, lo_ref, hi_ref, first_ref, valid_ref,
                    x_ref, wg_ref, wu_ref, wd_ref, o_ref):
    del tile_ref, expert_ref
    n = pl.program_id(0)

    @pl.when(valid_ref[n] == 1)
    def _():
        x = x_ref[...].astype(BF16)
        rows = lax.broadcasted_iota(jnp.int32, (EXPERT_TILE, 1), 0)
        inside = (rows >= lo_ref[n]) & (rows < hi_ref[n])
        f = wg_ref.shape[-1]
        fs = f // EXPERT_HIDDEN_SLICES
        cols = [slice(s * fs, (s + 1) * fs) for s in range(EXPERT_HIDDEN_SLICES)]
        pre = [(_dot(x, wg_ref[0, 0, :, c].astype(BF16)), _dot(x, wu_ref[0, 0, :, c].astype(BF16)))
               for c in cols]
        hid = [jnp.where(inside, _silu(g) * u, 0.0).astype(BF16) for g, u in pre]
        y = sum(_dot(hid[s], wd_ref[0, 0, c, :].astype(BF16)) for s, c in enumerate(cols))

        @pl.when(first_ref[n] == 1)
        def _():
            o_ref[...] = y

        @pl.when(first_ref[n] == 0)
        def _():
            o_ref[...] += y


def _experts(work, xs, w_gate, w_up, w_down, layer):
    rows, k = xs.shape
    f = w_gate.shape[-1]
    n_items = work[0].shape[0]
    return pl.pallas_call(
        _experts_kernel,
        out_shape=jax.ShapeDtypeStruct((rows, k), F32),
        grid_spec=pltpu.PrefetchScalarGridSpec(
            num_scalar_prefetch=6,
            grid=(n_items,),
            in_specs=[
                pl.BlockSpec((EXPERT_TILE, k), lambda n, tile, *_: (tile[n], 0)),
                pl.BlockSpec((1, 1, k, f), lambda n, tile, expert, *_: (layer, expert[n], 0, 0)),
                pl.BlockSpec((1, 1, k, f), lambda n, tile, expert, *_: (layer, expert[n], 0, 0)),
                pl.BlockSpec((1, 1, f, k), lambda n, tile, expert, *_: (layer, expert[n], 0, 0)),
            ],
            out_specs=pl.BlockSpec((EXPERT_TILE, k), lambda n, tile, *_: (tile[n], 0)),
        ),
        compiler_params=_params("arbitrary"),
        name="moe_experts",
    )(*work, xs, w_gate, w_up, w_down)


COMBINE_TOKENS = 512


def _combine_kernel(pos1_ref, pos2_ref, h_ref, route_ref, ys_ref, *rest, final_norm):
    if final_norm:
        g_ref, o_ref, buf, sem = rest
    else:
        o_ref, buf, sem = rest
    tm = COMBINE_TOKENS
    i = pl.program_id(0)
    slot = i % 2

    def issue(step, into):
        def body(t, _):
            tok = step * tm + t
            _row_copy(ys_ref, pos1_ref[tok], buf.at[into, 0], t, sem.at[into]).start()
            _row_copy(ys_ref, pos2_ref[tok], buf.at[into, 1], t, sem.at[into]).start(priority=1)
            return 0

        lax.fori_loop(0, tm, body, 0, unroll=DMA_ISSUE_UNROLL)

    @pl.when(i == 0)
    def _():
        issue(0, 0)

    @pl.when(i + 1 < pl.num_programs(0))
    def _():
        issue(i + 1, 1 - slot)

    for pair in range(2):
        pltpu.make_async_copy(ys_ref.at[pl.ds(0, tm)], buf.at[slot, pair], sem.at[slot]).wait()
    route = route_ref[...]
    w1 = route[:, ROUTE_W1:ROUTE_W1 + 1]
    w2 = route[:, ROUTE_W2:ROUTE_W2 + 1]
    out = h_ref[...] + w1 * buf[slot, 0] + w2 * buf[slot, 1]
    if final_norm:
        out = out * _rms_scale(out) * g_ref[...]
    o_ref[...] = out


def _combine(h, route, ys, pos1, pos2, final_gain=None):
    m, k = h.shape
    tm = COMBINE_TOKENS
    final_norm = final_gain is not None
    in_specs = [
        pl.BlockSpec((tm, k), lambda i, *_: (i, 0)),
        pl.BlockSpec((tm, LANES), lambda i, *_: (i, 0)),
        pl.BlockSpec(memory_space=pl.ANY),
    ]
    operands = [h, route, ys]
    if final_norm:
        in_specs.append(pl.BlockSpec((1, k), lambda i, *_: (0, 0)))
        operands.append(final_gain.reshape(1, k))
    return pl.pallas_call(
        functools.partial(_combine_kernel, final_norm=final_norm),
        out_shape=jax.ShapeDtypeStruct((m, k), F32),
        grid_spec=pltpu.PrefetchScalarGridSpec(
            num_scalar_prefetch=2,
            grid=(m // tm,),
            in_specs=in_specs,
            out_specs=pl.BlockSpec((tm, k), lambda i, *_: (i, 0)),
            scratch_shapes=[pltpu.VMEM((2, 2, tm, k), F32), pltpu.SemaphoreType.DMA((2,))],
        ),
        compiler_params=_params("arbitrary"),
        name="moe_combine",
    )(pos1, pos2, *operands)


def _work_list(counts, n_rows):
    tm = EXPERT_TILE
    n_tiles = n_rows // tm
    n_items = n_tiles + N_EXPERTS - 1
    i32 = jnp.int32
    count_le = lambda sorted_vals, q: jnp.sum((sorted_vals[None, :] <= q[:, None]).astype(i32), axis=1)
    pick = lambda table, idx: jnp.sum(
        jnp.where(idx[:, None] == jnp.arange(table.shape[0], dtype=i32)[None, :], table[None, :], 0), axis=1)
    e_ids = jnp.arange(N_EXPERTS, dtype=i32)
    ends = jnp.sum(jnp.where(e_ids[None, :] <= e_ids[:, None], counts[None, :], 0), axis=1)
    starts = ends - counts
    tile_ids = jnp.arange(n_tiles, dtype=i32)
    tile_lo = tile_ids * tm
    e_first = count_le(ends, tile_lo)
    e_last = count_le(ends, tile_lo + (tm - 1))
    per_tile = e_last - e_first + 1
    item0 = jnp.sum(jnp.where(tile_ids[None, :] < tile_ids[:, None], per_tile[None, :], 0), axis=1)
    total = jnp.sum(per_tile)
    n = jnp.arange(n_items, dtype=i32)
    valid = n < total
    tile = jnp.where(valid, count_le(item0, n) - 1, n_tiles - 1)
    expert = jnp.where(valid, pick(e_first, tile) + (n - pick(item0, tile)), e_last[-1])
    expert = jnp.clip(expert, 0, N_EXPERTS - 1)
    lo = jnp.clip(pick(starts, expert) - tile * tm, 0, tm)
    hi = jnp.clip(pick(ends, expert) - tile * tm, 0, tm)
    first = (n == pick(item0, tile)) & valid
    as_i32 = lambda a: a.astype(i32)
    return tuple(map(as_i32, (tile, expert, lo, hi, first, valid))), starts


def _router_weights(w_coarse, b_coarse, w_fine, b_fine):
    pad = LANES - N_GROUPS - N_EXPERTS
    w = jnp.concatenate([w_coarse, w_fine.reshape(D_MODEL, N_EXPERTS),
                         jnp.zeros((D_MODEL, pad), F32)], axis=1)
    b = jnp.concatenate([b_coarse, b_fine.reshape(N_EXPERTS), jnp.zeros((pad,), F32)])
    return w, b.reshape(1, LANES)


def _moe(h, norm_g, w_coarse, b_coarse, w_fine, b_fine, w_gate, w_up, w_down, layer, final_gain=None):
    w_r, b_r = _router_weights(w_coarse, b_coarse, w_fine, b_fine)
    hn, route, counts = _router(h, norm_g, w_r, b_r, tm=512)
    counts = counts[0, ROUTER_FINE0:ROUTER_FINE0 + N_EXPERTS].astype(jnp.int32)
    work, starts = _work_list(counts, 2 * h.shape[0])
    ids = route[:, :ROUTE_R2 + 1].astype(jnp.int32)
    e_ids = jnp.arange(N_EXPERTS, dtype=jnp.int32)[None, :]
    start_of = lambda e: jnp.sum(jnp.where(e[:, None] == e_ids, starts[None, :], 0), axis=1)
    pos1 = start_of(ids[:, ROUTE_E1]) + ids[:, ROUTE_R1]
    pos2 = start_of(ids[:, ROUTE_E2]) + ids[:, ROUTE_R2]
    xs = _dispatch(hn, pos1, pos2)
    ys = _experts(work, xs, w_gate, w_up, w_down, layer)
    return _combine(h, route, ys, pos1, pos2, final_gain)


def kernel(x, mix_norm, ffn_norm, ssm_w_in, ssm_conv_w, ssm_conv_b, ssm_dt_bias, ssm_a_log, ssm_d,
           ssm_norm_w, ssm_w_out, kv_norm, w_k, w_v, sb_w_q, sb_w_out, moe_w_coarse, moe_b_coarse,
           moe_w_fine, moe_b_fine, moe_w_gate, moe_w_up, moe_w_down, final_norm):
    batch, seq, d = x.shape
    h = x.reshape(batch * seq, d)

    zx, dt = _in_proj(h, mix_norm[0], ssm_w_in)
    per_head = lambda p: jnp.pad(p, (0, LANES - SSM_HEADS)).reshape(1, LANES)
    d_skip = jnp.repeat(ssm_d[0], SSM_HEAD_DIM).reshape(1, D_INNER)
    yn = _ssd(zx, dt, ssm_conv_w[0], ssm_conv_b[0], per_head(ssm_dt_bias[0]), per_head(ssm_a_log[0]),
              d_skip, ssm_norm_w[0], batch=batch, seq=seq)
    h = _matmul_res(yn, ssm_w_out.astype(BF16), h, tm=1024, tn=512)
    h = _moe(h, ffn_norm[0], moe_w_coarse[0], moe_b_coarse[0], moe_w_fine[0], moe_b_fine[0],
             moe_w_gate, moe_w_up, moe_w_down, 0)

    w_kv = jnp.concatenate([w_k.astype(BF16), w_v.astype(BF16)], axis=1)
    kv = _norm_matmul(h, kv_norm, [w_kv], tm=1024, tn=1024, out_dtype=BF16)

    q = _norm_matmul(h, mix_norm[1], [sb_w_q.astype(BF16)], tm=1024, tn=1024, out_dtype=BF16)
    o = _attention(q, kv, batch=batch, seq=seq)
    h = _matmul_res(o, sb_w_out.astype(BF16), h, tm=1024, tn=512)
    h = _moe(h, ffn_norm[1], moe_w_coarse[1], moe_b_coarse[1], moe_w_fine[1], moe_b_fine[1],
             moe_w_gate, moe_w_up, moe_w_down, 1, final_gain=final_norm)
    return h.reshape(batch, seq, d)
```
